```python
import jax, jax.numpy as jnp
from jax import lax
import numpy as np

D_MODEL = 1024
BATCH = 4
SEQ = 8192
DEPTH = 1
DEC_BATCH = 128
DEC_SEQ = 1
PAST_LEN = 16384
PAGE_SIZE = 128

N_Q_HEADS = 8
N_KV_HEADS = 2
GROUP = N_Q_HEADS // N_KV_HEADS
HEAD_DIM = 64
WINDOW = 128
ATTN_Q_WIDTH = N_Q_HEADS * HEAD_DIM
KV_WIDTH = N_KV_HEADS * HEAD_DIM
N_GDN_HEADS = 4
GDN_DK = 128
GDN_DV = 128
GDN_QK_WIDTH = N_GDN_HEADS * GDN_DK
GDN_V_WIDTH = N_GDN_HEADS * GDN_DV
GDN_CONV_DIM = 2 * GDN_QK_WIDTH + GDN_V_WIDTH
CONV_WIDTH = 4
CHUNK = 64
MIX_WIDTH = ATTN_Q_WIDTH + GDN_V_WIDTH
IN_WIDTH = ATTN_Q_WIDTH + 2 * KV_WIDTH + 2 * GDN_QK_WIDTH + 2 * GDN_V_WIDTH + 2 * N_GDN_HEADS
D_FF = 2816
EPS = 1e-6

kernel_name = "hymba_swa_sink_alibi_gdn_macaron"


def _split_points():
    sizes = (ATTN_Q_WIDTH, KV_WIDTH, KV_WIDTH, GDN_QK_WIDTH, GDN_QK_WIDTH,
             GDN_V_WIDTH, GDN_V_WIDTH, N_GDN_HEADS, N_GDN_HEADS)
    pts, acc = [], 0
    for s in sizes[:-1]:
        acc += s
        pts.append(acc)
    return pts


def rmsnorm(x, g):
    xf = x.astype(jnp.float32)
    y = xf * lax.rsqrt(jnp.mean(xf * xf, axis=-1, keepdims=True) + EPS)
    return (y * g.astype(jnp.float32)).astype(x.dtype)


def l2norm(x):
    return x * lax.rsqrt(jnp.sum(x * x, axis=-1, keepdims=True) + EPS)


def swiglu(x, w_in, w_out):
    gate, up = jnp.split(x @ w_in, 2, axis=-1)
    return (jax.nn.silu(gate) * up) @ w_out


def alibi_slopes():
    h = jnp.arange(1, N_Q_HEADS + 1, dtype=jnp.float32)
    return jnp.exp2(-8.0 * h / N_Q_HEADS).reshape(N_KV_HEADS, GROUP)


def band_attention(q, k, v, dist, valid, sinks, slopes):
    s = jnp.einsum("...qhgd,...khd->...hgqk", q, k).astype(jnp.float32) * (HEAD_DIM ** -0.5)
    s = s - slopes[:, :, None, None] * dist[..., None, None, :, :].astype(jnp.float32)
    s = jnp.where(valid[..., None, None, :, :], s, -jnp.inf)
    sink = jnp.broadcast_to(sinks.astype(jnp.float32)[:, :, None, None], s.shape[:-1] + (1,))
    p = jax.nn.softmax(jnp.concatenate([s, sink], axis=-1), axis=-1)[..., :-1]
    return jnp.einsum("...hgqk,...khd->...qhgd", p.astype(v.dtype), v)


def swa_prompt(q, k, v, sinks, slopes):
    B, T, KV, G, HD = q.shape
    nb = T // WINDOW
    qb = q.reshape(B, nb, WINDOW, KV, G, HD)
    pad = jnp.zeros((B, WINDOW, KV, HD), k.dtype)

    def band(t):
        tp = jnp.concatenate([pad, t], axis=1).reshape(B, nb + 1, WINDOW, KV, HD)
        return jnp.concatenate([tp[:, :-1], tp[:, 1:]], axis=2)

    r = jnp.arange(WINDOW)[:, None]
    c = jnp.arange(2 * WINDOW)[None, :]
    dist = r - c + WINDOW
    blk = jnp.arange(nb)[:, None, None]
    valid = (dist >= 0) & (dist <= WINDOW) & (blk * WINDOW + c - WINDOW >= 0)
    o = band_attention(qb, band(k), band(v), dist, valid, sinks, slopes)
    return o.reshape(B, T, KV * G * HD)


def swa_sample(q, k_new, v_new, k_hist, v_hist, sinks, slopes):
    DB, T, KV, G, HD = q.shape
    wb = k_hist.shape[1]
    kk = jnp.concatenate([k_hist, k_new], axis=1)
    vv = jnp.concatenate([v_hist, v_new], axis=1)
    dist = jnp.arange(T)[:, None] - (jnp.arange(wb + T)[None, :] - wb)
    valid = (dist >= 0) & (dist <= WINDOW)
    o = band_attention(q, kk, vv, dist, valid, sinks, slopes)
    return o.reshape(DB, T, KV * G * HD), kk[:, -wb:], vv[:, -wb:]


def gdn_chunked(q, k, v, g, beta, S0):
    B, T, H, _ = q.shape
    dv = v.shape[-1]
    n = T // CHUNK

    def blk(t):
        return jnp.moveaxis(t.reshape((B, n, CHUNK, H) + t.shape[3:]), 3, 1)

    q, k, v, g, beta = blk(q), blk(k), blk(v), blk(g), blk(beta)
    gc = jnp.cumsum(g, axis=-1)
    idx = jnp.arange(CHUNK)
    tril = idx[:, None] >= idx[None, :]
    strict = idx[:, None] > idx[None, :]
    decay = jnp.exp(jnp.where(tril, gc[..., :, None] - gc[..., None, :], -jnp.inf))
    kk = jnp.einsum("bhnid,bhnjd->bhnij", k, k)
    A = jnp.where(strict, beta[..., None] * kk * decay, 0.0)
    eye = jnp.eye(CHUNK, dtype=A.dtype)
    Tm = lax.linalg.triangular_solve(eye + A, jnp.broadcast_to(eye, A.shape),
                                     left_side=True, lower=True, unit_diagonal=True)
    u = Tm @ (v * beta[..., None])
    w = Tm @ (k * (beta * jnp.exp(gc))[..., None])
    qk = jnp.where(tril, jnp.einsum("bhnid,bhnjd->bhnij", q, k) * decay, 0.0)
    q_dec = q * jnp.exp(gc)[..., None]
    k_dec = k * jnp.exp(gc[..., -1:] - gc)[..., None]
    g_last = jnp.exp(gc[..., -1])

    def step(S, xs):
        u_c, w_c, qk_c, qd_c, kd_c, gl_c = xs
        v_new = u_c - w_c @ S
        o = qd_c @ S + qk_c @ v_new
        S = S * gl_c[..., None, None] + jnp.swapaxes(kd_c, -1, -2) @ v_new
        return S, o

    xs = tuple(jnp.moveaxis(t, 2, 0) for t in (u, w, qk, q_dec, k_dec, g_last))
    S, o = lax.scan(step, S0, xs)
    o = jnp.moveaxis(jnp.moveaxis(o, 0, 2), 1, 3).reshape(B, T, H, dv)
    return o, S


def gdn_recurrent(q, k, v, g, beta, S0):
    def step(S, xs):
        q_t, k_t, v_t, g_t, b_t = xs
        S = S * jnp.exp(g_t)[..., None, None]
        kS = jnp.einsum("bhk,bhkv->bhv", k_t, S)
        S = S + jnp.einsum("bhk,bhv->bhkv", k_t, b_t[..., None] * (v_t - kS))
        return S, jnp.einsum("bhk,bhkv->bhv", q_t, S)

    xs = tuple(jnp.moveaxis(t, 1, 0) for t in (q, k, v, g, beta))
    S, o = lax.scan(step, S0, xs)
    return jnp.moveaxis(o, 0, 1), S


def token_mix(n, k_hist, v_hist, conv_hist, S0, w_in_mix, attn_sinks, conv_w,
              gdn_A_log, gdn_dt_bias, gdn_norm_g, w_out_mix):
    B, T, _ = n.shape
    aq, ak, av, gq, gk, gv, gz, gb, ga = jnp.split(n @ w_in_mix, _split_points(), axis=-1)
    slopes = alibi_slopes()
    sinks = attn_sinks.reshape(N_KV_HEADS, GROUP)
    q = aq.reshape(B, T, N_KV_HEADS, GROUP, HEAD_DIM)
    k = ak.reshape(B, T, N_KV_HEADS, HEAD_DIM)
    v = av.reshape(B, T, N_KV_HEADS, HEAD_DIM)
    if k_hist is None:
        attn_out = swa_prompt(q, k, v, sinks, slopes)
        wb = min(WINDOW, T)
        new_k, new_v = k[:, -wb:], v[:, -wb:]
        hist = jnp.zeros((B, CONV_WIDTH - 1, GDN_CONV_DIM), n.dtype)
    else:
        attn_out, new_k, new_v = swa_sample(q, k, v, k_hist, v_hist, sinks, slopes)
        hist = conv_hist
    xh = jnp.concatenate([hist, jnp.concatenate([gq, gk, gv], axis=-1)], axis=1)
    conv = conv_w[0] * xh[:, 0:T]
    for j in range(1, CONV_WIDTH):
        conv = conv + conv_w[j] * xh[:, j:j + T]
    new_conv = xh[:, -(CONV_WIDTH - 1):]
    act = jax.nn.silu(conv).astype(jnp.float32)
    cq, ck, cv = jnp.split(act, [GDN_QK_WIDTH, 2 * GDN_QK_WIDTH], axis=-1)
    dq = l2norm(cq.reshape(B, T, N_GDN_HEADS, GDN_DK)) * (GDN_DK ** -0.5)
    dk = l2norm(ck.reshape(B, T, N_GDN_HEADS, GDN_DK))
    dvv = cv.reshape(B, T, N_GDN_HEADS, GDN_DV)
    beta = jax.nn.sigmoid(gb.astype(jnp.float32))
    g = -jnp.exp(gdn_A_log.astype(jnp.float32)) * jax.nn.softplus(
        ga.astype(jnp.float32) + gdn_dt_bias.astype(jnp.float32))
    if S0 is None:
        o, new_S = gdn_chunked(dq, dk, dvv, g, beta,
                               jnp.zeros((B, N_GDN_HEADS, GDN_DK, GDN_DV), jnp.float32))
    else:
        o, new_S = gdn_recurrent(dq, dk, dvv, g, beta, S0.astype(jnp.float32))
    o = rmsnorm(o, gdn_norm_g) * jax.nn.silu(gz.astype(jnp.float32).reshape(B, T, N_GDN_HEADS, GDN_DV))
    gdn_out = o.reshape(B, T, GDN_V_WIDTH).astype(n.dtype)
    mix = jnp.concatenate([attn_out, gdn_out], axis=-1) @ w_out_mix
    return mix, new_k, new_v, new_conv, new_S.astype(n.dtype)


def decoder_layer(x, k_hist, v_hist, conv_hist, S0, ffn1_norm_g, ffn1_w_in, ffn1_w_out,
                  mix_norm_g, w_in_mix, attn_sinks, conv_w, gdn_A_log, gdn_dt_bias,
                  gdn_norm_g, w_out_mix, ffn2_norm_g, ffn2_w_in, ffn2_w_out):
    x = x + 0.5 * swiglu(rmsnorm(x, ffn1_norm_g), ffn1_w_in, ffn1_w_out)
    mix, new_k, new_v, new_conv, new_S = token_mix(
        rmsnorm(x, mix_norm_g), k_hist, v_hist, conv_hist, S0, w_in_mix, attn_sinks, conv_w,
        gdn_A_log, gdn_dt_bias, gdn_norm_g, w_out_mix)
    x = x + mix
    x = x + 0.5 * swiglu(rmsnorm(x, ffn2_norm_g), ffn2_w_in, ffn2_w_out)
    return x, new_k, new_v, new_conv, new_S


def setup_inputs(seed: int = 0) -> dict:
    key = jax.random.key(seed)
    ks = jax.random.split(key, 24)
    f32 = jnp.float32
    wb = min(WINDOW, PAST_LEN)
    nrm = lambda k, shape, scale: jax.random.normal(k, shape, f32) * scale
    gain = lambda k, shape: 1.0 + 0.02 * jax.random.normal(k, shape, f32)
    dt = jnp.exp(jax.random.uniform(ks[0], (DEPTH, N_GDN_HEADS), f32, np.log(1e-3), np.log(1e-1)))
    return {
        "x_prompt": nrm(ks[1], (BATCH, SEQ, D_MODEL), 1.0),
        "x_sample": nrm(ks[2], (DEC_BATCH, DEC_SEQ, D_MODEL), 1.0),
        "cache_attn_k": nrm(ks[3], (DEPTH, DEC_BATCH, wb, N_KV_HEADS, HEAD_DIM), 1.0),
        "cache_attn_v": nrm(ks[4], (DEPTH, DEC_BATCH, wb, N_KV_HEADS, HEAD_DIM), 1.0),
        "state_conv": nrm(ks[5], (DEPTH, DEC_BATCH, CONV_WIDTH - 1, GDN_CONV_DIM), 1.0),
        "state_gdn": nrm(ks[6], (DEPTH, DEC_BATCH, N_GDN_HEADS, GDN_DK, GDN_DV), GDN_DK ** -0.5),
        "ffn1_norm_g": gain(ks[7], (DEPTH, D_MODEL)),
        "ffn1_w_in": nrm(ks[8], (DEPTH, D_MODEL, 2 * D_FF), D_MODEL ** -0.5),
        "ffn1_w_out": nrm(ks[9], (DEPTH, D_FF, D_MODEL), D_FF ** -0.5),
        "mix_norm_g": gain(ks[10], (DEPTH, D_MODEL)),
        "w_in_mix": nrm(ks[11], (DEPTH, D_MODEL, IN_WIDTH), D_MODEL ** -0.5),
        "attn_sinks": nrm(ks[12], (DEPTH, N_Q_HEADS), 0.5),
        "conv_w": nrm(ks[13], (DEPTH, CONV_WIDTH, GDN_CONV_DIM), CONV_WIDTH ** -0.5),
        "gdn_A_log": jnp.log(jax.random.uniform(ks[14], (DEPTH, N_GDN_HEADS), f32, 1.0, 16.0)),
        "gdn_dt_bias": dt + jnp.log(-jnp.expm1(-dt)),
        "gdn_norm_g": gain(ks[15], (DEPTH, GDN_DV)),
        "w_out_mix": nrm(ks[16], (DEPTH, MIX_WIDTH, D_MODEL), MIX_WIDTH ** -0.5),
        "ffn2_norm_g": gain(ks[17], (DEPTH, D_MODEL)),
        "ffn2_w_in": nrm(ks[18], (DEPTH, D_MODEL, 2 * D_FF), D_MODEL ** -0.5),
        "ffn2_w_out": nrm(ks[19], (DEPTH, D_FF, D_MODEL), D_FF ** -0.5),
        "final_norm_g": gain(ks[20], (D_MODEL,)),
    }


def reference(x_prompt, x_sample, cache_attn_k, cache_attn_v, state_conv, state_gdn,
              ffn1_norm_g, ffn1_w_in, ffn1_w_out, mix_norm_g, w_in_mix, attn_sinks, conv_w,
              gdn_A_log, gdn_dt_bias, gdn_norm_g, w_out_mix, ffn2_norm_g, ffn2_w_in, ffn2_w_out,
              final_norm_g):
    xp, xs = x_prompt, x_sample
    kp_l, vp_l, cp_l, sp_l, ks_l, vs_l, cs_l, ss_l = [], [], [], [], [], [], [], []
    for l in range(DEPTH):
        lw = (ffn1_norm_g[l], ffn1_w_in[l], ffn1_w_out[l], mix_norm_g[l], w_in_mix[l],
              attn_sinks[l], conv_w[l], gdn_A_log[l], gdn_dt_bias[l], gdn_norm_g[l],
              w_out_mix[l], ffn2_norm_g[l], ffn2_w_in[l], ffn2_w_out[l])
        xp, kp, vp, cp, sp = decoder_layer(xp, None, None, None, None, *lw)
        xs, kss, vss, css, sss = decoder_layer(xs, cache_attn_k[l], cache_attn_v[l],
                                               state_conv[l], state_gdn[l], *lw)
        kp_l.append(kp); vp_l.append(vp); cp_l.append(cp); sp_l.append(sp)
        ks_l.append(kss); vs_l.append(vss); cs_l.append(css); ss_l.append(sss)
    y_prompt = rmsnorm(xp, final_norm_g)
    y_sample = rmsnorm(xs, final_norm_g)
    new_k_prompt = jnp.stack(kp_l)
    new_v_prompt = jnp.stack(vp_l)
    new_conv_prompt = jnp.stack(cp_l)
    new_gdn_prompt = jnp.stack(sp_l)
    new_k_sample = jnp.stack(ks_l)
    new_v_sample = jnp.stack(vs_l)
    new_conv_sample = jnp.stack(cs_l)
    new_gdn_sample = jnp.stack(ss_l)
    return (y_prompt, y_sample, new_k_prompt, new_v_prompt, new_conv_prompt, new_gdn_prompt,
            new_k_sample, new_v_sample, new_conv_sample, new_gdn_sample)
```

```python
import functools

import jax
import jax.numpy as jnp
from jax import lax
from jax.experimental import pallas as pl
from jax.experimental.pallas import tpu as pltpu

F32 = jnp.float32
BF16 = jnp.bfloat16

D_MODEL = 1024
D_FF = 2816
N_Q_HEADS = 8
N_KV_HEADS = 2
GROUP = N_Q_HEADS // N_KV_HEADS
HEAD_DIM = 64
WINDOW = 128
ATTN_Q_WIDTH = N_Q_HEADS * HEAD_DIM
KV_WIDTH = N_KV_HEADS * HEAD_DIM
N_GDN_HEADS = 4
GDN_DK = 128
GDN_DV = 128
GDN_QK_WIDTH = N_GDN_HEADS * GDN_DK
GDN_V_WIDTH = N_GDN_HEADS * GDN_DV
GDN_CONV_DIM = 2 * GDN_QK_WIDTH + GDN_V_WIDTH
CONV_WIDTH = 4
CHUNK = 64
EPS = 1e-6

LANES = 128
SUBLANES = 8
VMEM_LIMIT = 56 * 1024 * 1024

P_GQKV = 0
P_AQ = P_GQKV + GDN_CONV_DIM
P_GZ = P_AQ + ATTN_Q_WIDTH
P_KV = P_GZ + GDN_V_WIDTH
P_GBA = P_KV + 2 * KV_WIDTH
P_WIDTH = P_GBA + LANES

FF_CHUNK = 256
TOKEN_TILE = 256
GDN_TILE = 256
SAMPLE_BLOCK = 8


def _sigmoid(x):
    return 1.0 / (1.0 + jnp.exp(-x))


def _silu(x):
    return x * _sigmoid(x)


def _softplus(x):
    return jnp.maximum(x, 0.0) + jnp.log1p(jnp.exp(-jnp.abs(x)))


def _rms(x, g):
    return x * lax.rsqrt(jnp.mean(x * x, axis=-1, keepdims=True) + EPS) * g


def _dot(a, b):
    return jnp.dot(a.astype(BF16), b.astype(BF16), preferred_element_type=F32)


def _dot_nt(a, b):
    return lax.dot_general(a.astype(BF16), b.astype(BF16), (((1,), (1,)), ((), ())),
                           preferred_element_type=F32)


def _dot_exact(a, b):
    return jnp.dot(a, b, preferred_element_type=F32, precision=lax.Precision.HIGHEST)


def _resident(shape):
    return pl.BlockSpec(shape, lambda *_: (0,) * len(shape), pipeline_mode=pl.Buffered(1))


def _params(*semantics):
    return pltpu.CompilerParams(dimension_semantics=semantics, vmem_limit_bytes=VMEM_LIMIT)


def _ffn_half_step(x, g, wgu_ref, wo_ref):
    n = _rms(x, g).astype(BF16)
    acc = jnp.zeros(x.shape, F32)
    for c in range(D_FF // FF_CHUNK):
        lo = c * FF_CHUNK
        gate = jnp.dot(n, wgu_ref[:, lo:lo + FF_CHUNK], preferred_element_type=F32)
        up = jnp.dot(n, wgu_ref[:, D_FF + lo:D_FF + lo + FF_CHUNK], preferred_element_type=F32)
        h = (_silu(gate) * up).astype(BF16)
        acc = acc + jnp.dot(h, wo_ref[lo:lo + FF_CHUNK, :], preferred_element_type=F32)
    return x + 0.5 * acc


def _ffn_mix_in_kernel(x_ref, g1_ref, wgu_ref, wo_ref, gm_ref, win_ref, x1_ref, p_ref):
    x1 = _ffn_half_step(x_ref[...], g1_ref[...], wgu_ref, wo_ref)
    x1_ref[...] = x1
    n = _rms(x1, gm_ref[...]).astype(BF16)
    for lo in range(0, P_WIDTH, 512):
        hi = min(lo + 512, P_WIDTH)
        p_ref[:, lo:hi] = jnp.dot(n, win_ref[:, lo:hi], preferred_element_type=F32)


def _ffn_mix_in(x, g1, wgu, wo, gm, win, tm):
    n_tok = x.shape[0]
    row = lambda w: pl.BlockSpec((tm, w), lambda i: (i, 0))
    return pl.pallas_call(
        _ffn_mix_in_kernel,
        grid=(n_tok // tm,),
        in_specs=[row(D_MODEL), _resident((1, D_MODEL)), _resident((D_MODEL, 2 * D_FF)),
                  _resident((D_FF, D_MODEL)), _resident((1, D_MODEL)), _resident((D_MODEL, P_WIDTH))],
        out_specs=[row(D_MODEL), row(P_WIDTH)],
        out_shape=[jax.ShapeDtypeStruct((n_tok, D_MODEL), F32),
                   jax.ShapeDtypeStruct((n_tok, P_WIDTH), F32)],
        compiler_params=_params("parallel"),
        name="ffn_mix_in",
    )(x, g1, wgu, wo, gm, win)


def _mix_out_ffn_kernel(x1_ref, attn_ref, gdn_ref, wa_ref, wb_ref, g2_ref, wgu_ref, wo_ref, gf_ref,
                        y_ref):
    x2 = x1_ref[...] + _dot(attn_ref[...], wa_ref[...]) + _dot(gdn_ref[...], wb_ref[...])
    x3 = _ffn_half_step(x2, g2_ref[...], wgu_ref, wo_ref)
    y_ref[...] = _rms(x3, gf_ref[...])


def _mix_out_ffn(x1, attn, gdn, wa, wb, g2, wgu, wo, gf, tm):
    n_tok = x1.shape[0]
    row = lambda w: pl.BlockSpec((tm, w), lambda i: (i, 0))
    return pl.pallas_call(
        _mix_out_ffn_kernel,
        grid=(n_tok // tm,),
        in_specs=[row(D_MODEL), row(ATTN_Q_WIDTH), row(GDN_V_WIDTH),
                  _resident((ATTN_Q_WIDTH, D_MODEL)), _resident((GDN_V_WIDTH, D_MODEL)),
                  _resident((1, D_MODEL)), _resident((D_MODEL, 2 * D_FF)),
                  _resident((D_FF, D_MODEL)), _resident((1, D_MODEL))],
        out_specs=row(D_MODEL),
        out_shape=jax.ShapeDtypeStruct((n_tok, D_MODEL), F32),
        compiler_params=_params("parallel"),
        name="mix_out_ffn",
    )(x1, attn, gdn, wa, wb, g2, wgu, wo, gf)


def _alibi_slope(head):
    return 2.0 ** (-8.0 * (head + 1) / N_Q_HEADS)


def _swa_prompt_kernel(sink_ref, q_ref, kvc_ref, kvp_ref, o_ref):
    blk = pl.program_id(1)
    q = q_ref[...].astype(BF16)
    kv = jnp.concatenate([kvp_ref[...], kvc_ref[...]], axis=0)
    r = lax.broadcasted_iota(jnp.int32, (WINDOW, 2 * WINDOW), 0)
    c = lax.broadcasted_iota(jnp.int32, (WINDOW, 2 * WINDOW), 1)
    dist = r - c + WINDOW
    valid = (dist >= 0) & (dist <= WINDOW) & ((c >= WINDOW) | (blk > 0))
    distf = dist.astype(F32)
    for h in range(N_Q_HEADS):
        kvh = h // GROUP
        qh = q[:, h * HEAD_DIM:(h + 1) * HEAD_DIM]
        kh = kv[:, kvh * HEAD_DIM:(kvh + 1) * HEAD_DIM]
        vh = kv[:, KV_WIDTH + kvh * HEAD_DIM:KV_WIDTH + (kvh + 1) * HEAD_DIM]
        s = _dot_nt(qh, kh) * (HEAD_DIM ** -0.5) - _alibi_slope(h) * distf
        s = jnp.where(valid, s, -jnp.inf)
        sink = sink_ref[h]
        m = jnp.maximum(jnp.max(s, axis=-1, keepdims=True), sink)
        p = jnp.exp(s - m)
        denom = jnp.sum(p, axis=-1, keepdims=True) + jnp.exp(sink - m)
        o_ref[:, h * HEAD_DIM:(h + 1) * HEAD_DIM] = _dot(p, vh) / denom


def _swa_prompt(sinks, p3):
    batch, seq, _ = p3.shape
    nblk = seq // WINDOW
    q_blk = P_AQ // ATTN_Q_WIDTH
    kv_blk = P_KV // (2 * KV_WIDTH)
    return pl.pallas_call(
        _swa_prompt_kernel,
        grid=(batch, nblk),
        in_specs=[pl.BlockSpec(memory_space=pltpu.SMEM),
                  pl.BlockSpec((None, WINDOW, ATTN_Q_WIDTH), lambda b, i: (b, i, q_blk)),
                  pl.BlockSpec((None, WINDOW, 2 * KV_WIDTH), lambda b, i: (b, i, kv_blk)),
                  pl.BlockSpec((None, WINDOW, 2 * KV_WIDTH),
                               lambda b, i: (b, jnp.maximum(i - 1, 0), kv_blk))],
        out_specs=pl.BlockSpec((None, WINDOW, ATTN_Q_WIDTH), lambda b, i: (b, i, 0)),
        out_shape=jax.ShapeDtypeStruct((batch, seq, ATTN_Q_WIDTH), F32),
        compiler_params=_params("parallel", "parallel"),
        name="swa_prompt",
    )(sinks, p3, p3, p3)


def _gdn_gates(gba, alog, dtb):
    beta = _sigmoid(gba)
    g = -jnp.exp(alog) * _softplus(gba + dtb)
    return beta, g


def _l2norm(x):
    return x * lax.rsqrt(jnp.sum(x * x, axis=-1, keepdims=True) + EPS)


def _unit_lower_inverse(a, xor_idx, eye):
    x = eye - jnp.where(xor_idx < 2, a, 0.0)
    s = 4
    while s <= CHUNK:
        r = jnp.where((xor_idx >= s // 2) & (xor_idx < s), a, 0.0)
        x = x - _dot(_dot(x, r), x)
        s *= 2
    return x


def _gdn_prompt_kernel(raw_ref, prev_ref, gz_ref, gba_ref, convw_ref, alog_ref, dtb_ref, ng_ref,
                       o_ref, s_ref):
    t = pl.program_id(1)
    tt = GDN_TILE
    nchunk = tt // CHUNK

    @pl.when(t == 0)
    def _():
        s_ref[...] = jnp.zeros(s_ref.shape, F32)

    raw = raw_ref[...]
    prev = jnp.where(t > 0, prev_ref[...], 0.0)
    ext = jnp.concatenate([prev, raw], axis=0)
    w = convw_ref[...]
    base = SUBLANES - (CONV_WIDTH - 1)
    conv = w[0:1] * ext[base:base + tt]
    for j in range(1, CONV_WIDTH - 1):
        conv = conv + w[j:j + 1] * ext[base + j:base + j + tt]
    conv = conv + w[CONV_WIDTH - 1:CONV_WIDTH] * raw
    act = _silu(conv)

    beta_all, g_all = _gdn_gates(gba_ref[...], alog_ref[...], dtb_ref[...])

    r = lax.broadcasted_iota(jnp.int32, (tt, tt), 0)
    c = lax.broadcasted_iota(jnp.int32, (tt, tt), 1)
    xor_idx = r ^ c
    same_chunk = xor_idx < CHUNK
    tril = same_chunk & (r >= c)
    strict = same_chunk & (r > c)
    eye = (r == c).astype(F32)
    gc = _dot_exact(tril.astype(F32), g_all)
    gtot = _dot_exact(same_chunk.astype(F32), g_all)
    gc_t = _dot_exact(g_all.T, (same_chunk & (c >= r)).astype(F32))

    for h in range(N_GDN_HEADS):
        lane = slice(h * GDN_DK, (h + 1) * GDN_DK)
        q = _l2norm(act[:, lane]) * (GDN_DK ** -0.5)
        k = _l2norm(act[:, GDN_QK_WIDTH + h * GDN_DK:GDN_QK_WIDTH + (h + 1) * GDN_DK])
        v = act[:, 2 * GDN_QK_WIDTH + h * GDN_DV:2 * GDN_QK_WIDTH + (h + 1) * GDN_DV]
        beta = beta_all[:, h:h + 1]
        gcol = gc[:, N_GDN_HEADS + h:N_GDN_HEADS + h + 1]
        gend = gtot[:, N_GDN_HEADS + h:N_GDN_HEADS + h + 1]
        grow = gc_t[N_GDN_HEADS + h:N_GDN_HEADS + h + 1, :]
        decay = jnp.exp(jnp.where(tril, gcol - grow, -jnp.inf))
        kb = k.astype(BF16)
        a = jnp.where(strict, beta * _dot_nt(kb, kb) * decay, 0.0)
        tm = _unit_lower_inverse(a, xor_idx, eye)
        eg = jnp.exp(gcol)
        uw = _dot(tm, jnp.concatenate([v * beta, k * (beta * eg)], axis=1))
        u = uw[:, :GDN_DV]
        wmat = uw[:, GDN_DV:]
        qk = jnp.where(tril, _dot_nt(q, kb) * decay, 0.0)
        q_dec = q * eg
        kd_t = (k * jnp.exp(gend - gcol)).T

        state = s_ref[h]
        v_new = []
        o_inter = []
        for ci in range(nchunk):
            rows = slice(ci * CHUNK, (ci + 1) * CHUNK)
            vn = u[rows] - _dot(wmat[rows], state)
            o_inter.append(_dot(q_dec[rows], state))
            g_last = jnp.exp(gend[ci * CHUNK:ci * CHUNK + 1, :])
            state = state * g_last + _dot(kd_t[:, rows], vn)
            v_new.append(vn)
        s_ref[h] = state
        o = jnp.concatenate(o_inter, axis=0) + _dot(qk, jnp.concatenate(v_new, axis=0))
        gz = gz_ref[:, h * GDN_DV:(h + 1) * GDN_DV]
        o_ref[:, h * GDN_DV:(h + 1) * GDN_DV] = _rms(o, ng_ref[...]) * _silu(gz)


def _gdn_prompt(p3, convw, alog, dtb, ng):
    batch, seq, _ = p3.shape
    tt = GDN_TILE
    prev_per_tile = tt // SUBLANES
    small = lambda shape: pl.BlockSpec(shape, lambda b, t: (0,) * len(shape))
    return pl.pallas_call(
        _gdn_prompt_kernel,
        grid=(batch, seq // tt),
        in_specs=[pl.BlockSpec((None, tt, GDN_CONV_DIM), lambda b, t: (b, t, P_GQKV // GDN_CONV_DIM)),
                  pl.BlockSpec((None, SUBLANES, GDN_CONV_DIM),
                               lambda b, t: (b, jnp.maximum(t * prev_per_tile - 1, 0),
                                             P_GQKV // GDN_CONV_DIM)),
                  pl.BlockSpec((None, tt, GDN_V_WIDTH), lambda b, t: (b, t, P_GZ // GDN_V_WIDTH)),
                  pl.BlockSpec((None, tt, LANES), lambda b, t: (b, t, P_GBA // LANES)),
                  small((CONV_WIDTH, GDN_CONV_DIM)), small((1, LANES)), small((1, LANES)),
                  small((1, GDN_DV))],
        out_specs=[pl.BlockSpec((None, tt, GDN_V_WIDTH), lambda b, t: (b, t, 0)),
                   pl.BlockSpec((None, N_GDN_HEADS, GDN_DK, GDN_DV), lambda b, t: (b, 0, 0, 0))],
        out_shape=[jax.ShapeDtypeStruct((batch, seq, GDN_V_WIDTH), F32),
                   jax.ShapeDtypeStruct((batch, N_GDN_HEADS, GDN_DK, GDN_DV), F32)],
        compiler_params=_params("parallel", "arbitrary"),
        name="gdn_prompt",
    )(p3, p3, p3, p3, convw, alog, dtb, ng)


def _sample_prep_kernel(p_ref, hist_ref, convw_ref, alog_ref, dtb_ref,
                        conv_ref, qt_ref, kt_ref, v_ref, beta_ref, decay_ref):
    raw = p_ref[:, P_GQKV:P_GQKV + GDN_CONV_DIM]
    w = convw_ref[...]
    conv = w[0:1] * hist_ref[:, 0:GDN_CONV_DIM]
    for j in range(1, CONV_WIDTH - 1):
        conv = conv + w[j:j + 1] * hist_ref[:, j * GDN_CONV_DIM:(j + 1) * GDN_CONV_DIM]
    conv = conv + w[CONV_WIDTH - 1:CONV_WIDTH] * raw
    act = _silu(conv)
    conv_ref[:, 0:(CONV_WIDTH - 2) * GDN_CONV_DIM] = hist_ref[:, GDN_CONV_DIM:]
    conv_ref[:, (CONV_WIDTH - 2) * GDN_CONV_DIM:] = raw
    v_ref[...] = act[:, 2 * GDN_QK_WIDTH:]
    beta, g = _gdn_gates(p_ref[:, P_GBA:P_GBA + LANES], alog_ref[...], dtb_ref[...])
    beta_ref[...] = beta
    decay_ref[...] = jnp.exp(g)
    nblk = qt_ref.shape[0]
    for h in range(N_GDN_HEADS):
        qt = (_l2norm(act[:, h * GDN_DK:(h + 1) * GDN_DK]) * (GDN_DK ** -0.5)).T
        kt = _l2norm(act[:, GDN_QK_WIDTH + h * GDN_DK:GDN_QK_WIDTH + (h + 1) * GDN_DK]).T
        for i in range(nblk):
            qt_ref[i, h] = qt[:, i * SAMPLE_BLOCK:(i + 1) * SAMPLE_BLOCK]
            kt_ref[i, h] = kt[:, i * SAMPLE_BLOCK:(i + 1) * SAMPLE_BLOCK]


def _sample_prep(p, hist, convw, alog, dtb):
    nseq = p.shape[0]
    nblk = nseq // SAMPLE_BLOCK
    hist_w = (CONV_WIDTH - 1) * GDN_CONV_DIM
    cols = jax.ShapeDtypeStruct((nblk, N_GDN_HEADS, GDN_DK, SAMPLE_BLOCK), F32)
    return pl.pallas_call(
        _sample_prep_kernel,
        out_shape=[jax.ShapeDtypeStruct((nseq, hist_w), F32), cols, cols,
                   jax.ShapeDtypeStruct((nseq, GDN_V_WIDTH), F32),
                   jax.ShapeDtypeStruct((nseq, LANES), F32),
                   jax.ShapeDtypeStruct((nseq, LANES), F32)],
        compiler_params=pltpu.CompilerParams(vmem_limit_bytes=VMEM_LIMIT),
        name="sample_prep",
    )(p, hist, convw, alog, dtb)


def _per_group(values):
    g = lax.broadcasted_iota(jnp.int32, (GROUP, 1), 0)
    col = jnp.full((GROUP, 1), values[GROUP - 1], F32)
    for i in range(GROUP - 2, -1, -1):
        col = jnp.where(g == i, values[i], col)
    return col


def _sample_mix_kernel(sink_ref, p_ref, kh_ref, vh_ref, qt_ref, kt_ref, v_ref, beta_ref, decay_ref,
                       s_ref, ng_ref, attn_ref, gdn_ref, nk_ref, nv_ref, ns_ref):
    key_pos = lax.broadcasted_iota(jnp.int32, (1, WINDOW), 1)
    dist_hist = (WINDOW - key_pos).astype(F32)
    for b in range(SAMPLE_BLOCK):
        k_hist = kh_ref[b]
        v_hist = vh_ref[b]
        k_new = p_ref[b:b + 1, P_KV:P_KV + KV_WIDTH]
        v_new = p_ref[b:b + 1, P_KV + KV_WIDTH:P_KV + 2 * KV_WIDTH]
        nk_ref[b] = jnp.concatenate([k_hist[1:], k_new], axis=0)
        nv_ref[b] = jnp.concatenate([v_hist[1:], v_new], axis=0)
        outs = []
        for kvh in range(N_KV_HEADS):
            heads = range(kvh * GROUP, (kvh + 1) * GROUP)
            qg = jnp.concatenate(
                [p_ref[b:b + 1, P_AQ + h * HEAD_DIM:P_AQ + (h + 1) * HEAD_DIM] for h in heads],
                axis=0)
            cols = slice(kvh * HEAD_DIM, (kvh + 1) * HEAD_DIM)
            slope = _per_group([_alibi_slope(h) for h in heads])
            sink = _per_group([sink_ref[h] for h in heads])
            scale = HEAD_DIM ** -0.5
            s_hist = _dot_nt(qg, k_hist[:, cols]) * scale - slope * dist_hist
            qb = qg.astype(BF16).astype(F32)
            kb = k_new[:, cols].astype(BF16).astype(F32)
            s_new = jnp.sum(qb * kb, axis=-1, keepdims=True) * scale
            m = jnp.maximum(jnp.maximum(jnp.max(s_hist, axis=-1, keepdims=True), s_new), sink)
            p_hist = jnp.exp(s_hist - m)
            p_new = jnp.exp(s_new - m)
            denom = jnp.sum(p_hist, axis=-1, keepdims=True) + p_new + jnp.exp(sink - m)
            o = (_dot(p_hist, v_hist[:, cols]) + p_new * v_new[:, cols]) / denom
            outs.extend(o[g:g + 1] for g in range(GROUP))
        attn_ref[b:b + 1, :] = jnp.concatenate(outs, axis=1)

    for b in range(SAMPLE_BLOCK):
        for h in range(N_GDN_HEADS):
            kcol = kt_ref[h, :, b:b + 1]
            qcol = qt_ref[h, :, b:b + 1]
            v = v_ref[b:b + 1, h * GDN_DV:(h + 1) * GDN_DV]
            beta = beta_ref[b:b + 1, h:h + 1]
            decay = decay_ref[b:b + 1, N_GDN_HEADS + h:N_GDN_HEADS + h + 1]
            state = s_ref[b, h] * decay
            ks = jnp.sum(state * kcol, axis=0, keepdims=True)
            state = state + kcol * (beta * (v - ks))
            ns_ref[b, h] = state
            o = jnp.sum(state * qcol, axis=0, keepdims=True)
            gz = p_ref[b:b + 1, P_GZ + h * GDN_DV:P_GZ + (h + 1) * GDN_DV]
            gdn_ref[b:b + 1, h * GDN_DV:(h + 1) * GDN_DV] = _rms(o, ng_ref[...]) * _silu(gz)


def _sample_mix(sinks, p, k_hist, v_hist, qt, kt, v, beta, decay, state, ng):
    nseq = p.shape[0]
    bb = SAMPLE_BLOCK
    rows = lambda w: pl.BlockSpec((bb, w), lambda i: (i, 0))
    cache = pl.BlockSpec((bb, WINDOW, KV_WIDTH), lambda i: (i, 0, 0))
    cols = pl.BlockSpec((None, N_GDN_HEADS, GDN_DK, bb), lambda i: (i, 0, 0, 0))
    st = pl.BlockSpec((bb, N_GDN_HEADS, GDN_DK, GDN_DV), lambda i: (i, 0, 0, 0))
    return pl.pallas_call(
        _sample_mix_kernel,
        grid=(nseq // bb,),
        in_specs=[pl.BlockSpec(memory_space=pltpu.SMEM), rows(P_WIDTH), cache, cache, cols, cols,
                  rows(GDN_V_WIDTH), rows(LANES), rows(LANES), st,
                  pl.BlockSpec((1, GDN_DV), lambda i: (0, 0))],
        out_specs=[rows(ATTN_Q_WIDTH), rows(GDN_V_WIDTH), cache, cache, st],
        out_shape=[jax.ShapeDtypeStruct((nseq, ATTN_Q_WIDTH), F32),
                   jax.ShapeDtypeStruct((nseq, GDN_V_WIDTH), F32),
                   jax.ShapeDtypeStruct(k_hist.shape, F32),
                   jax.ShapeDtypeStruct(v_hist.shape, F32),
                   jax.ShapeDtypeStruct(state.shape, F32)],
        compiler_params=_params("parallel"),
        name="sample_mix",
    )(sinks, p, k_hist, v_hist, qt, kt, v, beta, decay, state, ng)


def _reorder_in_mix(w):
    o_ak = ATTN_Q_WIDTH
    o_gq = o_ak + 2 * KV_WIDTH
    o_gz = o_gq + GDN_CONV_DIM
    o_gb = o_gz + GDN_V_WIDTH
    pad = jnp.zeros((w.shape[0], LANES - 2 * N_GDN_HEADS), w.dtype)
    return jnp.concatenate([w[:, o_gq:o_gz], w[:, :o_ak], w[:, o_gz:o_gb], w[:, o_ak:o_gq],
                            w[:, o_gb:], pad], axis=1)


def _lane_pad(vec, offset):
    return jnp.zeros((1, LANES), F32).at[0, offset:offset + vec.shape[0]].set(vec)


def kernel(x_prompt, x_sample, cache_attn_k, cache_attn_v, state_conv, state_gdn, ffn1_norm_g,
           ffn1_w_in, ffn1_w_out, mix_norm_g, w_in_mix, attn_sinks, conv_w, gdn_A_log, gdn_dt_bias,
           gdn_norm_g, w_out_mix, ffn2_norm_g, ffn2_w_in, ffn2_w_out, final_norm_g):
    depth = ffn1_w_in.shape[0]
    assert depth == 1, "single-layer trunk"
    batch, seq, _ = x_prompt.shape
    nseq = x_sample.shape[0]
    assert x_sample.shape[1] == 1 and cache_attn_k.shape[2] == WINDOW
    assert seq % GDN_TILE == 0 and seq % WINDOW == 0 and nseq % SAMPLE_BLOCK == 0
    l = 0
    row = lambda v: v.reshape(1, -1)
    g1, gm, g2, gf = row(ffn1_norm_g[l]), row(mix_norm_g[l]), row(ffn2_norm_g[l]), row(final_norm_g)
    ng = row(gdn_norm_g[l])
    wgu1, wo1 = ffn1_w_in[l].astype(BF16), ffn1_w_out[l].astype(BF16)
    wgu2, wo2 = ffn2_w_in[l].astype(BF16), ffn2_w_out[l].astype(BF16)
    win = _reorder_in_mix(w_in_mix[l]).astype(BF16)
    wa = w_out_mix[l, :ATTN_Q_WIDTH].astype(BF16)
    wb = w_out_mix[l, ATTN_Q_WIDTH:].astype(BF16)
    alog = _lane_pad(gdn_A_log[l], N_GDN_HEADS)
    dtb = _lane_pad(gdn_dt_bias[l], N_GDN_HEADS)
    sinks = attn_sinks[l]
    convw = conv_w[l]

    xp = x_prompt.reshape(batch * seq, D_MODEL)
    x1p, pp = _ffn_mix_in(xp, g1, wgu1, wo1, gm, win, TOKEN_TILE)
    pp3 = pp.reshape(batch, seq, P_WIDTH)
    attn_p = _swa_prompt(sinks, pp3)
    gdn_p, state_p = _gdn_prompt(pp3, convw, alog, dtb, ng)
    y_p = _mix_out_ffn(x1p, attn_p.reshape(batch * seq, ATTN_Q_WIDTH),
                       gdn_p.reshape(batch * seq, GDN_V_WIDTH), wa, wb, g2, wgu2, wo2, gf, TOKEN_TILE)
    tail = pp3[:, seq - WINDOW:, P_KV:P_KV + 2 * KV_WIDTH]
    new_k_p = tail[:, :, :KV_WIDTH].reshape(1, batch, WINDOW, N_KV_HEADS, HEAD_DIM)
    new_v_p = tail[:, :, KV_WIDTH:].reshape(1, batch, WINDOW, N_KV_HEADS, HEAD_DIM)
    new_conv_p = pp3[:, seq - (CONV_WIDTH - 1):, P_GQKV:P_GQKV + GDN_CONV_DIM][None]

    xs = x_sample.reshape(nseq, D_MODEL)
    x1s, ps = _ffn_mix_in(xs, g1, wgu1, wo1, gm, win, nseq)
    hist = state_conv[l].reshape(nseq, (CONV_WIDTH - 1) * GDN_CONV_DIM)
    new_conv_s, qt, kt, v_s, beta_s, decay_s = _sample_prep(ps, hist, convw, alog, dtb)
    attn_s, gdn_s, new_k_s, new_v_s, state_s = _sample_mix(
        sinks, ps, cache_attn_k[l].reshape(nseq, WINDOW, KV_WIDTH),
        cache_attn_v[l].reshape(nseq, WINDOW, KV_WIDTH), qt, kt, v_s, beta_s, decay_s,
        state_gdn[l], ng)
    y_s = _mix_out_ffn(x1s, attn_s, gdn_s, wa, wb, g2, wgu2, wo2, gf, nseq)

    kv_shape = (1, nseq, WINDOW, N_KV_HEADS, HEAD_DIM)
    return (y_p.reshape(batch, seq, D_MODEL), y_s.reshape(nseq, 1, D_MODEL),
            new_k_p, new_v_p, new_conv_p, state_p[None],
            new_k_s.reshape(kv_shape), new_v_s.reshape(kv_shape),
            new_conv_s.reshape(1, nseq, CONV_WIDTH - 1, GDN_CONV_DIM), state_s[None])
```

```python
import functools

import jax
import jax.numpy as jnp
from jax import lax
from jax.experimental import pallas as pl
from jax.experimental.pallas import tpu as pltpu

F32 = jnp.float32
BF16 = jnp.bfloat16

D_MODEL = 1024
D_FF = 2816
N_Q_HEADS = 8
N_KV_HEADS = 2
GROUP = N_Q_HEADS // N_KV_HEADS
HEAD_DIM = 64
WINDOW = 128
ATTN_Q_WIDTH = N_Q_HEADS * HEAD_DIM
KV_WIDTH = N_KV_HEADS * HEAD_DIM
N_GDN_HEADS = 4
GDN_DK = 128
GDN_DV = 128
GDN_QK_WIDTH = N_GDN_HEADS * GDN_DK
GDN_V_WIDTH = N_GDN_HEADS * GDN_DV
GDN_CONV_DIM = 2 * GDN_QK_WIDTH + GDN_V_WIDTH
CONV_WIDTH = 4
CHUNK = 64
EPS = 1e-6

LANES = 128
SUBLANES = 8
VMEM_LIMIT = 56 * 1024 * 1024

P_GQKV = 0
P_AQ = P_GQKV + GDN_CONV_DIM
P_GZ = P_AQ + ATTN_Q_WIDTH
P_KV = P_GZ + GDN_V_WIDTH
P_GBA = P_KV + 2 * KV_WIDTH
P_WIDTH = P_GBA + LANES

FF_CHUNK = 256
TOKEN_TILE = 512
GDN_TILE = 256
SWA_TILE = 512
SAMPLE_BLOCK = 8


def _sigmoid(x):
    return 1.0 / (1.0 + jnp.exp(-x))


def _silu(x):
    return x * _sigmoid(x)


def _softplus(x):
    return jnp.maximum(x, 0.0) + jnp.log1p(jnp.exp(-jnp.abs(x)))


def _rms(x, g):
    return x * lax.rsqrt(jnp.mean(x * x, axis=-1, keepdims=True) + EPS) * g


def _dot(a, b):
    return jnp.dot(a.astype(BF16), b.astype(BF16), preferred_element_type=F32)


def _dot_nt(a, b):
    return lax.dot_general(a.astype(BF16), b.astype(BF16), (((1,), (1,)), ((), ())),
                           preferred_element_type=F32)


def _dot_exact(a, b):
    return jnp.dot(a, b, preferred_element_type=F32, precision=lax.Precision.HIGHEST)


def _resident(shape):
    return pl.BlockSpec(shape, lambda *_: (0,) * len(shape), pipeline_mode=pl.Buffered(1))


def _params(*semantics):
    return pltpu.CompilerParams(dimension_semantics=semantics, vmem_limit_bytes=VMEM_LIMIT)


def _ffn_half_step(x, g, wgu_ref, wo_ref):
    n = _rms(x, g).astype(BF16)
    acc = jnp.zeros(x.shape, F32)
    for c in range(D_FF // FF_CHUNK):
        lo = c * FF_CHUNK
        gate = jnp.dot(n, wgu_ref[:, lo:lo + FF_CHUNK], preferred_element_type=F32)
        up = jnp.dot(n, wgu_ref[:, D_FF + lo:D_FF + lo + FF_CHUNK], preferred_element_type=F32)
        h = (_silu(gate) * up).astype(BF16)
        acc = acc + jnp.dot(h, wo_ref[lo:lo + FF_CHUNK, :], preferred_element_type=F32)
    return x + 0.5 * acc


def _ffn_mix_in_kernel(x_ref, g1_ref, wgu_ref, wo_ref, gm_ref, win_ref, x1_ref, p_ref):
    x1 = _ffn_half_step(x_ref[...], g1_ref[...], wgu_ref, wo_ref)
    x1_ref[...] = x1
    n = _rms(x1, gm_ref[...]).astype(BF16)
    for lo in range(0, P_WIDTH, 512):
        hi = min(lo + 512, P_WIDTH)
        p_ref[:, lo:hi] = jnp.dot(n, win_ref[:, lo:hi], preferred_element_type=F32)


def _ffn_mix_in(x, g1, wgu, wo, gm, win, tm):
    n_tok = x.shape[0]
    row = lambda w: pl.BlockSpec((tm, w), lambda i: (i, 0))
    return pl.pallas_call(
        _ffn_mix_in_kernel,
        grid=(n_tok // tm,),
        in_specs=[row(D_MODEL), _resident((1, D_MODEL)), _resident((D_MODEL, 2 * D_FF)),
                  _resident((D_FF, D_MODEL)), _resident((1, D_MODEL)), _resident((D_MODEL, P_WIDTH))],
        out_specs=[row(D_MODEL), row(P_WIDTH)],
        out_shape=[jax.ShapeDtypeStruct((n_tok, D_MODEL), F32),
                   jax.ShapeDtypeStruct((n_tok, P_WIDTH), F32)],
        compiler_params=_params("parallel"),
        name="ffn_mix_in",
    )(x, g1, wgu, wo, gm, win)


def _mix_out_ffn_kernel(x1_ref, attn_ref, gdn_ref, wa_ref, wb_ref, g2_ref, wgu_ref, wo_ref, gf_ref,
                        y_ref):
    x2 = x1_ref[...] + _dot(attn_ref[...], wa_ref[...]) + _dot(gdn_ref[...], wb_ref[...])
    x3 = _ffn_half_step(x2, g2_ref[...], wgu_ref, wo_ref)
    y_ref[...] = _rms(x3, gf_ref[...])


def _mix_out_ffn(x1, attn, gdn, wa, wb, g2, wgu, wo, gf, tm):
    n_tok = x1.shape[0]
    row = lambda w: pl.BlockSpec((tm, w), lambda i: (i, 0))
    return pl.pallas_call(
        _mix_out_ffn_kernel,
        grid=(n_tok // tm,),
        in_specs=[row(D_MODEL), row(ATTN_Q_WIDTH), row(GDN_V_WIDTH),
                  _resident((ATTN_Q_WIDTH, D_MODEL)), _resident((GDN_V_WIDTH, D_MODEL)),
                  _resident((1, D_MODEL)), _resident((D_MODEL, 2 * D_FF)),
                  _resident((D_FF, D_MODEL)), _resident((1, D_MODEL))],
        out_specs=row(D_MODEL),
        out_shape=jax.ShapeDtypeStruct((n_tok, D_MODEL), F32),
        compiler_params=_params("parallel"),
        name="mix_out_ffn",
    )(x1, attn, gdn, wa, wb, g2, wgu, wo, gf)


def _alibi_slope(head):
    return 2.0 ** (-8.0 * (head + 1) / N_Q_HEADS)


def _swa_prompt_kernel(sink_ref, q_ref, kvc_ref, kvp_ref, o_ref, bias_ref):
    blk = pl.program_id(1)
    nkeys = 2 * WINDOW

    @pl.when((pl.program_id(0) == 0) & (blk == 0))
    def _():
        key = lax.broadcasted_iota(jnp.int32, (nkeys, WINDOW), 0)
        qry = lax.broadcasted_iota(jnp.int32, (nkeys, WINDOW), 1)
        dist = qry - key + WINDOW
        valid = (dist >= 0) & (dist <= WINDOW)
        for h in range(N_Q_HEADS):
            g = h % GROUP
            bias_ref[h // GROUP, :, g * WINDOW:(g + 1) * WINDOW] = jnp.where(
                valid, -_alibi_slope(h) * dist.astype(F32), -jnp.inf)

    first_pen = jnp.where(blk > 0, 0.0, -jnp.inf)
    scale = HEAD_DIM ** -0.5
    nsub = q_ref.shape[0] // WINDOW
    jobs = [(j, kvh) for j in range(nsub) for kvh in range(N_KV_HEADS)]
    group_heads = lambda kvh: range(kvh * GROUP, (kvh + 1) * GROUP)
    sinks = [jnp.concatenate([jnp.full((1, WINDOW), sink_ref[h], F32) for h in group_heads(kvh)],
                             axis=1) for kvh in range(N_KV_HEADS)]

    def keys_values(j, col):
        if j == 0:
            return jnp.concatenate([kvp_ref[:, col], kvc_ref[0:WINDOW, col]], axis=0)
        return kvc_ref[(j - 1) * WINDOW:(j + 1) * WINDOW, col]

    scores = []
    for j, kvh in jobs:
        rows = slice(j * WINDOW, (j + 1) * WINDOW)
        k_all = keys_values(j, slice(kvh * HEAD_DIM, (kvh + 1) * HEAD_DIM)).astype(BF16)
        q_stack = jnp.concatenate(
            [(q_ref[rows, h * HEAD_DIM:(h + 1) * HEAD_DIM] * scale).astype(BF16)
             for h in group_heads(kvh)], axis=0)
        s = lax.dot_general(k_all, q_stack, (((1,), (1,)), ((), ())),
                            preferred_element_type=F32) + bias_ref[kvh]
        if j == 0:
            s = jnp.concatenate([s[:WINDOW] + first_pen, s[WINDOW:]], axis=0)
        scores.append(s)
    maxes = [jnp.maximum(jnp.max(s, axis=0, keepdims=True), sinks[kvh])
             for s, (j, kvh) in zip(scores, jobs)]
    probs = [jnp.exp(s - m) for s, m in zip(scores, maxes)]
    denoms = [jnp.sum(p, axis=0, keepdims=True) + jnp.exp(sinks[kvh] - m)
              for p, m, (j, kvh) in zip(probs, maxes, jobs)]
    for p, denom, (j, kvh) in zip(probs, denoms, jobs):
        v_all = keys_values(j, slice(KV_WIDTH + kvh * HEAD_DIM,
                                     KV_WIDTH + (kvh + 1) * HEAD_DIM)).astype(BF16)
        o_t = lax.dot_general(v_all, p.astype(BF16), (((0,), (0,)), ((), ())),
                              preferred_element_type=F32) / denom
        for g, h in enumerate(group_heads(kvh)):
            o_ref[j * WINDOW:(j + 1) * WINDOW, h * HEAD_DIM:(h + 1) * HEAD_DIM] = (
                o_t[:, g * WINDOW:(g + 1) * WINDOW].T)


def _swa_prompt(sinks, p3):
    batch, seq, _ = p3.shape
    tq = SWA_TILE
    nsub = tq // WINDOW
    q_blk = P_AQ // ATTN_Q_WIDTH
    kv_blk = P_KV // (2 * KV_WIDTH)
    return pl.pallas_call(
        _swa_prompt_kernel,
        grid=(batch, seq // tq),
        in_specs=[pl.BlockSpec(memory_space=pltpu.SMEM),
                  pl.BlockSpec((None, tq, ATTN_Q_WIDTH), lambda b, i: (b, i, q_blk)),
                  pl.BlockSpec((None, tq, 2 * KV_WIDTH), lambda b, i: (b, i, kv_blk)),
                  pl.BlockSpec((None, WINDOW, 2 * KV_WIDTH),
                               lambda b, i: (b, jnp.maximum(i * nsub - 1, 0), kv_blk))],
        out_specs=pl.BlockSpec((None, tq, ATTN_Q_WIDTH), lambda b, i: (b, i, 0)),
        out_shape=jax.ShapeDtypeStruct((batch, seq, ATTN_Q_WIDTH), F32),
        scratch_shapes=[pltpu.VMEM((N_KV_HEADS, 2 * WINDOW, GROUP * WINDOW), F32)],
        compiler_params=_params("arbitrary", "arbitrary"),
        name="swa_prompt",
    )(sinks, p3, p3, p3)


def _gdn_gates(gba, alog, dtb):
    beta = _sigmoid(gba)
    g = -jnp.exp(alog) * _softplus(gba + dtb)
    return beta, g


def _l2norm(x):
    return x * lax.rsqrt(jnp.sum(x * x, axis=-1, keepdims=True) + EPS)


INV_LEVELS = tuple(2 ** i for i in range(1, CHUNK.bit_length()))


def _gdn_prompt_kernel(raw_ref, prev_ref, gz_ref, gba_ref, convw_ref, alog_ref, dtb_ref, ng_ref,
                       o_ref, s_ref, mask_ref):
    t = pl.program_id(1)
    tt = GDN_TILE
    nchunk = tt // CHUNK
    heads = range(N_GDN_HEADS)

    r = lax.broadcasted_iota(jnp.int32, (tt, tt), 0)
    c = lax.broadcasted_iota(jnp.int32, (tt, tt), 1)
    xor_idx = r ^ c

    @pl.when((pl.program_id(0) == 0) & (t == 0))
    def _():
        mask_ref[0] = jnp.where((xor_idx < CHUNK) & (r >= c), 1.0, 0.0).astype(BF16)
        for i, s in enumerate(INV_LEVELS):
            mask_ref[i + 1] = jnp.where((xor_idx >= s // 2) & (xor_idx < s) & (r > c),
                                        1.0, 0.0).astype(BF16)

    @pl.when(t == 0)
    def _():
        s_ref[...] = jnp.zeros(s_ref.shape, F32)

    raw = raw_ref[...]
    prev = jnp.where(t > 0, prev_ref[...], 0.0)
    ext = jnp.concatenate([prev, raw], axis=0)
    w = convw_ref[...]
    base = SUBLANES - (CONV_WIDTH - 1)
    conv = w[0:1] * ext[base:base + tt]
    for j in range(1, CONV_WIDTH - 1):
        conv = conv + w[j:j + 1] * ext[base + j:base + j + tt]
    conv = conv + w[CONV_WIDTH - 1:CONV_WIDTH] * raw
    act = _silu(conv)

    beta_all, g_all = _gdn_gates(gba_ref[...], alog_ref[...], dtb_ref[...])

    strict = (xor_idx < CHUNK) & (r > c)
    diag = r == c

    g_hi = g_all.astype(BF16)
    g_lo = (g_all - g_hi.astype(F32)).astype(BF16)
    cs = jnp.dot(mask_ref[0], jnp.concatenate([g_hi, g_lo], axis=1), preferred_element_type=F32)
    gc = cs[:, :LANES] + cs[:, LANES:]
    gc_t = gc.T
    gtot = jnp.concatenate(
        [jnp.broadcast_to(gc[(ci + 1) * CHUNK - 1:(ci + 1) * CHUNK, :], (CHUNK, LANES))
         for ci in range(nchunk)], axis=0)

    q_l, k_l, v_l, beta_l, gcol_l, gend_l, a_l, qk_l, x_l = [], [], [], [], [], [], [], [], []
    for h in heads:
        q = _l2norm(act[:, h * GDN_DK:(h + 1) * GDN_DK]) * (GDN_DK ** -0.5)
        k = _l2norm(act[:, GDN_QK_WIDTH + h * GDN_DK:GDN_QK_WIDTH + (h + 1) * GDN_DK])
        beta = beta_all[:, h:h + 1]
        gcol = gc[:, N_GDN_HEADS + h:N_GDN_HEADS + h + 1]
        grow = gc_t[N_GDN_HEADS + h:N_GDN_HEADS + h + 1, :]
        decay = jnp.exp(jnp.where(strict, gcol - grow, -jnp.inf))
        kb = k.astype(BF16)
        qk_kk = _dot_nt(jnp.concatenate([q.astype(BF16), kb], axis=0), kb)
        a = ((beta * qk_kk[tt:]) * decay).astype(BF16)
        qk_l.append((qk_kk[:tt] * jnp.where(diag, 1.0, decay)).astype(BF16))
        x_l.append(jnp.where(diag, 1.0, 0.0).astype(BF16) - a * mask_ref[1])
        q_l.append(q); k_l.append(k); beta_l.append(beta); gcol_l.append(gcol); a_l.append(a)
        gend_l.append(gtot[:, N_GDN_HEADS + h:N_GDN_HEADS + h + 1])
        v_l.append(act[:, 2 * GDN_QK_WIDTH + h * GDN_DV:2 * GDN_QK_WIDTH + (h + 1) * GDN_DV])

    for i in range(1, len(INV_LEVELS)):
        for h in heads:
            x = x_l[h]
            y = jnp.dot(x, a_l[h], preferred_element_type=F32).astype(BF16)
            z = jnp.dot(y, x, preferred_element_type=F32).astype(BF16)
            x_l[h] = x - z * mask_ref[i + 1]

    u_l, w_l, qd_l, kd_l = [], [], [], []
    for h in heads:
        eg = jnp.exp(gcol_l[h])
        rhs = jnp.concatenate([v_l[h] * beta_l[h], k_l[h] * (beta_l[h] * eg)], axis=1)
        uw = jnp.dot(x_l[h], rhs.astype(BF16), preferred_element_type=F32)
        u_l.append(uw[:, :GDN_DV])
        w_l.append(uw[:, GDN_DV:].astype(BF16))
        qd_l.append((q_l[h] * eg).astype(BF16))
        kd_l.append((k_l[h] * jnp.exp(gend_l[h] - gcol_l[h])).astype(BF16))

    states = [s_ref[h] for h in heads]
    v_new = [[] for _ in heads]
    o_inter = [[] for _ in heads]
    for ci in range(nchunk):
        rows = slice(ci * CHUNK, (ci + 1) * CHUNK)
        for h in heads:
            sb = states[h].astype(BF16)
            ws_qs = jnp.dot(jnp.concatenate([w_l[h][rows], qd_l[h][rows]], axis=0), sb,
                            preferred_element_type=F32)
            vn = u_l[h][rows] - ws_qs[:CHUNK]
            o_inter[h].append(ws_qs[CHUNK:])
            g_last = jnp.exp(gend_l[h][ci * CHUNK:ci * CHUNK + 1, :])
            upd = lax.dot_general(kd_l[h][rows], vn.astype(BF16), (((0,), (0,)), ((), ())),
                                  preferred_element_type=F32)
            states[h] = states[h] * g_last + upd
            v_new[h].append(vn)

    for h in heads:
        s_ref[h] = states[h]
        vn_all = jnp.concatenate(v_new[h], axis=0).astype(BF16)
        o = jnp.concatenate(o_inter[h], axis=0) + jnp.dot(qk_l[h], vn_all,
                                                           preferred_element_type=F32)
        gz = gz_ref[:, h * GDN_DV:(h + 1) * GDN_DV]
        o_ref[:, h * GDN_DV:(h + 1) * GDN_DV] = _rms(o, ng_ref[...]) * _silu(gz)


def _gdn_prompt(p3, convw, alog, dtb, ng):
    batch, seq, _ = p3.shape
    tt = GDN_TILE
    prev_per_tile = tt // SUBLANES
    small = lambda shape: pl.BlockSpec(shape, lambda b, t: (0,) * len(shape))
    return pl.pallas_call(
        _gdn_prompt_kernel,
        grid=(batch, seq // tt),
        in_specs=[pl.BlockSpec((None, tt, GDN_CONV_DIM), lambda b, t: (b, t, P_GQKV // GDN_CONV_DIM)),
                  pl.BlockSpec((None, SUBLANES, GDN_CONV_DIM),
                               lambda b, t: (b, jnp.maximum(t * prev_per_tile - 1, 0),
                                             P_GQKV // GDN_CONV_DIM)),
                  pl.BlockSpec((None, tt, GDN_V_WIDTH), lambda b, t: (b, t, P_GZ // GDN_V_WIDTH)),
                  pl.BlockSpec((None, tt, LANES), lambda b, t: (b, t, P_GBA // LANES)),
                  small((CONV_WIDTH, GDN_CONV_DIM)), small((1, LANES)), small((1, LANES)),
                  small((1, GDN_DV))],
        out_specs=[pl.BlockSpec((None, tt, GDN_V_WIDTH), lambda b, t: (b, t, 0)),
                   pl.BlockSpec((None, N_GDN_HEADS, GDN_DK, GDN_DV), lambda b, t: (b, 0, 0, 0))],
        out_shape=[jax.ShapeDtypeStruct((batch, seq, GDN_V_WIDTH), F32),
                   jax.ShapeDtypeStruct((batch, N_GDN_HEADS, GDN_DK, GDN_DV), F32)],
        scratch_shapes=[pltpu.VMEM((len(INV_LEVELS) + 1, tt, tt), BF16)],
        compiler_params=_params("arbitrary", "arbitrary"),
        name="gdn_prompt",
    )(p3, p3, p3, p3, convw, alog, dtb, ng)


def _sample_prep_kernel(p_ref, hist_ref, convw_ref, alog_ref, dtb_ref,
                        conv_ref, qt_ref, kt_ref, v_ref, beta_ref, decay_ref):
    raw = p_ref[:, P_GQKV:P_GQKV + GDN_CONV_DIM]
    w = convw_ref[...]
    conv = w[0:1] * hist_ref[:, 0:GDN_CONV_DIM]
    for j in range(1, CONV_WIDTH - 1):
        conv = conv + w[j:j + 1] * hist_ref[:, j * GDN_CONV_DIM:(j + 1) * GDN_CONV_DIM]
    conv = conv + w[CONV_WIDTH - 1:CONV_WIDTH] * raw
    act = _silu(conv)
    conv_ref[:, 0:(CONV_WIDTH - 2) * GDN_CONV_DIM] = hist_ref[:, GDN_CONV_DIM:]
    conv_ref[:, (CONV_WIDTH - 2) * GDN_CONV_DIM:] = raw
    v_ref[...] = act[:, 2 * GDN_QK_WIDTH:]
    beta, g = _gdn_gates(p_ref[:, P_GBA:P_GBA + LANES], alog_ref[...], dtb_ref[...])
    beta_ref[...] = beta
    decay_ref[...] = jnp.exp(g)
    nblk = qt_ref.shape[0]
    for h in range(N_GDN_HEADS):
        qt = (_l2norm(act[:, h * GDN_DK:(h + 1) * GDN_DK]) * (GDN_DK ** -0.5)).T
        kt = _l2norm(act[:, GDN_QK_WIDTH + h * GDN_DK:GDN_QK_WIDTH + (h + 1) * GDN_DK]).T
        for i in range(nblk):
            qt_ref[i, h] = qt[:, i * SAMPLE_BLOCK:(i + 1) * SAMPLE_BLOCK]
            kt_ref[i, h] = kt[:, i * SAMPLE_BLOCK:(i + 1) * SAMPLE_BLOCK]


def _sample_prep(p, hist, convw, alog, dtb):
    nseq = p.shape[0]
    nblk = nseq // SAMPLE_BLOCK
    hist_w = (CONV_WIDTH - 1) * GDN_CONV_DIM
    cols = jax.ShapeDtypeStruct((nblk, N_GDN_HEADS, GDN_DK, SAMPLE_BLOCK), F32)
    return pl.pallas_call(
        _sample_prep_kernel,
        out_shape=[jax.ShapeDtypeStruct((nseq, hist_w), F32), cols, cols,
                   jax.ShapeDtypeStruct((nseq, GDN_V_WIDTH), F32),
                   jax.ShapeDtypeStruct((nseq, LANES), F32),
                   jax.ShapeDtypeStruct((nseq, LANES), F32)],
        compiler_params=pltpu.CompilerParams(vmem_limit_bytes=VMEM_LIMIT),
        name="sample_prep",
    )(p, hist, convw, alog, dtb)


def _per_group(values):
    g = lax.broadcasted_iota(jnp.int32, (GROUP, 1), 0)
    col = jnp.full((GROUP, 1), values[GROUP - 1], F32)
    for i in range(GROUP - 2, -1, -1):
        col = jnp.where(g == i, values[i], col)
    return col


def _sample_mix_kernel(sink_ref, p_ref, kh_ref, vh_ref, qt_ref, kt_ref, v_ref, beta_ref, decay_ref,
                       s_ref, ng_ref, attn_ref, gdn_ref, nk_ref, nv_ref, ns_ref):
    key_pos = lax.broadcasted_iota(jnp.int32, (1, WINDOW), 1)
    dist_hist = (WINDOW - key_pos).astype(F32)
    for b in range(SAMPLE_BLOCK):
        k_hist = kh_ref[b]
        v_hist = vh_ref[b]
        k_new = p_ref[b:b + 1, P_KV:P_KV + KV_WIDTH]
        v_new = p_ref[b:b + 1, P_KV + KV_WIDTH:P_KV + 2 * KV_WIDTH]
        nk_ref[b] = jnp.concatenate([k_hist[1:], k_new], axis=0)
        nv_ref[b] = jnp.concatenate([v_hist[1:], v_new], axis=0)
        outs = []
        for kvh in range(N_KV_HEADS):
            heads = range(kvh * GROUP, (kvh + 1) * GROUP)
            qg = jnp.concatenate(
                [p_ref[b:b + 1, P_AQ + h * HEAD_DIM:P_AQ + (h + 1) * HEAD_DIM] for h in heads],
                axis=0)
            cols = slice(kvh * HEAD_DIM, (kvh + 1) * HEAD_DIM)
            slope = _per_group([_alibi_slope(h) for h in heads])
            sink = _per_group([sink_ref[h] for h in heads])
            scale = HEAD_DIM ** -0.5
            s_hist = _dot_nt(qg, k_hist[:, cols]) * scale - slope * dist_hist
            qb = qg.astype(BF16).astype(F32)
            kb = k_new[:, cols].astype(BF16).astype(F32)
            s_new = jnp.sum(qb * kb, axis=-1, keepdims=True) * scale
            m = jnp.maximum(jnp.maximum(jnp.max(s_hist, axis=-1, keepdims=True), s_new), sink)
            p_hist = jnp.exp(s_hist - m)
            p_new = jnp.exp(s_new - m)
            denom = jnp.sum(p_hist, axis=-1, keepdims=True) + p_new + jnp.exp(sink - m)
            o = (_dot(p_hist, v_hist[:, cols]) + p_new * v_new[:, cols]) / denom
            outs.extend(o[g:g + 1] for g in range(GROUP))
        attn_ref[b:b + 1, :] = jnp.concatenate(outs, axis=1)

    for b in range(SAMPLE_BLOCK):
        for h in range(N_GDN_HEADS):
            kcol = kt_ref[h, :, b:b + 1]
            qcol = qt_ref[h, :, b:b + 1]
            v = v_ref[b:b + 1, h * GDN_DV:(h + 1) * GDN_DV]
            beta = beta_ref[b:b + 1, h:h + 1]
            decay = decay_ref[b:b + 1, N_GDN_HEADS + h:N_GDN_HEADS + h + 1]
            state = s_ref[b, h] * decay
            ks = jnp.sum(state * kcol, axis=0, keepdims=True)
            state = state + kcol * (beta * (v - ks))
            ns_ref[b, h] = state
            o = jnp.sum(state * qcol, axis=0, keepdims=True)
            gz = p_ref[b:b + 1, P_GZ + h * GDN_DV:P_GZ + (h + 1) * GDN_DV]
            gdn_ref[b:b + 1, h * GDN_DV:(h + 1) * GDN_DV] = _rms(o, ng_ref[...]) * _silu(gz)


def _sample_mix(sinks, p, k_hist, v_hist, qt, kt, v, beta, decay, state, ng):
    nseq = p.shape[0]
    bb = SAMPLE_BLOCK
    rows = lambda w: pl.BlockSpec((bb, w), lambda i: (i, 0))
    cache = pl.BlockSpec((bb, WINDOW, KV_WIDTH), lambda i: (i, 0, 0))
    cols = pl.BlockSpec((None, N_GDN_HEADS, GDN_DK, bb), lambda i: (i, 0, 0, 0))
    st = pl.BlockSpec((bb, N_GDN_HEADS, GDN_DK, GDN_DV), lambda i: (i, 0, 0, 0))
    return pl.pallas_call(
        _sample_mix_kernel,
        grid=(nseq // bb,),
        in_specs=[pl.BlockSpec(memory_space=pltpu.SMEM), rows(P_WIDTH), cache, cache, cols, cols,
                  rows(GDN_V_WIDTH), rows(LANES), rows(LANES), st,
                  pl.BlockSpec((1, GDN_DV), lambda i: (0, 0))],
        out_specs=[rows(ATTN_Q_WIDTH), rows(GDN_V_WIDTH), cache, cache, st],
        out_shape=[jax.ShapeDtypeStruct((nseq, ATTN_Q_WIDTH), F32),
                   jax.ShapeDtypeStruct((nseq, GDN_V_WIDTH), F32),
                   jax.ShapeDtypeStruct(k_hist.shape, F32),
                   jax.ShapeDtypeStruct(v_hist.shape, F32),
                   jax.ShapeDtypeStruct(state.shape, F32)],
        compiler_params=_params("parallel"),
        name="sample_mix",
    )(sinks, p, k_hist, v_hist, qt, kt, v, beta, decay, state, ng)


def _reorder_in_mix(w):
    o_ak = ATTN_Q_WIDTH
    o_gq = o_ak + 2 * KV_WIDTH
    o_gz = o_gq + GDN_CONV_DIM
    o_gb = o_gz + GDN_V_WIDTH
    pad = jnp.zeros((w.shape[0], LANES - 2 * N_GDN_HEADS), w.dtype)
    return jnp.concatenate([w[:, o_gq:o_gz], w[:, :o_ak], w[:, o_gz:o_gb], w[:, o_ak:o_gq],
                            w[:, o_gb:], pad], axis=1)


def _lane_pad(vec, offset):
    return jnp.zeros((1, LANES), F32).at[0, offset:offset + vec.shape[0]].set(vec)


def kernel(x_prompt, x_sample, cache_attn_k, cache_attn_v, state_conv, state_gdn, ffn1_norm_g,
           ffn1_w_in, ffn1_w_out, mix_norm_g, w_in_mix, attn_sinks, conv_w, gdn_A_log, gdn_dt_bias,
           gdn_norm_g, w_out_mix, ffn2_norm_g, ffn2_w_in, ffn2_w_out, final_norm_g):
    depth = ffn1_w_in.shape[0]
    assert depth == 1, "single-layer trunk"
    batch, seq, _ = x_prompt.shape
    nseq = x_sample.shape[0]
    assert x_sample.shape[1] == 1 and cache_attn_k.shape[2] == WINDOW
    assert seq % GDN_TILE == 0 and seq % WINDOW == 0 and nseq % SAMPLE_BLOCK == 0
    l = 0
    row = lambda v: v.reshape(1, -1)
    g1, gm, g2, gf = row(ffn1_norm_g[l]), row(mix_norm_g[l]), row(ffn2_norm_g[l]), row(final_norm_g)
    ng = row(gdn_norm_g[l])
    wgu1, wo1 = ffn1_w_in[l].astype(BF16), ffn1_w_out[l].astype(BF16)
    wgu2, wo2 = ffn2_w_in[l].astype(BF16), ffn2_w_out[l].astype(BF16)
    win = _reorder_in_mix(w_in_mix[l]).astype(BF16)
    wa = w_out_mix[l, :ATTN_Q_WIDTH].astype(BF16)
    wb = w_out_mix[l, ATTN_Q_WIDTH:].astype(BF16)
    alog = _lane_pad(gdn_A_log[l], N_GDN_HEADS)
    dtb = _lane_pad(gdn_dt_bias[l], N_GDN_HEADS)
    sinks = attn_sinks[l]
    convw = conv_w[l]

    xp = x_prompt.reshape(batch * seq, D_MODEL)
    x1p, pp = _ffn_mix_in(xp, g1, wgu1, wo1, gm, win, TOKEN_TILE)
    pp3 = pp.reshape(batch, seq, P_WIDTH)
    attn_p = _swa_prompt(sinks, pp3)
    gdn_p, state_p = _gdn_prompt(pp3, convw, alog, dtb, ng)
    y_p = _mix_out_ffn(x1p, attn_p.reshape(batch * seq, ATTN_Q_WIDTH),
                       gdn_p.reshape(batch * seq, GDN_V_WIDTH), wa, wb, g2, wgu2, wo2, gf, TOKEN_TILE)
    tail = pp3[:, seq - WINDOW:, P_KV:P_KV + 2 * KV_WIDTH]
    new_k_p = tail[:, :, :KV_WIDTH].reshape(1, batch, WINDOW, N_KV_HEADS, HEAD_DIM)
    new_v_p = tail[:, :, KV_WIDTH:].reshape(1, batch, WINDOW, N_KV_HEADS, HEAD_DIM)
    new_conv_p = pp3[:, seq - (CONV_WIDTH - 1):, P_GQKV:P_GQKV + GDN_CONV_DIM][None]

    xs = x_sample.reshape(nseq, D_MODEL)
    x1s, ps = _ffn_mix_in(xs, g1, wgu1, wo1, gm, win, nseq)
    hist = state_conv[l].reshape(nseq, (CONV_WIDTH - 1) * GDN_CONV_DIM)
    new_conv_s, qt, kt, v_s, beta_s, decay_s = _sample_prep(ps, hist, convw, alog, dtb)
    attn_s, gdn_s, new_k_s, new_v_s, state_s = _sample_mix(
        sinks, ps, cache_attn_k[l].reshape(nseq, WINDOW, KV_WIDTH),
        cache_attn_v[l].reshape(nseq, WINDOW, KV_WIDTH), qt, kt, v_s, beta_s, decay_s,
        state_gdn[l], ng)
    y_s = _mix_out_ffn(x1s, attn_s, gdn_s, wa, wb, g2, wgu2, wo2, gf, nseq)

    kv_shape = (1, nseq, WINDOW, N_KV_HEADS, HEAD_DIM)
    return (y_p.reshape(batch, seq, D_MODEL), y_s.reshape(nseq, 1, D_MODEL),
            new_k_p, new_v_p, new_conv_p, state_p[None],
            new_k_s.reshape(kv_shape), new_v_s.reshape(kv_shape),
            new_conv_s.reshape(1, nseq, CONV_WIDTH - 1, GDN_CONV_DIM), state_s[None])
```

```python
import functools

import jax
import jax.numpy as jnp
from jax import lax
from jax.experimental import pallas as pl
from jax.experimental.pallas import tpu as pltpu

F32 = jnp.float32
BF16 = jnp.bfloat16

D_MODEL = 1024
D_FF = 2816
N_Q_HEADS = 8
N_KV_HEADS = 2
GROUP = N_Q_HEADS // N_KV_HEADS
HEAD_DIM = 64
WINDOW = 128
ATTN_Q_WIDTH = N_Q_HEADS * HEAD_DIM
KV_WIDTH = N_KV_HEADS * HEAD_DIM
N_GDN_HEADS = 4
GDN_DK = 128
GDN_DV = 128
GDN_QK_WIDTH = N_GDN_HEADS * GDN_DK
GDN_V_WIDTH = N_GDN_HEADS * GDN_DV
GDN_CONV_DIM = 2 * GDN_QK_WIDTH + GDN_V_WIDTH
CONV_WIDTH = 4
CHUNK = 64
EPS = 1e-6

LANES = 128
SUBLANES = 8
VMEM_LIMIT = 56 * 1024 * 1024

P_GQKV = 0
P_AQ = P_GQKV + GDN_CONV_DIM
P_GZ = P_AQ + ATTN_Q_WIDTH
P_KV = P_GZ + GDN_V_WIDTH
P_GBA = P_KV + 2 * KV_WIDTH
P_WIDTH = P_GBA + LANES

FF_CHUNK = 256
TOKEN_TILE = 512
GDN_TILE = 256
SWA_TILE = 512
CONV_ROWS = 64
SAMPLE_BLOCK = 8


def _sigmoid(x):
    return 1.0 / (1.0 + jnp.exp(-x))


def _silu(x):
    return x * _sigmoid(x)


def _softplus(x):
    return jnp.maximum(x, 0.0) + jnp.log1p(jnp.exp(-jnp.abs(x)))


def _rms(x, g):
    return x * lax.rsqrt(jnp.mean(x * x, axis=-1, keepdims=True) + EPS) * g


def _dot(a, b):
    return jnp.dot(a.astype(BF16), b.astype(BF16), preferred_element_type=F32)


def _dot_nt(a, b):
    return lax.dot_general(a.astype(BF16), b.astype(BF16), (((1,), (1,)), ((), ())),
                           preferred_element_type=F32)


def _dot_exact(a, b):
    return jnp.dot(a, b, preferred_element_type=F32, precision=lax.Precision.HIGHEST)


def _resident(shape):
    return pl.BlockSpec(shape, lambda *_: (0,) * len(shape), pipeline_mode=pl.Buffered(1))


def _params(*semantics):
    return pltpu.CompilerParams(dimension_semantics=semantics, vmem_limit_bytes=VMEM_LIMIT)


def _ffn_half_step(x, g, wgu_ref, wo_ref):
    n = _rms(x, g).astype(BF16)
    acc = jnp.zeros(x.shape, F32)
    for c in range(D_FF // FF_CHUNK):
        lo = c * FF_CHUNK
        gate = jnp.dot(n, wgu_ref[:, lo:lo + FF_CHUNK], preferred_element_type=F32)
        up = jnp.dot(n, wgu_ref[:, D_FF + lo:D_FF + lo + FF_CHUNK], preferred_element_type=F32)
        h = (_silu(gate) * up).astype(BF16)
        acc = acc + jnp.dot(h, wo_ref[lo:lo + FF_CHUNK, :], preferred_element_type=F32)
    return x + 0.5 * acc


_W_AQ, _W_KV = 0, ATTN_Q_WIDTH
_W_GQKV = _W_KV + 2 * KV_WIDTH
_W_GZ = _W_GQKV + GDN_CONV_DIM
_W_GBA = _W_GZ + GDN_V_WIDTH
_PROJ_GROUPS = ((_W_GQKV, GDN_CONV_DIM, P_GQKV), (_W_GBA, LANES, P_GBA), (_W_AQ, ATTN_Q_WIDTH, P_AQ),
                (_W_GZ, GDN_V_WIDTH, P_GZ), (_W_KV, 2 * KV_WIDTH, P_KV))


def _gdn_gates(gba, alog, dtb):
    beta = _sigmoid(gba)
    g = -jnp.exp(alog) * _softplus(gba + dtb)
    return beta, g


def _l2norm(x):
    return x * lax.rsqrt(jnp.sum(x * x, axis=-1, keepdims=True) + EPS)


def _causal_conv(hist, raw, w):
    rows = raw.shape[0]
    ext = jnp.concatenate([hist, raw], axis=0)
    base = SUBLANES - (CONV_WIDTH - 1)
    conv = w[0:1] * ext[base:base + rows]
    for j in range(1, CONV_WIDTH - 1):
        conv = conv + w[j:j + 1] * ext[base + j:base + j + rows]
    return conv + w[CONV_WIDTH - 1:CONV_WIDTH] * raw


def _ffn_mix_in_kernel(x_ref, g1_ref, wgu_ref, wo_ref, gm_ref, win_ref, *rest, tiles_per_seq):
    fused = tiles_per_seq is not None
    if fused:
        convw_ref, alog_ref, dtb_ref, x1_ref, p_ref, tail_ref, raw_ref = rest

        @pl.when(pl.program_id(0) == 0)
        def _():
            raw_ref[...] = jnp.zeros(raw_ref.shape, F32)
    else:
        x1_ref, p_ref = rest
    x1 = _ffn_half_step(x_ref[...], g1_ref[...], wgu_ref, wo_ref)
    x1_ref[...] = x1
    n = _rms(x1, gm_ref[...]).astype(BF16)
    proj = lambda src, width: jnp.dot(n, win_ref[:, src:src + width], preferred_element_type=F32)
    if not fused:
        for src, width, dst in _PROJ_GROUPS:
            p_ref[:, dst:dst + width] = proj(src, width)
        return
    tm = x1.shape[0]
    first = pl.program_id(0) % tiles_per_seq == 0
    raw_ref[0:SUBLANES] = jnp.where(first, 0.0, raw_ref[tm:tm + SUBLANES])
    for lo in range(0, GDN_CONV_DIM, 512):
        raw_ref[SUBLANES:SUBLANES + tm, lo:lo + 512] = proj(_W_GQKV + lo, 512)
    gba = proj(_W_GBA, LANES)
    for src, width, dst in _PROJ_GROUPS[2:]:
        p_ref[:, dst:dst + width] = proj(src, width)
    tail_ref[...] = raw_ref[tm:tm + SUBLANES]

    base = SUBLANES - (CONV_WIDTH - 1)
    for cg in range(GDN_CONV_DIM // LANES):
        cols = slice(cg * LANES, (cg + 1) * LANES)
        w = convw_ref[:, cols]
        for r0 in range(0, tm, CONV_ROWS):
            acc = w[0:1] * raw_ref[base + r0:base + r0 + CONV_ROWS, cols]
            for j in range(1, CONV_WIDTH):
                acc = acc + w[j:j + 1] * raw_ref[base + j + r0:base + j + r0 + CONV_ROWS, cols]
            act = _silu(acc)
            if cg < GDN_QK_WIDTH // LANES:
                act = _l2norm(act) * (GDN_DK ** -0.5)
            elif cg < 2 * GDN_QK_WIDTH // LANES:
                act = _l2norm(act)
            p_ref[r0:r0 + CONV_ROWS, cols] = act
    beta, g = _gdn_gates(gba, alog_ref[...], dtb_ref[...])
    lane = lax.broadcasted_iota(jnp.int32, gba.shape, 1)
    p_ref[:, P_GBA:P_GBA + LANES] = jnp.where(lane < N_GDN_HEADS, beta, g)


def _ffn_mix_in(x, g1, wgu, wo, gm, win, tm, conv=None):
    n_tok = x.shape[0]
    row = lambda w: pl.BlockSpec((tm, w), lambda i: (i, 0))
    in_specs = [row(D_MODEL), _resident((1, D_MODEL)), _resident((D_MODEL, 2 * D_FF)),
                _resident((D_FF, D_MODEL)), _resident((1, D_MODEL)), _resident((D_MODEL, P_WIDTH))]
    out_specs = [row(D_MODEL), row(P_WIDTH)]
    out_shape = [jax.ShapeDtypeStruct((n_tok, D_MODEL), F32),
                 jax.ShapeDtypeStruct((n_tok, P_WIDTH), F32)]
    args = [x, g1, wgu, wo, gm, win]
    scratch = []
    tiles_per_seq = None
    if conv is not None:
        convw, alog, dtb, seq = conv
        tiles_per_seq = seq // tm
        in_specs += [_resident((CONV_WIDTH, GDN_CONV_DIM)), _resident((1, LANES)),
                     _resident((1, LANES))]
        out_specs.append(pl.BlockSpec((None, SUBLANES, GDN_CONV_DIM),
                                      lambda i: (i // tiles_per_seq, 0, 0)))
        out_shape.append(jax.ShapeDtypeStruct((n_tok // seq, SUBLANES, GDN_CONV_DIM), F32))
        args += [convw, alog, dtb]
        scratch = [pltpu.VMEM((SUBLANES + tm, GDN_CONV_DIM), F32)]
    return pl.pallas_call(
        functools.partial(_ffn_mix_in_kernel, tiles_per_seq=tiles_per_seq),
        grid=(n_tok // tm,),
        in_specs=in_specs,
        out_specs=out_specs,
        out_shape=out_shape,
        scratch_shapes=scratch,
        compiler_params=_params("arbitrary"),
        name="ffn_mix_in",
    )(*args)


def _mix_out_ffn_kernel(x1_ref, attn_ref, gdn_ref, wa_ref, wb_ref, g2_ref, wgu_ref, wo_ref, gf_ref,
                        y_ref):
    x2 = x1_ref[...] + _dot(attn_ref[...], wa_ref[...]) + _dot(gdn_ref[...], wb_ref[...])
    x3 = _ffn_half_step(x2, g2_ref[...], wgu_ref, wo_ref)
    y_ref[...] = _rms(x3, gf_ref[...])


def _mix_out_ffn(x1, attn, gdn, wa, wb, g2, wgu, wo, gf, tm):
    n_tok = x1.shape[0]
    row = lambda w: pl.BlockSpec((tm, w), lambda i: (i, 0))
    return pl.pallas_call(
        _mix_out_ffn_kernel,
        grid=(n_tok // tm,),
        in_specs=[row(D_MODEL), row(ATTN_Q_WIDTH), row(GDN_V_WIDTH),
                  _resident((ATTN_Q_WIDTH, D_MODEL)), _resident((GDN_V_WIDTH, D_MODEL)),
                  _resident((1, D_MODEL)), _resident((D_MODEL, 2 * D_FF)),
                  _resident((D_FF, D_MODEL)), _resident((1, D_MODEL))],
        out_specs=row(D_MODEL),
        out_shape=jax.ShapeDtypeStruct((n_tok, D_MODEL), F32),
        compiler_params=_params("parallel"),
        name="mix_out_ffn",
    )(x1, attn, gdn, wa, wb, g2, wgu, wo, gf)


def _alibi_slope(head):
    return 2.0 ** (-8.0 * (head + 1) / N_Q_HEADS)


def _swa_prompt_kernel(sink_ref, q_ref, kvc_ref, kvp_ref, o_ref, bias_ref, s_ref):
    blk = pl.program_id(1)
    nkeys = 2 * WINDOW

    @pl.when((pl.program_id(0) == 0) & (blk == 0))
    def _():
        key = lax.broadcasted_iota(jnp.int32, (nkeys, WINDOW), 0)
        qry = lax.broadcasted_iota(jnp.int32, (nkeys, WINDOW), 1)
        dist = qry - key + WINDOW
        valid = (dist >= 0) & (dist <= WINDOW)
        for h in range(N_Q_HEADS):
            g = h % GROUP
            bias_ref[h // GROUP, :, g * WINDOW:(g + 1) * WINDOW] = jnp.where(
                valid, -_alibi_slope(h) * dist.astype(F32), -jnp.inf)

    first_pen = jnp.where(blk > 0, 0.0, -jnp.inf)
    scale = HEAD_DIM ** -0.5
    nsub = q_ref.shape[0] // WINDOW
    jobs = [(j, kvh) for j in range(nsub) for kvh in range(N_KV_HEADS)]
    group_heads = lambda kvh: range(kvh * GROUP, (kvh + 1) * GROUP)
    def keys_values(j, col):
        if j == 0:
            return jnp.concatenate([kvp_ref[:, col], kvc_ref[0:WINDOW, col]], axis=0)
        return kvc_ref[(j - 1) * WINDOW:(j + 1) * WINDOW, col]

    for job, (j, kvh) in enumerate(jobs):
        rows = slice(j * WINDOW, (j + 1) * WINDOW)
        k_all = keys_values(j, slice(kvh * HEAD_DIM, (kvh + 1) * HEAD_DIM)).astype(BF16)
        v_t = keys_values(j, slice(KV_WIDTH + kvh * HEAD_DIM,
                                   KV_WIDTH + (kvh + 1) * HEAD_DIM)).T.astype(BF16)
        q_stack = jnp.concatenate(
            [(q_ref[rows, h * HEAD_DIM:(h + 1) * HEAD_DIM] * scale).astype(BF16)
             for h in group_heads(kvh)], axis=0)
        s_ref[job] = lax.dot_general(k_all, q_stack, (((1,), (1,)), ((), ())),
                                     preferred_element_type=F32)
        for g, h in enumerate(group_heads(kvh)):
            lanes = slice(g * WINDOW, (g + 1) * WINDOW)
            s = s_ref[job, :, lanes] + bias_ref[kvh, :, lanes]
            if j == 0:
                s = jnp.concatenate([s[:WINDOW] + first_pen, s[WINDOW:]], axis=0)
            sink = sink_ref[h]
            m = jnp.maximum(jnp.max(s, axis=0, keepdims=True), sink)
            p = jnp.exp(s - m)
            denom = jnp.sum(p, axis=0, keepdims=True) + jnp.exp(sink - m)
            o_t = jnp.dot(v_t, p.astype(BF16), preferred_element_type=F32) / denom
            o_ref[rows, h * HEAD_DIM:(h + 1) * HEAD_DIM] = o_t.T


def _swa_prompt(sinks, p3):
    batch, seq, _ = p3.shape
    tq = SWA_TILE
    nsub = tq // WINDOW
    q_blk = P_AQ // ATTN_Q_WIDTH
    kv_blk = P_KV // (2 * KV_WIDTH)
    return pl.pallas_call(
        _swa_prompt_kernel,
        grid=(batch, seq // tq),
        in_specs=[pl.BlockSpec(memory_space=pltpu.SMEM),
                  pl.BlockSpec((None, tq, ATTN_Q_WIDTH), lambda b, i: (b, i, q_blk)),
                  pl.BlockSpec((None, tq, 2 * KV_WIDTH), lambda b, i: (b, i, kv_blk)),
                  pl.BlockSpec((None, WINDOW, 2 * KV_WIDTH),
                               lambda b, i: (b, jnp.maximum(i * nsub - 1, 0), kv_blk))],
        out_specs=pl.BlockSpec((None, tq, ATTN_Q_WIDTH), lambda b, i: (b, i, 0)),
        out_shape=jax.ShapeDtypeStruct((batch, seq, ATTN_Q_WIDTH), F32),
        scratch_shapes=[pltpu.VMEM((N_KV_HEADS, 2 * WINDOW, GROUP * WINDOW), F32),
                        pltpu.VMEM((nsub * N_KV_HEADS, 2 * WINDOW, GROUP * WINDOW), F32)],
        compiler_params=_params("arbitrary", "arbitrary"),
        name="swa_prompt",
    )(sinks, p3, p3, p3)


INV_LEVELS = tuple(2 ** i for i in range(1, CHUNK.bit_length()))


def _gdn_prompt_kernel(act_ref, gz_ref, gates_ref, ng_ref, o_ref, s_ref, mask_ref):
    t = pl.program_id(1)
    tt = GDN_TILE
    nchunk = tt // CHUNK
    heads = range(N_GDN_HEADS)

    r = lax.broadcasted_iota(jnp.int32, (tt, tt), 0)
    c = lax.broadcasted_iota(jnp.int32, (tt, tt), 1)
    xor_idx = r ^ c

    @pl.when((pl.program_id(0) == 0) & (t == 0))
    def _():
        mask_ref[0] = jnp.where((xor_idx < CHUNK) & (r >= c), 1.0, 0.0).astype(BF16)
        for i, s in enumerate(INV_LEVELS):
            mask_ref[i + 1] = jnp.where((xor_idx >= s // 2) & (xor_idx < s) & (r > c),
                                        1.0, 0.0).astype(BF16)

    @pl.when(t == 0)
    def _():
        s_ref[...] = jnp.zeros(s_ref.shape, F32)

    beta_all = g_all = gates_ref[...]

    strict = (xor_idx < CHUNK) & (r > c)
    diag = r == c

    g_hi = g_all.astype(BF16)
    g_lo = (g_all - g_hi.astype(F32)).astype(BF16)
    cs = jnp.dot(mask_ref[0], jnp.concatenate([g_hi, g_lo], axis=1), preferred_element_type=F32)
    gc = cs[:, :LANES] + cs[:, LANES:]
    gc_t = gc.T
    gtot = jnp.concatenate(
        [jnp.broadcast_to(gc[(ci + 1) * CHUNK - 1:(ci + 1) * CHUNK, :], (CHUNK, LANES))
         for ci in range(nchunk)], axis=0)

    q_l, k_l, v_l, beta_l, gcol_l, gend_l, a_l, qk_l, x_l = [], [], [], [], [], [], [], [], []
    for h in heads:
        q = act_ref[:, h * GDN_DK:(h + 1) * GDN_DK]
        k = act_ref[:, GDN_QK_WIDTH + h * GDN_DK:GDN_QK_WIDTH + (h + 1) * GDN_DK]
        beta = beta_all[:, h:h + 1]
        gcol = gc[:, N_GDN_HEADS + h:N_GDN_HEADS + h + 1]
        grow = gc_t[N_GDN_HEADS + h:N_GDN_HEADS + h + 1, :]
        decay = jnp.exp(jnp.where(strict, gcol - grow, -jnp.inf))
        kb = k.astype(BF16)
        qk_kk = _dot_nt(jnp.concatenate([q.astype(BF16), kb], axis=0), kb)
        a = ((beta * qk_kk[tt:]) * decay).astype(BF16)
        qk_l.append((qk_kk[:tt] * jnp.where(diag, 1.0, decay)).astype(BF16))
        x_l.append(jnp.where(diag, 1.0, 0.0).astype(BF16) - a * mask_ref[1])
        q_l.append(q); k_l.append(k); beta_l.append(beta); gcol_l.append(gcol); a_l.append(a)
        gend_l.append(gtot[:, N_GDN_HEADS + h:N_GDN_HEADS + h + 1])
        v_l.append(act_ref[:, 2 * GDN_QK_WIDTH + h * GDN_DV:2 * GDN_QK_WIDTH + (h + 1) * GDN_DV])

    for i in range(1, len(INV_LEVELS)):
        for h in heads:
            x = x_l[h]
            y = jnp.dot(x, a_l[h], preferred_element_type=F32).astype(BF16)
            z = jnp.dot(y, x, preferred_element_type=F32).astype(BF16)
            x_l[h] = x - z * mask_ref[i + 1]

    u_l, w_l, qd_l, kd_l = [], [], [], []
    for h in heads:
        eg = jnp.exp(gcol_l[h])
        rhs = jnp.concatenate([v_l[h] * beta_l[h], k_l[h] * (beta_l[h] * eg)], axis=1)
        uw = jnp.dot(x_l[h], rhs.astype(BF16), preferred_element_type=F32)
        u_l.append(uw[:, :GDN_DV])
        w_l.append(uw[:, GDN_DV:].astype(BF16))
        qd_l.append((q_l[h] * eg).astype(BF16))
        kd_l.append((k_l[h] * jnp.exp(gend_l[h] - gcol_l[h])).astype(BF16))

    states = [s_ref[h] for h in heads]
    v_new = [[] for _ in heads]
    o_inter = [[] for _ in heads]
    for ci in range(nchunk):
        rows = slice(ci * CHUNK, (ci + 1) * CHUNK)
        for h in heads:
            sb = states[h].astype(BF16)
            ws_qs = jnp.dot(jnp.concatenate([w_l[h][rows], qd_l[h][rows]], axis=0), sb,
                            preferred_element_type=F32)
            vn = u_l[h][rows] - ws_qs[:CHUNK]
            o_inter[h].append(ws_qs[CHUNK:])
            g_last = jnp.exp(gend_l[h][ci * CHUNK:ci * CHUNK + 1, :])
            upd = lax.dot_general(kd_l[h][rows], vn.astype(BF16), (((0,), (0,)), ((), ())),
                                  preferred_element_type=F32)
            states[h] = states[h] * g_last + upd
            v_new[h].append(vn)

    for h in heads:
        s_ref[h] = states[h]
        vn_all = jnp.concatenate(v_new[h], axis=0).astype(BF16)
        o = jnp.concatenate(o_inter[h], axis=0) + jnp.dot(qk_l[h], vn_all,
                                                           preferred_element_type=F32)
        gz = gz_ref[:, h * GDN_DV:(h + 1) * GDN_DV]
        o_ref[:, h * GDN_DV:(h + 1) * GDN_DV] = _rms(o, ng_ref[...]) * _silu(gz)


def _gdn_prompt(p3, ng):
    batch, seq, _ = p3.shape
    tt = GDN_TILE
    return pl.pallas_call(
        _gdn_prompt_kernel,
        grid=(batch, seq // tt),
        in_specs=[pl.BlockSpec((None, tt, GDN_CONV_DIM), lambda b, t: (b, t, P_GQKV // GDN_CONV_DIM)),
                  pl.BlockSpec((None, tt, GDN_V_WIDTH), lambda b, t: (b, t, P_GZ // GDN_V_WIDTH)),
                  pl.BlockSpec((None, tt, LANES), lambda b, t: (b, t, P_GBA // LANES)),
                  pl.BlockSpec((1, GDN_DV), lambda b, t: (0, 0))],
        out_specs=[pl.BlockSpec((None, tt, GDN_V_WIDTH), lambda b, t: (b, t, 0)),
                   pl.BlockSpec((None, N_GDN_HEADS, GDN_DK, GDN_DV), lambda b, t: (b, 0, 0, 0))],
        out_shape=[jax.ShapeDtypeStruct((batch, seq, GDN_V_WIDTH), F32),
                   jax.ShapeDtypeStruct((batch, N_GDN_HEADS, GDN_DK, GDN_DV), F32)],
        scratch_shapes=[pltpu.VMEM((len(INV_LEVELS) + 1, tt, tt), BF16)],
        compiler_params=_params("arbitrary", "arbitrary"),
        name="gdn_prompt",
    )(p3, p3, p3, ng)


def _sample_prep_kernel(p_ref, hist_ref, convw_ref, alog_ref, dtb_ref,
                        conv_ref, qt_ref, kt_ref, v_ref, beta_ref, decay_ref):
    raw = p_ref[:, P_GQKV:P_GQKV + GDN_CONV_DIM]
    w = convw_ref[...]
    conv = w[0:1] * hist_ref[:, 0:GDN_CONV_DIM]
    for j in range(1, CONV_WIDTH - 1):
        conv = conv + w[j:j + 1] * hist_ref[:, j * GDN_CONV_DIM:(j + 1) * GDN_CONV_DIM]
    conv = conv + w[CONV_WIDTH - 1:CONV_WIDTH] * raw
    act = _silu(conv)
    conv_ref[:, 0:(CONV_WIDTH - 2) * GDN_CONV_DIM] = hist_ref[:, GDN_CONV_DIM:]
    conv_ref[:, (CONV_WIDTH - 2) * GDN_CONV_DIM:] = raw
    v_ref[...] = act[:, 2 * GDN_QK_WIDTH:]
    beta, g = _gdn_gates(p_ref[:, P_GBA:P_GBA + LANES], alog_ref[...], dtb_ref[...])
    beta_ref[...] = beta
    decay_ref[...] = jnp.exp(g)
    nblk = qt_ref.shape[0]
    for h in range(N_GDN_HEADS):
        qt = (_l2norm(act[:, h * GDN_DK:(h + 1) * GDN_DK]) * (GDN_DK ** -0.5)).T
        kt = _l2norm(act[:, GDN_QK_WIDTH + h * GDN_DK:GDN_QK_WIDTH + (h + 1) * GDN_DK]).T
        for i in range(nblk):
            qt_ref[i, h] = qt[:, i * SAMPLE_BLOCK:(i + 1) * SAMPLE_BLOCK]
            kt_ref[i, h] = kt[:, i * SAMPLE_BLOCK:(i + 1) * SAMPLE_BLOCK]


def _sample_prep(p, hist, convw, alog, dtb):
    nseq = p.shape[0]
    nblk = nseq // SAMPLE_BLOCK
    hist_w = (CONV_WIDTH - 1) * GDN_CONV_DIM
    cols = jax.ShapeDtypeStruct((nblk, N_GDN_HEADS, GDN_DK, SAMPLE_BLOCK), F32)
    return pl.pallas_call(
        _sample_prep_kernel,
        out_shape=[jax.ShapeDtypeStruct((nseq, hist_w), F32), cols, cols,
                   jax.ShapeDtypeStruct((nseq, GDN_V_WIDTH), F32),
                   jax.ShapeDtypeStruct((nseq, LANES), F32),
                   jax.ShapeDtypeStruct((nseq, LANES), F32)],
        compiler_params=pltpu.CompilerParams(vmem_limit_bytes=VMEM_LIMIT),
        name="sample_prep",
    )(p, hist, convw, alog, dtb)


def _per_group(values):
    g = lax.broadcasted_iota(jnp.int32, (GROUP, 1), 0)
    col = jnp.full((GROUP, 1), values[GROUP - 1], F32)
    for i in range(GROUP - 2, -1, -1):
        col = jnp.where(g == i, values[i], col)
    return col


def _sample_mix_kernel(sink_ref, p_ref, kh_ref, vh_ref, qt_ref, kt_ref, v_ref, beta_ref, decay_ref,
                       s_ref, ng_ref, attn_ref, gdn_ref, nk_ref, nv_ref, ns_ref):
    key_pos = lax.broadcasted_iota(jnp.int32, (1, WINDOW), 1)
    dist_hist = (WINDOW - key_pos).astype(F32)
    for b in range(SAMPLE_BLOCK):
        k_hist = kh_ref[b]
        v_hist = vh_ref[b]
        k_new = p_ref[b:b + 1, P_KV:P_KV + KV_WIDTH]
        v_new = p_ref[b:b + 1, P_KV + KV_WIDTH:P_KV + 2 * KV_WIDTH]
        nk_ref[b] = jnp.concatenate([k_hist[1:], k_new], axis=0)
        nv_ref[b] = jnp.concatenate([v_hist[1:], v_new], axis=0)
        outs = []
        for kvh in range(N_KV_HEADS):
            heads = range(kvh * GROUP, (kvh + 1) * GROUP)
            qg = jnp.concatenate(
                [p_ref[b:b + 1, P_AQ + h * HEAD_DIM:P_AQ + (h + 1) * HEAD_DIM] for h in heads],
                axis=0)
            cols = slice(kvh * HEAD_DIM, (kvh + 1) * HEAD_DIM)
            slope = _per_group([_alibi_slope(h) for h in heads])
            sink = _per_group([sink_ref[h] for h in heads])
            scale = HEAD_DIM ** -0.5
            s_hist = _dot_nt(qg, k_hist[:, cols]) * scale - slope * dist_hist
            qb = qg.astype(BF16).astype(F32)
            kb = k_new[:, cols].astype(BF16).astype(F32)
            s_new = jnp.sum(qb * kb, axis=-1, keepdims=True) * scale
            m = jnp.maximum(jnp.maximum(jnp.max(s_hist, axis=-1, keepdims=True), s_new), sink)
            p_hist = jnp.exp(s_hist - m)
            p_new = jnp.exp(s_new - m)
            denom = jnp.sum(p_hist, axis=-1, keepdims=True) + p_new + jnp.exp(sink - m)
            o = (_dot(p_hist, v_hist[:, cols]) + p_new * v_new[:, cols]) / denom
            outs.extend(o[g:g + 1] for g in range(GROUP))
        attn_ref[b:b + 1, :] = jnp.concatenate(outs, axis=1)

    for b in range(SAMPLE_BLOCK):
        for h in range(N_GDN_HEADS):
            kcol = kt_ref[h, :, b:b + 1]
            qcol = qt_ref[h, :, b:b + 1]
            v = v_ref[b:b + 1, h * GDN_DV:(h + 1) * GDN_DV]
            beta = beta_ref[b:b + 1, h:h + 1]
            decay = decay_ref[b:b + 1, N_GDN_HEADS + h:N_GDN_HEADS + h + 1]
            state = s_ref[b, h] * decay
            ks = jnp.sum(state * kcol, axis=0, keepdims=True)
            state = state + kcol * (beta * (v - ks))
            ns_ref[b, h] = state
            o = jnp.sum(state * qcol, axis=0, keepdims=True)
            gz = p_ref[b:b + 1, P_GZ + h * GDN_DV:P_GZ + (h + 1) * GDN_DV]
            gdn_ref[b:b + 1, h * GDN_DV:(h + 1) * GDN_DV] = _rms(o, ng_ref[...]) * _silu(gz)


def _sample_mix(sinks, p, k_hist, v_hist, qt, kt, v, beta, decay, state, ng):
    nseq = p.shape[0]
    bb = SAMPLE_BLOCK
    rows = lambda w: pl.BlockSpec((bb, w), lambda i: (i, 0))
    cache = pl.BlockSpec((bb, WINDOW, KV_WIDTH), lambda i: (i, 0, 0))
    cols = pl.BlockSpec((None, N_GDN_HEADS, GDN_DK, bb), lambda i: (i, 0, 0, 0))
    st = pl.BlockSpec((bb, N_GDN_HEADS, GDN_DK, GDN_DV), lambda i: (i, 0, 0, 0))
    return pl.pallas_call(
        _sample_mix_kernel,
        grid=(nseq // bb,),
        in_specs=[pl.BlockSpec(memory_space=pltpu.SMEM), rows(P_WIDTH), cache, cache, cols, cols,
                  rows(GDN_V_WIDTH), rows(LANES), rows(LANES), st,
                  pl.BlockSpec((1, GDN_DV), lambda i: (0, 0))],
        out_specs=[rows(ATTN_Q_WIDTH), rows(GDN_V_WIDTH), cache, cache, st],
        out_shape=[jax.ShapeDtypeStruct((nseq, ATTN_Q_WIDTH), F32),
                   jax.ShapeDtypeStruct((nseq, GDN_V_WIDTH), F32),
                   jax.ShapeDtypeStruct(k_hist.shape, F32),
                   jax.ShapeDtypeStruct(v_hist.shape, F32),
                   jax.ShapeDtypeStruct(state.shape, F32)],
        compiler_params=_params("parallel"),
        name="sample_mix",
    )(sinks, p, k_hist, v_hist, qt, kt, v, beta, decay, state, ng)


def _lane_pad(vec, offset):
    return jnp.zeros((1, LANES), F32).at[0, offset:offset + vec.shape[0]].set(vec)


def kernel(x_prompt, x_sample, cache_attn_k, cache_attn_v, state_conv, state_gdn, ffn1_norm_g,
           ffn1_w_in, ffn1_w_out, mix_norm_g, w_in_mix, attn_sinks, conv_w, gdn_A_log, gdn_dt_bias,
           gdn_norm_g, w_out_mix, ffn2_norm_g, ffn2_w_in, ffn2_w_out, final_norm_g):
    depth = ffn1_w_in.shape[0]
    assert depth == 1, "single-layer trunk"
    batch, seq, _ = x_prompt.shape
    nseq = x_sample.shape[0]
    assert x_sample.shape[1] == 1 and cache_attn_k.shape[2] == WINDOW
    assert seq % GDN_TILE == 0 and seq % WINDOW == 0 and nseq % SAMPLE_BLOCK == 0
    l = 0
    row = lambda v: v.reshape(1, -1)
    g1, gm, g2, gf = row(ffn1_norm_g[l]), row(mix_norm_g[l]), row(ffn2_norm_g[l]), row(final_norm_g)
    ng = row(gdn_norm_g[l])
    wgu1, wo1 = ffn1_w_in[l].astype(BF16), ffn1_w_out[l].astype(BF16)
    wgu2, wo2 = ffn2_w_in[l].astype(BF16), ffn2_w_out[l].astype(BF16)
    win = jnp.pad(w_in_mix[l].astype(BF16), ((0, 0), (0, P_WIDTH - w_in_mix.shape[2])))
    wa = w_out_mix[l, :ATTN_Q_WIDTH].astype(BF16)
    wb = w_out_mix[l, ATTN_Q_WIDTH:].astype(BF16)
    alog = _lane_pad(gdn_A_log[l], N_GDN_HEADS)
    dtb = _lane_pad(gdn_dt_bias[l], N_GDN_HEADS)
    sinks = attn_sinks[l]
    convw = conv_w[l]

    xp = x_prompt.reshape(batch * seq, D_MODEL)
    x1p, pp, conv_tail = _ffn_mix_in(xp, g1, wgu1, wo1, gm, win, TOKEN_TILE,
                                     conv=(convw, alog, dtb, seq))
    pp3 = pp.reshape(batch, seq, P_WIDTH)
    attn_p = _swa_prompt(sinks, pp3)
    gdn_p, state_p = _gdn_prompt(pp3, ng)
    y_p = _mix_out_ffn(x1p, attn_p.reshape(batch * seq, ATTN_Q_WIDTH),
                       gdn_p.reshape(batch * seq, GDN_V_WIDTH), wa, wb, g2, wgu2, wo2, gf, TOKEN_TILE)
    tail = pp3[:, seq - WINDOW:, P_KV:P_KV + 2 * KV_WIDTH]
    new_k_p = tail[:, :, :KV_WIDTH].reshape(1, batch, WINDOW, N_KV_HEADS, HEAD_DIM)
    new_v_p = tail[:, :, KV_WIDTH:].reshape(1, batch, WINDOW, N_KV_HEADS, HEAD_DIM)
    new_conv_p = conv_tail[:, SUBLANES - (CONV_WIDTH - 1):][None]

    xs = x_sample.reshape(nseq, D_MODEL)
    x1s, ps = _ffn_mix_in(xs, g1, wgu1, wo1, gm, win, nseq)
    hist = state_conv[l].reshape(nseq, (CONV_WIDTH - 1) * GDN_CONV_DIM)
    new_conv_s, qt, kt, v_s, beta_s, decay_s = _sample_prep(ps, hist, convw, alog, dtb)
    attn_s, gdn_s, new_k_s, new_v_s, state_s = _sample_mix(
        sinks, ps, cache_attn_k[l].reshape(nseq, WINDOW, KV_WIDTH),
        cache_attn_v[l].reshape(nseq, WINDOW, KV_WIDTH), qt, kt, v_s, beta_s, decay_s,
        state_gdn[l], ng)
    y_s = _mix_out_ffn(x1s, attn_s, gdn_s, wa, wb, g2, wgu2, wo2, gf, nseq)

    kv_shape = (1, nseq, WINDOW, N_KV_HEADS, HEAD_DIM)
    return (y_p.reshape(batch, seq, D_MODEL), y_s.reshape(nseq, 1, D_MODEL),
            new_k_p, new_v_p, new_conv_p, state_p[None],
            new_k_s.reshape(kv_shape), new_v_s.reshape(kv_shape),
            new_conv_s.reshape(1, nseq, CONV_WIDTH - 1, GDN_CONV_DIM), state_s[None])
```

```python
import functools

import jax
import jax.numpy as jnp
from jax import lax
from jax.experimental import pallas as pl
from jax.experimental.pallas import tpu as pltpu

F32 = jnp.float32
BF16 = jnp.bfloat16

D_MODEL = 1024
D_FF = 2816
N_Q_HEADS = 8
N_KV_HEADS = 2
GROUP = N_Q_HEADS // N_KV_HEADS
HEAD_DIM = 64
WINDOW = 128
ATTN_Q_WIDTH = N_Q_HEADS * HEAD_DIM
KV_WIDTH = N_KV_HEADS * HEAD_DIM
N_GDN_HEADS = 4
GDN_DK = 128
GDN_DV = 128
GDN_QK_WIDTH = N_GDN_HEADS * GDN_DK
GDN_V_WIDTH = N_GDN_HEADS * GDN_DV
GDN_CONV_DIM = 2 * GDN_QK_WIDTH + GDN_V_WIDTH
CONV_WIDTH = 4
CHUNK = 64
EPS = 1e-6

LANES = 128
SUBLANES = 8
VMEM_LIMIT = 56 * 1024 * 1024

P_GQKV = 0
P_AQ = P_GQKV + GDN_CONV_DIM
P_GZ = P_AQ + ATTN_Q_WIDTH
P_KV = P_GZ + GDN_V_WIDTH
P_GBA = P_KV + 2 * KV_WIDTH
P_WIDTH = P_GBA + LANES

FF_CHUNK = 256
TOKEN_TILE = 512
GDN_TILE = 256
GDN_SEQS = 4
SWA_TILE = 512
CONV_ROWS = 64
SAMPLE_BLOCK = 8


def _sigmoid(x):
    return 1.0 / (1.0 + jnp.exp(-x))


def _silu(x):
    return x * _sigmoid(x)


def _softplus(x):
    return jnp.maximum(x, 0.0) + jnp.log1p(jnp.exp(-jnp.abs(x)))


def _rms(x, g):
    return x * lax.rsqrt(jnp.mean(x * x, axis=-1, keepdims=True) + EPS) * g


def _dot(a, b):
    return jnp.dot(a.astype(BF16), b.astype(BF16), preferred_element_type=F32)


def _dot_nt(a, b):
    return lax.dot_general(a.astype(BF16), b.astype(BF16), (((1,), (1,)), ((), ())),
                           preferred_element_type=F32)


def _dot_exact(a, b):
    return jnp.dot(a, b, preferred_element_type=F32, precision=lax.Precision.HIGHEST)


def _resident(shape):
    return pl.BlockSpec(shape, lambda *_: (0,) * len(shape), pipeline_mode=pl.Buffered(1))


def _params(*semantics):
    return pltpu.CompilerParams(dimension_semantics=semantics, vmem_limit_bytes=VMEM_LIMIT)


FF_STEPS = D_FF // FF_CHUNK


def _ffn_half_step(x, g, wgu_ref, wo_ref):
    n = _rms(x, g).astype(BF16)
    acc = jnp.zeros(x.shape, F32)
    for c in range(FF_STEPS):
        lo = c * FF_CHUNK
        gate = jnp.dot(n, wgu_ref[:, lo:lo + FF_CHUNK], preferred_element_type=F32)
        up = jnp.dot(n, wgu_ref[:, D_FF + lo:D_FF + lo + FF_CHUNK], preferred_element_type=F32)
        h = (_silu(gate) * up).astype(BF16)
        acc = acc + jnp.dot(h, wo_ref[lo:lo + FF_CHUNK, :], preferred_element_type=F32)
    return x + 0.5 * acc


_W_AQ, _W_KV = 0, ATTN_Q_WIDTH
_W_GQKV = _W_KV + 2 * KV_WIDTH
_W_GZ = _W_GQKV + GDN_CONV_DIM
_W_GBA = _W_GZ + GDN_V_WIDTH
_PROJ_GROUPS = ((_W_GQKV, GDN_CONV_DIM, P_GQKV), (_W_GBA, LANES, P_GBA), (_W_AQ, ATTN_Q_WIDTH, P_AQ),
                (_W_GZ, GDN_V_WIDTH, P_GZ), (_W_KV, 2 * KV_WIDTH, P_KV))


def _gdn_gates(gba, alog, dtb):
    beta = _sigmoid(gba)
    g = -jnp.exp(alog) * _softplus(gba + dtb)
    return beta, g


def _l2norm(x):
    return x * lax.rsqrt(jnp.sum(x * x, axis=-1, keepdims=True) + EPS)


def _ffn_mix_in_kernel(x_ref, g1_ref, wgu_ref, wo_ref, gm_ref, win_ref, x1_ref, p_ref):
    x1 = _ffn_half_step(x_ref[...], g1_ref[...], wgu_ref, wo_ref)
    x1_ref[...] = x1
    n = _rms(x1, gm_ref[...]).astype(BF16)
    for src, width, dst in _PROJ_GROUPS:
        p_ref[:, dst:dst + width] = jnp.dot(n, win_ref[:, src:src + width],
                                            preferred_element_type=F32)


def _ffn_weight_specs():
    return [_resident((1, D_MODEL)), _resident((D_MODEL, 2 * D_FF)), _resident((D_FF, D_MODEL)),
            _resident((1, D_MODEL)), _resident((D_MODEL, P_WIDTH))]


def _ffn_mix_in(x, g1, wgu, wo, gm, win, tm):
    n_tok = x.shape[0]
    row = lambda w: pl.BlockSpec((tm, w), lambda i: (i, 0))
    return pl.pallas_call(
        _ffn_mix_in_kernel,
        grid=(n_tok // tm,),
        in_specs=[row(D_MODEL)] + _ffn_weight_specs(),
        out_specs=[row(D_MODEL), row(P_WIDTH)],
        out_shape=[jax.ShapeDtypeStruct((n_tok, D_MODEL), F32),
                   jax.ShapeDtypeStruct((n_tok, P_WIDTH), F32)],
        compiler_params=_params("parallel"),
        name="ffn_mix_in",
    )(x, g1, wgu, wo, gm, win)


def _mix_out_ffn_kernel(x1_ref, attn_ref, gdn_ref, wa_ref, wb_ref, g2_ref, wgu_ref, wo_ref, gf_ref,
                        y_ref):
    x2 = x1_ref[...] + _dot(attn_ref[...], wa_ref[...]) + _dot(gdn_ref[...], wb_ref[...])
    x3 = _ffn_half_step(x2, g2_ref[...], wgu_ref, wo_ref)
    y_ref[...] = _rms(x3, gf_ref[...])


def _mix_out_ffn(x1, attn, gdn, wa, wb, g2, wgu, wo, gf, tm):
    n_tok = x1.shape[0]
    row = lambda w: pl.BlockSpec((tm, w), lambda i: (i, 0))
    return pl.pallas_call(
        _mix_out_ffn_kernel,
        grid=(n_tok // tm,),
        in_specs=[row(D_MODEL), row(ATTN_Q_WIDTH), row(GDN_V_WIDTH),
                  _resident((ATTN_Q_WIDTH, D_MODEL)), _resident((GDN_V_WIDTH, D_MODEL)),
                  _resident((1, D_MODEL)), _resident((D_MODEL, 2 * D_FF)),
                  _resident((D_FF, D_MODEL)), _resident((1, D_MODEL))],
        out_specs=row(D_MODEL),
        out_shape=jax.ShapeDtypeStruct((n_tok, D_MODEL), F32),
        compiler_params=_params("parallel"),
        name="mix_out_ffn",
    )(x1, attn, gdn, wa, wb, g2, wgu, wo, gf)


def _alibi_slope(head):
    return 2.0 ** (-8.0 * (head + 1) / N_Q_HEADS)


def _swa_prompt_kernel(sink_ref, q_ref, kvc_ref, kvp_ref, o_ref, bias_ref, s_ref):
    blk = pl.program_id(1)
    nkeys = 2 * WINDOW

    @pl.when((pl.program_id(0) == 0) & (blk == 0))
    def _():
        key = lax.broadcasted_iota(jnp.int32, (nkeys, WINDOW), 0)
        qry = lax.broadcasted_iota(jnp.int32, (nkeys, WINDOW), 1)
        dist = qry - key + WINDOW
        valid = (dist >= 0) & (dist <= WINDOW)
        for h in range(N_Q_HEADS):
            g = h % GROUP
            bias_ref[h // GROUP, :, g * WINDOW:(g + 1) * WINDOW] = jnp.where(
                valid, -_alibi_slope(h) * dist.astype(F32), -jnp.inf)

    first_pen = jnp.where(blk > 0, 0.0, -jnp.inf)
    scale = HEAD_DIM ** -0.5
    nsub = q_ref.shape[0] // WINDOW
    jobs = [(j, kvh) for j in range(nsub) for kvh in range(N_KV_HEADS)]
    group_heads = lambda kvh: range(kvh * GROUP, (kvh + 1) * GROUP)
    def keys_values(j, col):
        if j == 0:
            return jnp.concatenate([kvp_ref[:, col], kvc_ref[0:WINDOW, col]], axis=0)
        return kvc_ref[(j - 1) * WINDOW:(j + 1) * WINDOW, col]

    for job, (j, kvh) in enumerate(jobs):
        rows = slice(j * WINDOW, (j + 1) * WINDOW)
        k_all = keys_values(j, slice(kvh * HEAD_DIM, (kvh + 1) * HEAD_DIM)).astype(BF16)
        v_t = keys_values(j, slice(KV_WIDTH + kvh * HEAD_DIM,
                                   KV_WIDTH + (kvh + 1) * HEAD_DIM)).T.astype(BF16)
        q_stack = jnp.concatenate(
            [(q_ref[rows, h * HEAD_DIM:(h + 1) * HEAD_DIM] * scale).astype(BF16)
             for h in group_heads(kvh)], axis=0)
        s_ref[job] = lax.dot_general(k_all, q_stack, (((1,), (1,)), ((), ())),
                                     preferred_element_type=F32)
        for g, h in enumerate(group_heads(kvh)):
            lanes = slice(g * WINDOW, (g + 1) * WINDOW)
            s = s_ref[job, :, lanes] + bias_ref[kvh, :, lanes]
            if j == 0:
                s = jnp.concatenate([s[:WINDOW] + first_pen, s[WINDOW:]], axis=0)
            sink = sink_ref[h]
            m = jnp.maximum(jnp.max(s, axis=0, keepdims=True), sink)
            p = jnp.exp(s - m)
            denom = jnp.sum(p, axis=0, keepdims=True) + jnp.exp(sink - m)
            o_t = jnp.dot(v_t, p.astype(BF16), preferred_element_type=F32) / denom
            o_ref[rows, h * HEAD_DIM:(h + 1) * HEAD_DIM] = o_t.T


def _swa_prompt(sinks, p3):
    batch, seq, _ = p3.shape
    tq = SWA_TILE
    nsub = tq // WINDOW
    q_blk = P_AQ // ATTN_Q_WIDTH
    kv_blk = P_KV // (2 * KV_WIDTH)
    return pl.pallas_call(
        _swa_prompt_kernel,
        grid=(batch, seq // tq),
        in_specs=[pl.BlockSpec(memory_space=pltpu.SMEM),
                  pl.BlockSpec((None, tq, ATTN_Q_WIDTH), lambda b, i: (b, i, q_blk)),
                  pl.BlockSpec((None, tq, 2 * KV_WIDTH), lambda b, i: (b, i, kv_blk)),
                  pl.BlockSpec((None, WINDOW, 2 * KV_WIDTH),
                               lambda b, i: (b, jnp.maximum(i * nsub - 1, 0), kv_blk))],
        out_specs=pl.BlockSpec((None, tq, ATTN_Q_WIDTH), lambda b, i: (b, i, 0)),
        out_shape=jax.ShapeDtypeStruct((batch, seq, ATTN_Q_WIDTH), F32),
        scratch_shapes=[pltpu.VMEM((N_KV_HEADS, 2 * WINDOW, GROUP * WINDOW), F32),
                        pltpu.VMEM((nsub * N_KV_HEADS, 2 * WINDOW, GROUP * WINDOW), F32)],
        compiler_params=_params("arbitrary", "arbitrary"),
        name="swa_prompt",
    )(sinks, p3, p3, p3)


INV_LEVELS = tuple(2 ** i for i in range(1, CHUNK.bit_length()))


def _gdn_prompt_kernel(raw_ref, prev_ref, gz_ref, gba_ref, convw_ref, alog_ref, dtb_ref, ng_ref,
                       o_ref, s_ref, mask_ref, act_ref):
    t = pl.program_id(1)
    tt = GDN_TILE
    nchunk = tt // CHUNK
    heads = range(N_GDN_HEADS)

    r = lax.broadcasted_iota(jnp.int32, (tt, tt), 0)
    c = lax.broadcasted_iota(jnp.int32, (tt, tt), 1)
    xor_idx = r ^ c

    @pl.when((pl.program_id(0) == 0) & (t == 0))
    def _():
        mask_ref[0] = jnp.where((xor_idx < CHUNK) & (r >= c), 1.0, 0.0).astype(BF16)
        for i, s in enumerate(INV_LEVELS):
            mask_ref[i + 1] = jnp.where((xor_idx >= s // 2) & (xor_idx < s) & (r > c),
                                        1.0, 0.0).astype(BF16)

    @pl.when(t == 0)
    def _():
        s_ref[...] = jnp.zeros(s_ref.shape, F32)

    strict = (xor_idx < CHUNK) & (r > c)
    diag = r == c

    nseq = raw_ref.shape[0]
    chains = [(sq, h) for sq in range(nseq) for h in heads]
    ids = range(len(chains))

    base = SUBLANES - (CONV_WIDTH - 1)
    for sq in range(nseq):
        for cg in range(GDN_CONV_DIM // LANES):
            cols = slice(cg * LANES, (cg + 1) * LANES)
            w = convw_ref[:, cols]
            for r0 in range(0, tt, CONV_ROWS):
                if r0 == 0:
                    before = jnp.where(t > 0, prev_ref[sq, :, cols], 0.0)
                else:
                    before = raw_ref[sq, r0 - SUBLANES:r0, cols]
                cur = raw_ref[sq, r0:r0 + CONV_ROWS, cols]
                ext = jnp.concatenate([before, cur], axis=0)
                acc = w[0:1] * ext[base:base + CONV_ROWS]
                for j in range(1, CONV_WIDTH - 1):
                    acc = acc + w[j:j + 1] * ext[base + j:base + j + CONV_ROWS]
                act = _silu(acc + w[CONV_WIDTH - 1:CONV_WIDTH] * cur)
                if cg < GDN_QK_WIDTH // LANES:
                    act = _l2norm(act) * (GDN_DK ** -0.5)
                elif cg < 2 * GDN_QK_WIDTH // LANES:
                    act = _l2norm(act)
                act_ref[sq, r0:r0 + CONV_ROWS, cols] = act

    gates, gc, gc_t, gtot = [], [], [], []
    for sq in range(nseq):
        beta_all, g_all = _gdn_gates(gba_ref[sq], alog_ref[...], dtb_ref[...])
        g_hi = g_all.astype(BF16)
        g_lo = (g_all - g_hi.astype(F32)).astype(BF16)
        cs = jnp.dot(mask_ref[0], jnp.concatenate([g_hi, g_lo], axis=1),
                     preferred_element_type=F32)
        gc_sq = cs[:, :LANES] + cs[:, LANES:]
        gates.append(beta_all)
        gc.append(gc_sq)
        gc_t.append(gc_sq.T)
        gtot.append(jnp.concatenate(
            [jnp.broadcast_to(gc_sq[(ci + 1) * CHUNK - 1:(ci + 1) * CHUNK, :], (CHUNK, LANES))
             for ci in range(nchunk)], axis=0))

    q_l, k_l, v_l, beta_l, gcol_l, gend_l, a_l, qk_l, x_l = [], [], [], [], [], [], [], [], []
    for sq, h in chains:
        q = act_ref[sq, :, h * GDN_DK:(h + 1) * GDN_DK]
        k = act_ref[sq, :, GDN_QK_WIDTH + h * GDN_DK:GDN_QK_WIDTH + (h + 1) * GDN_DK]
        beta = gates[sq][:, h:h + 1]
        gcol = gc[sq][:, N_GDN_HEADS + h:N_GDN_HEADS + h + 1]
        grow = gc_t[sq][N_GDN_HEADS + h:N_GDN_HEADS + h + 1, :]
        decay = jnp.exp(jnp.where(strict, gcol - grow, -jnp.inf))
        kb = k.astype(BF16)
        qk_kk = _dot_nt(jnp.concatenate([q.astype(BF16), kb], axis=0), kb)
        a = ((beta * qk_kk[tt:]) * decay).astype(BF16)
        qk_l.append((qk_kk[:tt] * jnp.where(diag, 1.0, decay)).astype(BF16))
        x_l.append(jnp.where(diag, 1.0, 0.0).astype(BF16) - a * mask_ref[1])
        q_l.append(q); k_l.append(k); beta_l.append(beta); gcol_l.append(gcol); a_l.append(a)
        gend_l.append(gtot[sq][:, N_GDN_HEADS + h:N_GDN_HEADS + h + 1])
        v_l.append(act_ref[sq, :, 2 * GDN_QK_WIDTH + h * GDN_DV:2 * GDN_QK_WIDTH + (h + 1) * GDN_DV])

    for i in range(1, len(INV_LEVELS)):
        for n in ids:
            x = x_l[n]
            y = jnp.dot(x, a_l[n], preferred_element_type=F32).astype(BF16)
            z = jnp.dot(y, x, preferred_element_type=F32).astype(BF16)
            x_l[n] = x - z * mask_ref[i + 1]

    u_l, w_l, qd_l, kd_l = [], [], [], []
    for n in ids:
        eg = jnp.exp(gcol_l[n])
        rhs = jnp.concatenate([v_l[n] * beta_l[n], k_l[n] * (beta_l[n] * eg)], axis=1)
        uw = jnp.dot(x_l[n], rhs.astype(BF16), preferred_element_type=F32)
        u_l.append(uw[:, :GDN_DV])
        w_l.append(uw[:, GDN_DV:].astype(BF16))
        qd_l.append((q_l[n] * eg).astype(BF16))
        kd_l.append((k_l[n] * jnp.exp(gend_l[n] - gcol_l[n])).astype(BF16))

    states = [s_ref[sq, h] for sq, h in chains]
    v_new = [[] for _ in ids]
    o_inter = [[] for _ in ids]
    for ci in range(nchunk):
        rows = slice(ci * CHUNK, (ci + 1) * CHUNK)
        for n in ids:
            sb = states[n].astype(BF16)
            ws_qs = jnp.dot(jnp.concatenate([w_l[n][rows], qd_l[n][rows]], axis=0), sb,
                            preferred_element_type=F32)
            vn = u_l[n][rows] - ws_qs[:CHUNK]
            o_inter[n].append(ws_qs[CHUNK:])
            g_last = jnp.exp(gend_l[n][ci * CHUNK:ci * CHUNK + 1, :])
            upd = lax.dot_general(kd_l[n][rows], vn.astype(BF16), (((0,), (0,)), ((), ())),
                                  preferred_element_type=F32)
            states[n] = states[n] * g_last + upd
            v_new[n].append(vn)

    for n, (sq, h) in enumerate(chains):
        s_ref[sq, h] = states[n]
        vn_all = jnp.concatenate(v_new[n], axis=0).astype(BF16)
        o = jnp.concatenate(o_inter[n], axis=0) + jnp.dot(qk_l[n], vn_all,
                                                           preferred_element_type=F32)
        gz = gz_ref[sq, :, h * GDN_DV:(h + 1) * GDN_DV]
        o_ref[sq, :, h * GDN_DV:(h + 1) * GDN_DV] = _rms(o, ng_ref[...]) * _silu(gz)


def _gdn_prompt(p3, convw, alog, dtb, ng):
    batch, seq, _ = p3.shape
    tt = GDN_TILE
    ns = GDN_SEQS if batch % GDN_SEQS == 0 else 1
    prev_per_tile = tt // SUBLANES
    small = lambda shape: pl.BlockSpec(shape, lambda b, t: (0,) * len(shape))
    return pl.pallas_call(
        _gdn_prompt_kernel,
        grid=(batch // ns, seq // tt),
        in_specs=[pl.BlockSpec((ns, tt, GDN_CONV_DIM), lambda b, t: (b, t, P_GQKV // GDN_CONV_DIM)),
                  pl.BlockSpec((ns, SUBLANES, GDN_CONV_DIM),
                               lambda b, t: (b, jnp.maximum(t * prev_per_tile - 1, 0),
                                             P_GQKV // GDN_CONV_DIM)),
                  pl.BlockSpec((ns, tt, GDN_V_WIDTH), lambda b, t: (b, t, P_GZ // GDN_V_WIDTH)),
                  pl.BlockSpec((ns, tt, LANES), lambda b, t: (b, t, P_GBA // LANES)),
                  small((CONV_WIDTH, GDN_CONV_DIM)), small((1, LANES)), small((1, LANES)),
                  small((1, GDN_DV))],
        out_specs=[pl.BlockSpec((ns, tt, GDN_V_WIDTH), lambda b, t: (b, t, 0)),
                   pl.BlockSpec((ns, N_GDN_HEADS, GDN_DK, GDN_DV), lambda b, t: (b, 0, 0, 0))],
        out_shape=[jax.ShapeDtypeStruct((batch, seq, GDN_V_WIDTH), F32),
                   jax.ShapeDtypeStruct((batch, N_GDN_HEADS, GDN_DK, GDN_DV), F32)],
        scratch_shapes=[pltpu.VMEM((len(INV_LEVELS) + 1, tt, tt), BF16),
                        pltpu.VMEM((ns, tt, GDN_CONV_DIM), F32)],
        compiler_params=_params("arbitrary", "arbitrary"),
        name="gdn_prompt",
    )(p3, p3, p3, p3, convw, alog, dtb, ng)


def _sample_prep_kernel(p_ref, hist_ref, convw_ref, alog_ref, dtb_ref,
                        conv_ref, qt_ref, kt_ref, v_ref, beta_ref, decay_ref):
    raw = p_ref[:, P_GQKV:P_GQKV + GDN_CONV_DIM]
    w = convw_ref[...]
    conv = w[0:1] * hist_ref[:, 0:GDN_CONV_DIM]
    for j in range(1, CONV_WIDTH - 1):
        conv = conv + w[j:j + 1] * hist_ref[:, j * GDN_CONV_DIM:(j + 1) * GDN_CONV_DIM]
    conv = conv + w[CONV_WIDTH - 1:CONV_WIDTH] * raw
    act = _silu(conv)
    conv_ref[:, 0:(CONV_WIDTH - 2) * GDN_CONV_DIM] = hist_ref[:, GDN_CONV_DIM:]
    conv_ref[:, (CONV_WIDTH - 2) * GDN_CONV_DIM:] = raw
    v_ref[...] = act[:, 2 * GDN_QK_WIDTH:]
    beta, g = _gdn_gates(p_ref[:, P_GBA:P_GBA + LANES], alog_ref[...], dtb_ref[...])
    beta_ref[...] = beta
    decay_ref[...] = jnp.exp(g)
    nblk = qt_ref.shape[0]
    for h in range(N_GDN_HEADS):
        qt = (_l2norm(act[:, h * GDN_DK:(h + 1) * GDN_DK]) * (GDN_DK ** -0.5)).T
        kt = _l2norm(act[:, GDN_QK_WIDTH + h * GDN_DK:GDN_QK_WIDTH + (h + 1) * GDN_DK]).T
        for i in range(nblk):
            qt_ref[i, h] = qt[:, i * SAMPLE_BLOCK:(i + 1) * SAMPLE_BLOCK]
            kt_ref[i, h] = kt[:, i * SAMPLE_BLOCK:(i + 1) * SAMPLE_BLOCK]


def _sample_prep(p, hist, convw, alog, dtb):
    nseq = p.shape[0]
    nblk = nseq // SAMPLE_BLOCK
    hist_w = (CONV_WIDTH - 1) * GDN_CONV_DIM
    cols = jax.ShapeDtypeStruct((nblk, N_GDN_HEADS, GDN_DK, SAMPLE_BLOCK), F32)
    return pl.pallas_call(
        _sample_prep_kernel,
        out_shape=[jax.ShapeDtypeStruct((nseq, hist_w), F32), cols, cols,
                   jax.ShapeDtypeStruct((nseq, GDN_V_WIDTH), F32),
                   jax.ShapeDtypeStruct((nseq, LANES), F32),
                   jax.ShapeDtypeStruct((nseq, LANES), F32)],
        compiler_params=pltpu.CompilerParams(vmem_limit_bytes=VMEM_LIMIT),
        name="sample_prep",
    )(p, hist, convw, alog, dtb)


def _per_group(values):
    g = lax.broadcasted_iota(jnp.int32, (GROUP, 1), 0)
    col = jnp.full((GROUP, 1), values[GROUP - 1], F32)
    for i in range(GROUP - 2, -1, -1):
        col = jnp.where(g == i, values[i], col)
    return col


def _sample_mix_kernel(sink_ref, p_ref, kh_ref, vh_ref, qt_ref, kt_ref, v_ref, beta_ref, decay_ref,
                       s_ref, ng_ref, attn_ref, gdn_ref, nk_ref, nv_ref, ns_ref):
    key_pos = lax.broadcasted_iota(jnp.int32, (1, WINDOW), 1)
    dist_hist = (WINDOW - key_pos).astype(F32)
    for b in range(SAMPLE_BLOCK):
        k_hist = kh_ref[b]
        v_hist = vh_ref[b]
        k_new = p_ref[b:b + 1, P_KV:P_KV + KV_WIDTH]
        v_new = p_ref[b:b + 1, P_KV + KV_WIDTH:P_KV + 2 * KV_WIDTH]
        nk_ref[b] = jnp.concatenate([k_hist[1:], k_new], axis=0)
        nv_ref[b] = jnp.concatenate([v_hist[1:], v_new], axis=0)
        outs = []
        for kvh in range(N_KV_HEADS):
            heads = range(kvh * GROUP, (kvh + 1) * GROUP)
            qg = jnp.concatenate(
                [p_ref[b:b + 1, P_AQ + h * HEAD_DIM:P_AQ + (h + 1) * HEAD_DIM] for h in heads],
                axis=0)
            cols = slice(kvh * HEAD_DIM, (kvh + 1) * HEAD_DIM)
            slope = _per_group([_alibi_slope(h) for h in heads])
            sink = _per_group([sink_ref[h] for h in heads])
            scale = HEAD_DIM ** -0.5
            s_hist = _dot_nt(qg, k_hist[:, cols]) * scale - slope * dist_hist
            qb = qg.astype(BF16).astype(F32)
            kb = k_new[:, cols].astype(BF16).astype(F32)
            s_new = jnp.sum(qb * kb, axis=-1, keepdims=True) * scale
            m = jnp.maximum(jnp.maximum(jnp.max(s_hist, axis=-1, keepdims=True), s_new), sink)
            p_hist = jnp.exp(s_hist - m)
            p_new = jnp.exp(s_new - m)
            denom = jnp.sum(p_hist, axis=-1, keepdims=True) + p_new + jnp.exp(sink - m)
            o = (_dot(p_hist, v_hist[:, cols]) + p_new * v_new[:, cols]) / denom
            outs.extend(o[g:g + 1] for g in range(GROUP))
        attn_ref[b:b + 1, :] = jnp.concatenate(outs, axis=1)

    for b in range(SAMPLE_BLOCK):
        for h in range(N_GDN_HEADS):
            kcol = kt_ref[h, :, b:b + 1]
            qcol = qt_ref[h, :, b:b + 1]
            v = v_ref[b:b + 1, h * GDN_DV:(h + 1) * GDN_DV]
            beta = beta_ref[b:b + 1, h:h + 1]
            decay = decay_ref[b:b + 1, N_GDN_HEADS + h:N_GDN_HEADS + h + 1]
            state = s_ref[b, h] * decay
            ks = jnp.sum(state * kcol, axis=0, keepdims=True)
            state = state + kcol * (beta * (v - ks))
            ns_ref[b, h] = state
            o = jnp.sum(state * qcol, axis=0, keepdims=True)
            gz = p_ref[b:b + 1, P_GZ + h * GDN_DV:P_GZ + (h + 1) * GDN_DV]
            gdn_ref[b:b + 1, h * GDN_DV:(h + 1) * GDN_DV] = _rms(o, ng_ref[...]) * _silu(gz)


def _sample_mix(sinks, p, k_hist, v_hist, qt, kt, v, beta, decay, state, ng):
    nseq = p.shape[0]
    bb = SAMPLE_BLOCK
    rows = lambda w: pl.BlockSpec((bb, w), lambda i: (i, 0))
    cache = pl.BlockSpec((bb, WINDOW, KV_WIDTH), lambda i: (i, 0, 0))
    cols = pl.BlockSpec((None, N_GDN_HEADS, GDN_DK, bb), lambda i: (i, 0, 0, 0))
    st = pl.BlockSpec((bb, N_GDN_HEADS, GDN_DK, GDN_DV), lambda i: (i, 0, 0, 0))
    return pl.pallas_call(
        _sample_mix_kernel,
        grid=(nseq // bb,),
        in_specs=[pl.BlockSpec(memory_space=pltpu.SMEM), rows(P_WIDTH), cache, cache, cols, cols,
                  rows(GDN_V_WIDTH), rows(LANES), rows(LANES), st,
                  pl.BlockSpec((1, GDN_DV), lambda i: (0, 0))],
        out_specs=[rows(ATTN_Q_WIDTH), rows(GDN_V_WIDTH), cache, cache, st],
        out_shape=[jax.ShapeDtypeStruct((nseq, ATTN_Q_WIDTH), F32),
                   jax.ShapeDtypeStruct((nseq, GDN_V_WIDTH), F32),
                   jax.ShapeDtypeStruct(k_hist.shape, F32),
                   jax.ShapeDtypeStruct(v_hist.shape, F32),
                   jax.ShapeDtypeStruct(state.shape, F32)],
        compiler_params=_params("parallel"),
        name="sample_mix",
    )(sinks, p, k_hist, v_hist, qt, kt, v, beta, decay, state, ng)


def _lane_pad(vec, offset):
    return jnp.zeros((1, LANES), F32).at[0, offset:offset + vec.shape[0]].set(vec)


def kernel(x_prompt, x_sample, cache_attn_k, cache_attn_v, state_conv, state_gdn, ffn1_norm_g,
           ffn1_w_in, ffn1_w_out, mix_norm_g, w_in_mix, attn_sinks, conv_w, gdn_A_log, gdn_dt_bias,
           gdn_norm_g, w_out_mix, ffn2_norm_g, ffn2_w_in, ffn2_w_out, final_norm_g):
    depth = ffn1_w_in.shape[0]
    assert depth == 1, "single-layer trunk"
    batch, seq, _ = x_prompt.shape
    nseq = x_sample.shape[0]
    assert x_sample.shape[1] == 1 and cache_attn_k.shape[2] == WINDOW
    assert seq % GDN_TILE == 0 and seq % WINDOW == 0 and nseq % SAMPLE_BLOCK == 0
    l = 0
    row = lambda v: v.reshape(1, -1)
    g1, gm, g2, gf = row(ffn1_norm_g[l]), row(mix_norm_g[l]), row(ffn2_norm_g[l]), row(final_norm_g)
    ng = row(gdn_norm_g[l])
    wgu1, wo1 = ffn1_w_in[l].astype(BF16), ffn1_w_out[l].astype(BF16)
    wgu2, wo2 = ffn2_w_in[l].astype(BF16), ffn2_w_out[l].astype(BF16)
    win = jnp.pad(w_in_mix[l].astype(BF16), ((0, 0), (0, P_WIDTH - w_in_mix.shape[2])))
    wa = w_out_mix[l, :ATTN_Q_WIDTH].astype(BF16)
    wb = w_out_mix[l, ATTN_Q_WIDTH:].astype(BF16)
    alog = _lane_pad(gdn_A_log[l], N_GDN_HEADS)
    dtb = _lane_pad(gdn_dt_bias[l], N_GDN_HEADS)
    sinks = attn_sinks[l]
    convw = conv_w[l]

    xp = x_prompt.reshape(batch * seq, D_MODEL)
    x1p, pp = _ffn_mix_in(xp, g1, wgu1, wo1, gm, win, TOKEN_TILE)
    pp3 = pp.reshape(batch, seq, P_WIDTH)
    attn_p = _swa_prompt(sinks, pp3)
    gdn_p, state_p = _gdn_prompt(pp3, convw, alog, dtb, ng)
    y_p = _mix_out_ffn(x1p, attn_p.reshape(batch * seq, ATTN_Q_WIDTH),
                       gdn_p.reshape(batch * seq, GDN_V_WIDTH), wa, wb, g2, wgu2, wo2, gf, TOKEN_TILE)
    tail = pp3[:, seq - WINDOW:, P_KV:P_KV + 2 * KV_WIDTH]
    new_k_p = tail[:, :, :KV_WIDTH].reshape(1, batch, WINDOW, N_KV_HEADS, HEAD_DIM)
    new_v_p = tail[:, :, KV_WIDTH:].reshape(1, batch, WINDOW, N_KV_HEADS, HEAD_DIM)
    new_conv_p = pp3[:, seq - (CONV_WIDTH - 1):, P_GQKV:P_GQKV + GDN_CONV_DIM][None]

    xs = x_sample.reshape(nseq, D_MODEL)
    x1s, ps = _ffn_mix_in(xs, g1, wgu1, wo1, gm, win, nseq)
    hist = state_conv[l].reshape(nseq, (CONV_WIDTH - 1) * GDN_CONV_DIM)
    new_conv_s, qt, kt, v_s, beta_s, decay_s = _sample_prep(ps, hist, convw, alog, dtb)
    attn_s, gdn_s, new_k_s, new_v_s, state_s = _sample_mix(
        sinks, ps, cache_attn_k[l].reshape(nseq, WINDOW, KV_WIDTH),
        cache_attn_v[l].reshape(nseq, WINDOW, KV_WIDTH), qt, kt, v_s, beta_s, decay_s,
        state_gdn[l], ng)
    y_s = _mix_out_ffn(x1s, attn_s, gdn_s, wa, wb, g2, wgu2, wo2, gf, nseq)

    kv_shape = (1, nseq, WINDOW, N_KV_HEADS, HEAD_DIM)
    return (y_p.reshape(batch, seq, D_MODEL), y_s.reshape(nseq, 1, D_MODEL),
            new_k_p, new_v_p, new_conv_p, state_p[None],
            new_k_s.reshape(kv_shape), new_v_s.reshape(kv_shape),
            new_conv_s.reshape(1, nseq, CONV_WIDTH - 1, GDN_CONV_DIM), state_s[None])
```

```python
import functools

import jax
import jax.numpy as jnp
from jax import lax
from jax.experimental import pallas as pl
from jax.experimental.pallas import tpu as pltpu

F32 = jnp.float32
BF16 = jnp.bfloat16

D_MODEL = 1024
D_FF = 2816
N_Q_HEADS = 8
N_KV_HEADS = 2
GROUP = N_Q_HEADS // N_KV_HEADS
HEAD_DIM = 64
WINDOW = 128
ATTN_Q_WIDTH = N_Q_HEADS * HEAD_DIM
KV_WIDTH = N_KV_HEADS * HEAD_DIM
N_GDN_HEADS = 4
GDN_DK = 128
GDN_DV = 128
GDN_QK_WIDTH = N_GDN_HEADS * GDN_DK
GDN_V_WIDTH = N_GDN_HEADS * GDN_DV
GDN_CONV_DIM = 2 * GDN_QK_WIDTH + GDN_V_WIDTH
CONV_WIDTH = 4
CHUNK = 64
EPS = 1e-6

LANES = 128
SUBLANES = 8
VMEM_LIMIT = 56 * 1024 * 1024

P_GQKV = 0
P_AQ = P_GQKV + GDN_CONV_DIM
P_GZ = P_AQ + ATTN_Q_WIDTH
P_KV = P_GZ + GDN_V_WIDTH
P_GBA = P_KV + 2 * KV_WIDTH
P_WIDTH = P_GBA + LANES

FF_CHUNK = 256
TOKEN_TILE = 512
GDN_TILE = 256
GDN_SEQS = 4
SWA_TILE = 512
CONV_ROWS = 64
SAMPLE_BLOCK = 8


def _sigmoid(x):
    return 1.0 / (1.0 + jnp.exp(-x))


def _silu(x):
    return x * _sigmoid(x)


def _softplus(x):
    return jnp.maximum(x, 0.0) + jnp.log1p(jnp.exp(-jnp.abs(x)))


def _rms(x, g):
    return x * lax.rsqrt(jnp.mean(x * x, axis=-1, keepdims=True) + EPS) * g


def _dot(a, b):
    return jnp.dot(a.astype(BF16), b.astype(BF16), preferred_element_type=F32)


def _dot_nt(a, b):
    return lax.dot_general(a.astype(BF16), b.astype(BF16), (((1,), (1,)), ((), ())),
                           preferred_element_type=F32)


def _dot_exact(a, b):
    return jnp.dot(a, b, preferred_element_type=F32, precision=lax.Precision.HIGHEST)


def _resident(shape):
    return pl.BlockSpec(shape, lambda *_: (0,) * len(shape), pipeline_mode=pl.Buffered(1))


def _params(*semantics):
    return pltpu.CompilerParams(dimension_semantics=semantics, vmem_limit_bytes=VMEM_LIMIT)


FF_STEPS = D_FF // FF_CHUNK


def _ffn_half_step(x, g, wgu_ref, wo_ref):
    n = _rms(x, g).astype(BF16)
    acc = jnp.zeros(x.shape, F32)
    for c in range(FF_STEPS):
        lo = c * FF_CHUNK
        gate = jnp.dot(n, wgu_ref[:, lo:lo + FF_CHUNK], preferred_element_type=F32)
        up = jnp.dot(n, wgu_ref[:, D_FF + lo:D_FF + lo + FF_CHUNK], preferred_element_type=F32)
        h = (_silu(gate) * up).astype(BF16)
        acc = acc + jnp.dot(h, wo_ref[lo:lo + FF_CHUNK, :], preferred_element_type=F32)
    return x + 0.5 * acc


_W_AQ, _W_KV = 0, ATTN_Q_WIDTH
_W_GQKV = _W_KV + 2 * KV_WIDTH
_W_GZ = _W_GQKV + GDN_CONV_DIM
_W_GBA = _W_GZ + GDN_V_WIDTH
_PROJ_GROUPS = ((_W_GQKV, GDN_CONV_DIM, P_GQKV), (_W_GBA, LANES, P_GBA), (_W_AQ, ATTN_Q_WIDTH, P_AQ),
                (_W_GZ, GDN_V_WIDTH, P_GZ), (_W_KV, 2 * KV_WIDTH, P_KV))


def _gdn_gates(gba, alog, dtb):
    beta = _sigmoid(gba)
    g = -jnp.exp(alog) * _softplus(gba + dtb)
    return beta, g


def _l2norm(x):
    return x * lax.rsqrt(jnp.sum(x * x, axis=-1, keepdims=True) + EPS)


def _ffn_mix_in_kernel(x_ref, g1_ref, wgu_ref, wo_ref, gm_ref, win_ref, x1_ref, p_ref):
    x1 = _ffn_half_step(x_ref[...], g1_ref[...], wgu_ref, wo_ref)
    x1_ref[...] = x1
    n = _rms(x1, gm_ref[...]).astype(BF16)
    for src, width, dst in _PROJ_GROUPS:
        p_ref[:, dst:dst + width] = jnp.dot(n, win_ref[:, src:src + width],
                                            preferred_element_type=F32)


def _ffn_weight_specs():
    return [_resident((1, D_MODEL)), _resident((D_MODEL, 2 * D_FF)), _resident((D_FF, D_MODEL)),
            _resident((1, D_MODEL)), _resident((D_MODEL, P_WIDTH))]


def _ffn_mix_in(x, g1, wgu, wo, gm, win, tm):
    n_tok = x.shape[0]
    row = lambda w: pl.BlockSpec((tm, w), lambda i: (i, 0))
    return pl.pallas_call(
        _ffn_mix_in_kernel,
        grid=(n_tok // tm,),
        in_specs=[row(D_MODEL)] + _ffn_weight_specs(),
        out_specs=[row(D_MODEL), row(P_WIDTH)],
        out_shape=[jax.ShapeDtypeStruct((n_tok, D_MODEL), F32),
                   jax.ShapeDtypeStruct((n_tok, P_WIDTH), F32)],
        compiler_params=_params("parallel"),
        name="ffn_mix_in",
    )(x, g1, wgu, wo, gm, win)


def _mix_out_ffn_kernel(x1_ref, attn_ref, gdn_ref, wa_ref, wb_ref, g2_ref, wgu_ref, wo_ref, gf_ref,
                        y_ref):
    x2 = x1_ref[...] + _dot(attn_ref[...], wa_ref[...]) + _dot(gdn_ref[...], wb_ref[...])
    x3 = _ffn_half_step(x2, g2_ref[...], wgu_ref, wo_ref)
    y_ref[...] = _rms(x3, gf_ref[...])


def _mix_out_ffn(x1, attn, gdn, wa, wb, g2, wgu, wo, gf, tm):
    n_tok = x1.shape[0]
    row = lambda w: pl.BlockSpec((tm, w), lambda i: (i, 0))
    return pl.pallas_call(
        _mix_out_ffn_kernel,
        grid=(n_tok // tm,),
        in_specs=[row(D_MODEL), row(ATTN_Q_WIDTH), row(GDN_V_WIDTH),
                  _resident((ATTN_Q_WIDTH, D_MODEL)), _resident((GDN_V_WIDTH, D_MODEL)),
                  _resident((1, D_MODEL)), _resident((D_MODEL, 2 * D_FF)),
                  _resident((D_FF, D_MODEL)), _resident((1, D_MODEL))],
        out_specs=row(D_MODEL),
        out_shape=jax.ShapeDtypeStruct((n_tok, D_MODEL), F32),
        compiler_params=_params("parallel"),
        name="mix_out_ffn",
    )(x1, attn, gdn, wa, wb, g2, wgu, wo, gf)


def _alibi_slope(head):
    return 2.0 ** (-8.0 * (head + 1) / N_Q_HEADS)


def _swa_prompt_kernel(sink_ref, q_ref, kvc_ref, kvp_ref, o_ref, bias_ref, s_ref):
    blk = pl.program_id(1)
    nkeys = 2 * WINDOW

    @pl.when((pl.program_id(0) == 0) & (blk == 0))
    def _():
        key = lax.broadcasted_iota(jnp.int32, (nkeys, WINDOW), 0)
        qry = lax.broadcasted_iota(jnp.int32, (nkeys, WINDOW), 1)
        dist = qry - key + WINDOW
        valid = (dist >= 0) & (dist <= WINDOW)
        for h in range(N_Q_HEADS):
            g = h % GROUP
            bias_ref[h // GROUP, :, g * WINDOW:(g + 1) * WINDOW] = jnp.where(
                valid, -_alibi_slope(h) * dist.astype(F32), -jnp.inf)

    first_pen = jnp.where(blk > 0, 0.0, -jnp.inf)
    scale = HEAD_DIM ** -0.5
    nsub = q_ref.shape[0] // WINDOW
    jobs = [(j, kvh) for j in range(nsub) for kvh in range(N_KV_HEADS)]
    group_heads = lambda kvh: range(kvh * GROUP, (kvh + 1) * GROUP)
    def keys_values(j, col):
        if j == 0:
            return jnp.concatenate([kvp_ref[:, col], kvc_ref[0:WINDOW, col]], axis=0)
        return kvc_ref[(j - 1) * WINDOW:(j + 1) * WINDOW, col]

    for job, (j, kvh) in enumerate(jobs):
        rows = slice(j * WINDOW, (j + 1) * WINDOW)
        k_all = keys_values(j, slice(kvh * HEAD_DIM, (kvh + 1) * HEAD_DIM)).astype(BF16)
        v_t = keys_values(j, slice(KV_WIDTH + kvh * HEAD_DIM,
                                   KV_WIDTH + (kvh + 1) * HEAD_DIM)).T.astype(BF16)
        q_stack = jnp.concatenate(
            [(q_ref[rows, h * HEAD_DIM:(h + 1) * HEAD_DIM] * scale).astype(BF16)
             for h in group_heads(kvh)], axis=0)
        s_ref[job] = lax.dot_general(k_all, q_stack, (((1,), (1,)), ((), ())),
                                     preferred_element_type=F32)
        for g, h in enumerate(group_heads(kvh)):
            lanes = slice(g * WINDOW, (g + 1) * WINDOW)
            s = s_ref[job, :, lanes] + bias_ref[kvh, :, lanes]
            if j == 0:
                s = jnp.concatenate([s[:WINDOW] + first_pen, s[WINDOW:]], axis=0)
            sink = sink_ref[h]
            m = jnp.maximum(jnp.max(s, axis=0, keepdims=True), sink)
            p = jnp.exp(s - m)
            denom = jnp.sum(p, axis=0, keepdims=True) + jnp.exp(sink - m)
            o_t = jnp.dot(v_t, p.astype(BF16), preferred_element_type=F32) / denom
            o_ref[rows, h * HEAD_DIM:(h + 1) * HEAD_DIM] = o_t.T


def _swa_prompt(sinks, p3):
    batch, seq, _ = p3.shape
    tq = SWA_TILE
    nsub = tq // WINDOW
    q_blk = P_AQ // ATTN_Q_WIDTH
    kv_blk = P_KV // (2 * KV_WIDTH)
    return pl.pallas_call(
        _swa_prompt_kernel,
        grid=(batch, seq // tq),
        in_specs=[pl.BlockSpec(memory_space=pltpu.SMEM),
                  pl.BlockSpec((None, tq, ATTN_Q_WIDTH), lambda b, i: (b, i, q_blk)),
                  pl.BlockSpec((None, tq, 2 * KV_WIDTH), lambda b, i: (b, i, kv_blk)),
                  pl.BlockSpec((None, WINDOW, 2 * KV_WIDTH),
                               lambda b, i: (b, jnp.maximum(i * nsub - 1, 0), kv_blk))],
        out_specs=pl.BlockSpec((None, tq, ATTN_Q_WIDTH), lambda b, i: (b, i, 0)),
        out_shape=jax.ShapeDtypeStruct((batch, seq, ATTN_Q_WIDTH), F32),
        scratch_shapes=[pltpu.VMEM((N_KV_HEADS, 2 * WINDOW, GROUP * WINDOW), F32),
                        pltpu.VMEM((nsub * N_KV_HEADS, 2 * WINDOW, GROUP * WINDOW), F32)],
        compiler_params=_params("arbitrary", "arbitrary"),
        name="swa_prompt",
    )(sinks, p3, p3, p3)


INV_LEVELS = tuple(2 ** i for i in range(1, CHUNK.bit_length()))


def _gdn_prompt_kernel(raw_ref, prev_ref, gz_ref, gba_ref, convw_ref, alog_ref, dtb_ref, ng_ref,
                       o_ref, s_ref, mask_ref, act_ref):
    t = pl.program_id(1)
    tt = GDN_TILE
    nchunk = tt // CHUNK
    heads = range(N_GDN_HEADS)

    r = lax.broadcasted_iota(jnp.int32, (tt, tt), 0)
    c = lax.broadcasted_iota(jnp.int32, (tt, tt), 1)
    xor_idx = r ^ c

    @pl.when((pl.program_id(0) == 0) & (t == 0))
    def _():
        mask_ref[0] = jnp.where((xor_idx < CHUNK) & (r >= c), 1.0, 0.0).astype(BF16)
        for i, s in enumerate(INV_LEVELS):
            mask_ref[i + 1] = jnp.where((xor_idx >= s // 2) & (xor_idx < s) & (r > c),
                                        1.0, 0.0).astype(BF16)

    @pl.when(t == 0)
    def _():
        s_ref[...] = jnp.zeros(s_ref.shape, F32)

    strict = (xor_idx < CHUNK) & (r > c)
    diag = r == c

    nseq = raw_ref.shape[0]
    chains = [(sq, h) for sq in range(nseq) for h in heads]
    ids = range(len(chains))

    base = SUBLANES - (CONV_WIDTH - 1)
    for sq in range(nseq):
        for cg in range(GDN_CONV_DIM // LANES):
            cols = slice(cg * LANES, (cg + 1) * LANES)
            w = convw_ref[:, cols]
            for r0 in range(0, tt, CONV_ROWS):
                if r0 == 0:
                    before = jnp.where(t > 0, prev_ref[sq, :, cols], 0.0)
                else:
                    before = raw_ref[sq, r0 - SUBLANES:r0, cols]
                cur = raw_ref[sq, r0:r0 + CONV_ROWS, cols]
                ext = jnp.concatenate([before, cur], axis=0)
                acc = w[0:1] * ext[base:base + CONV_ROWS]
                for j in range(1, CONV_WIDTH - 1):
                    acc = acc + w[j:j + 1] * ext[base + j:base + j + CONV_ROWS]
                act = _silu(acc + w[CONV_WIDTH - 1:CONV_WIDTH] * cur)
                if cg < GDN_QK_WIDTH // LANES:
                    act = _l2norm(act) * (GDN_DK ** -0.5)
                elif cg < 2 * GDN_QK_WIDTH // LANES:
                    act = _l2norm(act)
                act_ref[sq, r0:r0 + CONV_ROWS, cols] = act

    gates, gc, gc_t, gtot = [], [], [], []
    for sq in range(nseq):
        beta_all, g_all = _gdn_gates(gba_ref[sq], alog_ref[...], dtb_ref[...])
        g_hi = g_all.astype(BF16)
        g_lo = (g_all - g_hi.astype(F32)).astype(BF16)
        cs = jnp.dot(mask_ref[0], jnp.concatenate([g_hi, g_lo], axis=1),
                     preferred_element_type=F32)
        gc_sq = cs[:, :LANES] + cs[:, LANES:]
        gates.append(beta_all)
        gc.append(gc_sq)
        gc_t.append(gc_sq.T)
        gtot.append(jnp.concatenate(
            [jnp.broadcast_to(gc_sq[(ci + 1) * CHUNK - 1:(ci + 1) * CHUNK, :], (CHUNK, LANES))
             for ci in range(nchunk)], axis=0))

    q_l, k_l, v_l, beta_l, gcol_l, gend_l, a_l, qk_l, x_l = [], [], [], [], [], [], [], [], []
    for sq, h in chains:
        q = act_ref[sq, :, h * GDN_DK:(h + 1) * GDN_DK]
        k = act_ref[sq, :, GDN_QK_WIDTH + h * GDN_DK:GDN_QK_WIDTH + (h + 1) * GDN_DK]
        beta = gates[sq][:, h:h + 1]
        gcol = gc[sq][:, N_GDN_HEADS + h:N_GDN_HEADS + h + 1]
        grow = gc_t[sq][N_GDN_HEADS + h:N_GDN_HEADS + h + 1, :]
        decay = jnp.exp(jnp.where(strict, gcol - grow, -jnp.inf))
        kb = k.astype(BF16)
        qk_kk = _dot_nt(jnp.concatenate([q.astype(BF16), kb], axis=0), kb)
        a = ((beta * qk_kk[tt:]) * decay).astype(BF16)
        qk_l.append((qk_kk[:tt] * jnp.where(diag, 1.0, decay)).astype(BF16))
        x_l.append(jnp.where(diag, 1.0, 0.0).astype(BF16) - a * mask_ref[1])
        q_l.append(q); k_l.append(k); beta_l.append(beta); gcol_l.append(gcol); a_l.append(a)
        gend_l.append(gtot[sq][:, N_GDN_HEADS + h:N_GDN_HEADS + h + 1])
        v_l.append(act_ref[sq, :, 2 * GDN_QK_WIDTH + h * GDN_DV:2 * GDN_QK_WIDTH + (h + 1) * GDN_DV])

    for i in range(1, len(INV_LEVELS)):
        for n in ids:
            x = x_l[n]
            y = jnp.dot(x, a_l[n], preferred_element_type=F32).astype(BF16)
            z = jnp.dot(y, x, preferred_element_type=F32).astype(BF16)
            x_l[n] = x - z * mask_ref[i + 1]

    u_l, w_l, qd_l, kd_l = [], [], [], []
    for n in ids:
        eg = jnp.exp(gcol_l[n])
        rhs = jnp.concatenate([v_l[n] * beta_l[n], k_l[n] * (beta_l[n] * eg)], axis=1)
        uw = jnp.dot(x_l[n], rhs.astype(BF16), preferred_element_type=F32)
        u_l.append(uw[:, :GDN_DV])
        w_l.append(uw[:, GDN_DV:].astype(BF16))
        qd_l.append((q_l[n] * eg).astype(BF16))
        kd_l.append((k_l[n] * jnp.exp(gend_l[n] - gcol_l[n])).astype(BF16))

    states = [s_ref[sq, h] for sq, h in chains]
    v_new = [[] for _ in ids]
    o_inter = [[] for _ in ids]
    for ci in range(nchunk):
        rows = slice(ci * CHUNK, (ci + 1) * CHUNK)
        for n in ids:
            sb = states[n].astype(BF16)
            ws_qs = jnp.dot(jnp.concatenate([w_l[n][rows], qd_l[n][rows]], axis=0), sb,
                            preferred_element_type=F32)
            vn = u_l[n][rows] - ws_qs[:CHUNK]
            o_inter[n].append(ws_qs[CHUNK:])
            g_last = jnp.exp(gend_l[n][ci * CHUNK:ci * CHUNK + 1, :])
            upd = lax.dot_general(kd_l[n][rows], vn.astype(BF16), (((0,), (0,)), ((), ())),
                                  preferred_element_type=F32)
            states[n] = states[n] * g_last + upd
            v_new[n].append(vn)

    for n, (sq, h) in enumerate(chains):
        s_ref[sq, h] = states[n]
        vn_all = jnp.concatenate(v_new[n], axis=0).astype(BF16)
        o = jnp.concatenate(o_inter[n], axis=0) + jnp.dot(qk_l[n], vn_all,
                                                           preferred_element_type=F32)
        gz = gz_ref[sq, :, h * GDN_DV:(h + 1) * GDN_DV]
        o_ref[sq, :, h * GDN_DV:(h + 1) * GDN_DV] = _rms(o, ng_ref[...]) * _silu(gz)


def _gdn_prompt(p3, convw, alog, dtb, ng):
    batch, seq, _ = p3.shape
    tt = GDN_TILE
    ns = GDN_SEQS if batch % GDN_SEQS == 0 else 1
    prev_per_tile = tt // SUBLANES
    small = lambda shape: pl.BlockSpec(shape, lambda b, t: (0,) * len(shape))
    return pl.pallas_call(
        _gdn_prompt_kernel,
        grid=(batch // ns, seq // tt),
        in_specs=[pl.BlockSpec((ns, tt, GDN_CONV_DIM), lambda b, t: (b, t, P_GQKV // GDN_CONV_DIM)),
                  pl.BlockSpec((ns, SUBLANES, GDN_CONV_DIM),
                               lambda b, t: (b, jnp.maximum(t * prev_per_tile - 1, 0),
                                             P_GQKV // GDN_CONV_DIM)),
                  pl.BlockSpec((ns, tt, GDN_V_WIDTH), lambda b, t: (b, t, P_GZ // GDN_V_WIDTH)),
                  pl.BlockSpec((ns, tt, LANES), lambda b, t: (b, t, P_GBA // LANES)),
                  small((CONV_WIDTH, GDN_CONV_DIM)), small((1, LANES)), small((1, LANES)),
                  small((1, GDN_DV))],
        out_specs=[pl.BlockSpec((ns, tt, GDN_V_WIDTH), lambda b, t: (b, t, 0)),
                   pl.BlockSpec((ns, N_GDN_HEADS, GDN_DK, GDN_DV), lambda b, t: (b, 0, 0, 0))],
        out_shape=[jax.ShapeDtypeStruct((batch, seq, GDN_V_WIDTH), F32),
                   jax.ShapeDtypeStruct((batch, N_GDN_HEADS, GDN_DK, GDN_DV), F32)],
        scratch_shapes=[pltpu.VMEM((len(INV_LEVELS) + 1, tt, tt), BF16),
                        pltpu.VMEM((ns, tt, GDN_CONV_DIM), F32)],
        compiler_params=_params("arbitrary", "arbitrary"),
        name="gdn_prompt",
    )(p3, p3, p3, p3, convw, alog, dtb, ng)


def _sample_prep_kernel(p_ref, hist_ref, convw_ref, alog_ref, dtb_ref,
                        conv_ref, qt_ref, kt_ref, v_ref, beta_ref, decay_ref, kvt_ref):
    nblk = qt_ref.shape[0]
    kv_t = p_ref[:, P_KV:P_KV + 2 * KV_WIDTH].T
    for i in range(nblk):
        kvt_ref[i] = kv_t[:, i * SAMPLE_BLOCK:(i + 1) * SAMPLE_BLOCK]
    raw = p_ref[:, P_GQKV:P_GQKV + GDN_CONV_DIM]
    w = convw_ref[...]
    conv = w[0:1] * hist_ref[:, 0:GDN_CONV_DIM]
    for j in range(1, CONV_WIDTH - 1):
        conv = conv + w[j:j + 1] * hist_ref[:, j * GDN_CONV_DIM:(j + 1) * GDN_CONV_DIM]
    conv = conv + w[CONV_WIDTH - 1:CONV_WIDTH] * raw
    act = _silu(conv)
    conv_ref[:, 0:(CONV_WIDTH - 2) * GDN_CONV_DIM] = hist_ref[:, GDN_CONV_DIM:]
    conv_ref[:, (CONV_WIDTH - 2) * GDN_CONV_DIM:] = raw
    v_ref[...] = act[:, 2 * GDN_QK_WIDTH:]
    beta, g = _gdn_gates(p_ref[:, P_GBA:P_GBA + LANES], alog_ref[...], dtb_ref[...])
    beta_ref[...] = beta
    decay_ref[...] = jnp.exp(g)
    for h in range(N_GDN_HEADS):
        qt = (_l2norm(act[:, h * GDN_DK:(h + 1) * GDN_DK]) * (GDN_DK ** -0.5)).T
        kt = _l2norm(act[:, GDN_QK_WIDTH + h * GDN_DK:GDN_QK_WIDTH + (h + 1) * GDN_DK]).T
        for i in range(nblk):
            qt_ref[i, h] = qt[:, i * SAMPLE_BLOCK:(i + 1) * SAMPLE_BLOCK]
            kt_ref[i, h] = kt[:, i * SAMPLE_BLOCK:(i + 1) * SAMPLE_BLOCK]


def _sample_prep(p, hist, convw, alog, dtb):
    nseq = p.shape[0]
    nblk = nseq // SAMPLE_BLOCK
    hist_w = (CONV_WIDTH - 1) * GDN_CONV_DIM
    cols = jax.ShapeDtypeStruct((nblk, N_GDN_HEADS, GDN_DK, SAMPLE_BLOCK), F32)
    return pl.pallas_call(
        _sample_prep_kernel,
        out_shape=[jax.ShapeDtypeStruct((nseq, hist_w), F32), cols, cols,
                   jax.ShapeDtypeStruct((nseq, GDN_V_WIDTH), F32),
                   jax.ShapeDtypeStruct((nseq, LANES), F32),
                   jax.ShapeDtypeStruct((nseq, LANES), F32),
                   jax.ShapeDtypeStruct((nblk, 2 * KV_WIDTH, SAMPLE_BLOCK), F32)],
        compiler_params=pltpu.CompilerParams(vmem_limit_bytes=VMEM_LIMIT),
        name="sample_prep",
    )(p, hist, convw, alog, dtb)


def _per_head(values):
    h = lax.broadcasted_iota(jnp.int32, (N_Q_HEADS, 1), 0)
    col = jnp.full((N_Q_HEADS, 1), values[N_Q_HEADS - 1], F32)
    for i in range(N_Q_HEADS - 2, -1, -1):
        col = jnp.where(h == i, values[i], col)
    return col


def _sample_mix_kernel(sink_ref, q_ref, p_ref, kc_ref, vc_ref, kvt_ref, qt_ref, kt_ref, v_ref,
                       beta_ref, decay_ref, s_ref, ng_ref, attn_ref, gdn_ref, nk_ref, nv_ref, ns_ref):
    time = lax.broadcasted_iota(jnp.int32, (KV_WIDTH, WINDOW), 1)
    newest = time == WINDOW - 1
    key_pos = lax.broadcasted_iota(jnp.int32, (1, WINDOW), 1)
    dist_hist = (WINDOW - key_pos).astype(F32)
    head = lax.broadcasted_iota(jnp.int32, (N_Q_HEADS, HEAD_DIM), 0)
    kv_of_head = [head // GROUP == kvh for kvh in range(N_KV_HEADS)]
    slope = _per_head([_alibi_slope(h) for h in range(N_Q_HEADS)])
    sink = _per_head([sink_ref[h] for h in range(N_Q_HEADS)])
    scale = HEAD_DIM ** -0.5
    for b in range(SAMPLE_BLOCK):
        k_hist = kc_ref[b]
        v_hist = vc_ref[b]
        k_new = p_ref[b:b + 1, P_KV:P_KV + KV_WIDTH]
        v_new = p_ref[b:b + 1, P_KV + KV_WIDTH:P_KV + 2 * KV_WIDTH]
        nk_ref[b] = jnp.where(newest, kvt_ref[0:KV_WIDTH, b:b + 1],
                              pltpu.roll(k_hist, WINDOW - 1, axis=1))
        nv_ref[b] = jnp.where(newest, kvt_ref[KV_WIDTH:2 * KV_WIDTH, b:b + 1],
                              pltpu.roll(v_hist, WINDOW - 1, axis=1))
        q = q_ref[b]
        q_wide = jnp.concatenate([jnp.where(sel, q, 0.0) for sel in kv_of_head], axis=1)
        s_hist = _dot(q_wide, k_hist) * scale - slope * dist_hist
        s_new = jnp.sum(q_wide * k_new, axis=-1, keepdims=True) * scale
        m = jnp.maximum(jnp.maximum(jnp.max(s_hist, axis=-1, keepdims=True), s_new), sink)
        p_hist = jnp.exp(s_hist - m)
        p_new = jnp.exp(s_new - m)
        denom = jnp.sum(p_hist, axis=-1, keepdims=True) + p_new + jnp.exp(sink - m)
        o_wide = (_dot_nt(p_hist, v_hist) + p_new * v_new) / denom
        o = o_wide[:, 0:HEAD_DIM]
        for kvh in range(1, N_KV_HEADS):
            o = jnp.where(kv_of_head[kvh], o_wide[:, kvh * HEAD_DIM:(kvh + 1) * HEAD_DIM], o)
        attn_ref[b] = o

    heads = range(N_GDN_HEADS)
    for b in range(SAMPLE_BLOCK):
        kcol = [jnp.broadcast_to(kt_ref[h, :, b:b + 1], (GDN_DK, GDN_DV)) for h in heads]
        qcol = [jnp.broadcast_to(qt_ref[h, :, b:b + 1], (GDN_DK, GDN_DV)) for h in heads]
        decayed = [s_ref[b, h] * decay_ref[b:b + 1, N_GDN_HEADS + h:N_GDN_HEADS + h + 1]
                   for h in heads]
        ks = [jnp.sum(decayed[h] * kcol[h], axis=0, keepdims=True) for h in heads]
        delta = [beta_ref[b:b + 1, h:h + 1] * (v_ref[b:b + 1, h * GDN_DV:(h + 1) * GDN_DV] - ks[h])
                 for h in heads]
        state = [decayed[h] + kcol[h] * delta[h] for h in heads]
        outs = []
        for h in heads:
            ns_ref[b, h] = state[h]
            o = jnp.sum(state[h] * qcol[h], axis=0, keepdims=True)
            gz = p_ref[b:b + 1, P_GZ + h * GDN_DV:P_GZ + (h + 1) * GDN_DV]
            outs.append(_rms(o, ng_ref[...]) * _silu(gz))
        gdn_ref[b:b + 1, :] = jnp.concatenate(outs, axis=1)


def _sample_mix(sinks, q, p, k_hist, v_hist, kv_t, qt, kt, v, beta, decay, state, ng):
    nseq = p.shape[0]
    bb = SAMPLE_BLOCK
    rows = lambda w: pl.BlockSpec((bb, w), lambda i: (i, 0))
    heads = pl.BlockSpec((bb, N_Q_HEADS, HEAD_DIM), lambda i: (i, 0, 0))
    cache = pl.BlockSpec((bb, KV_WIDTH, WINDOW), lambda i: (i, 0, 0))
    cols = pl.BlockSpec((None, N_GDN_HEADS, GDN_DK, bb), lambda i: (i, 0, 0, 0))
    kv_cols = pl.BlockSpec((None, 2 * KV_WIDTH, bb), lambda i: (i, 0, 0))
    st = pl.BlockSpec((bb, N_GDN_HEADS, GDN_DK, GDN_DV), lambda i: (i, 0, 0, 0))
    return pl.pallas_call(
        _sample_mix_kernel,
        grid=(nseq // bb,),
        in_specs=[pl.BlockSpec(memory_space=pltpu.SMEM), heads, rows(P_WIDTH), cache, cache,
                  kv_cols, cols, cols, rows(GDN_V_WIDTH), rows(LANES), rows(LANES), st,
                  pl.BlockSpec((1, GDN_DV), lambda i: (0, 0))],
        out_specs=[heads, rows(GDN_V_WIDTH), cache, cache, st],
        out_shape=[jax.ShapeDtypeStruct((nseq, N_Q_HEADS, HEAD_DIM), F32),
                   jax.ShapeDtypeStruct((nseq, GDN_V_WIDTH), F32),
                   jax.ShapeDtypeStruct(k_hist.shape, F32),
                   jax.ShapeDtypeStruct(v_hist.shape, F32),
                   jax.ShapeDtypeStruct(state.shape, F32)],
        compiler_params=_params("parallel"),
        name="sample_mix",
    )(sinks, q, p, k_hist, v_hist, kv_t, qt, kt, v, beta, decay, state, ng)


def _lane_pad(vec, offset):
    return jnp.zeros((1, LANES), F32).at[0, offset:offset + vec.shape[0]].set(vec)


def kernel(x_prompt, x_sample, cache_attn_k, cache_attn_v, state_conv, state_gdn, ffn1_norm_g,
           ffn1_w_in, ffn1_w_out, mix_norm_g, w_in_mix, attn_sinks, conv_w, gdn_A_log, gdn_dt_bias,
           gdn_norm_g, w_out_mix, ffn2_norm_g, ffn2_w_in, ffn2_w_out, final_norm_g):
    depth = ffn1_w_in.shape[0]
    assert depth == 1, "single-layer trunk"
    batch, seq, _ = x_prompt.shape
    nseq = x_sample.shape[0]
    assert x_sample.shape[1] == 1 and cache_attn_k.shape[2] == WINDOW
    assert seq % GDN_TILE == 0 and seq % WINDOW == 0 and nseq % SAMPLE_BLOCK == 0
    l = 0
    row = lambda v: v.reshape(1, -1)
    g1, gm, g2, gf = row(ffn1_norm_g[l]), row(mix_norm_g[l]), row(ffn2_norm_g[l]), row(final_norm_g)
    ng = row(gdn_norm_g[l])
    wgu1, wo1 = ffn1_w_in[l].astype(BF16), ffn1_w_out[l].astype(BF16)
    wgu2, wo2 = ffn2_w_in[l].astype(BF16), ffn2_w_out[l].astype(BF16)
    win = jnp.pad(w_in_mix[l].astype(BF16), ((0, 0), (0, P_WIDTH - w_in_mix.shape[2])))
    wa = w_out_mix[l, :ATTN_Q_WIDTH].astype(BF16)
    wb = w_out_mix[l, ATTN_Q_WIDTH:].astype(BF16)
    alog = _lane_pad(gdn_A_log[l], N_GDN_HEADS)
    dtb = _lane_pad(gdn_dt_bias[l], N_GDN_HEADS)
    sinks = attn_sinks[l]
    convw = conv_w[l]

    xp = x_prompt.reshape(batch * seq, D_MODEL)
    x1p, pp = _ffn_mix_in(xp, g1, wgu1, wo1, gm, win, TOKEN_TILE)
    pp3 = pp.reshape(batch, seq, P_WIDTH)
    attn_p = _swa_prompt(sinks, pp3)
    gdn_p, state_p = _gdn_prompt(pp3, convw, alog, dtb, ng)
    y_p = _mix_out_ffn(x1p, attn_p.reshape(batch * seq, ATTN_Q_WIDTH),
                       gdn_p.reshape(batch * seq, GDN_V_WIDTH), wa, wb, g2, wgu2, wo2, gf, TOKEN_TILE)
    tail = pp3[:, seq - WINDOW:, P_KV:P_KV + 2 * KV_WIDTH]
    new_k_p = tail[:, :, :KV_WIDTH].reshape(1, batch, WINDOW, N_KV_HEADS, HEAD_DIM)
    new_v_p = tail[:, :, KV_WIDTH:].reshape(1, batch, WINDOW, N_KV_HEADS, HEAD_DIM)
    new_conv_p = pp3[:, seq - (CONV_WIDTH - 1):, P_GQKV:P_GQKV + GDN_CONV_DIM][None]

    xs = x_sample.reshape(nseq, D_MODEL)
    x1s, ps = _ffn_mix_in(xs, g1, wgu1, wo1, gm, win, nseq)
    hist = state_conv[l].reshape(nseq, (CONV_WIDTH - 1) * GDN_CONV_DIM)
    new_conv_s, qt, kt, v_s, beta_s, decay_s, kv_t = _sample_prep(ps, hist, convw, alog, dtb)
    time_minor = lambda c: jnp.transpose(c.reshape(nseq, WINDOW, KV_WIDTH), (0, 2, 1))
    time_major = lambda c: jnp.transpose(c, (0, 2, 1)).reshape(1, nseq, WINDOW, N_KV_HEADS, HEAD_DIM)
    q_s = ps[:, P_AQ:P_AQ + ATTN_Q_WIDTH].reshape(nseq, N_Q_HEADS, HEAD_DIM)
    attn_s, gdn_s, new_k_s, new_v_s, state_s = _sample_mix(
        sinks, q_s, ps, time_minor(cache_attn_k[l]), time_minor(cache_attn_v[l]), kv_t, qt, kt,
        v_s, beta_s, decay_s, state_gdn[l], ng)
    y_s = _mix_out_ffn(x1s, attn_s.reshape(nseq, ATTN_Q_WIDTH), gdn_s, wa, wb, g2, wgu2, wo2, gf,
                       nseq)

    return (y_p.reshape(batch, seq, D_MODEL), y_s.reshape(nseq, 1, D_MODEL),
            new_k_p, new_v_p, new_conv_p, state_p[None],
            time_major(new_k_s), time_major(new_v_s),
            new_conv_s.reshape(1, nseq, CONV_WIDTH - 1, GDN_CONV_DIM), state_s[None])
```

```python
import functools

import jax
import jax.numpy as jnp
from jax import lax
from jax.experimental import pallas as pl
from jax.experimental.pallas import tpu as pltpu

F32 = jnp.float32
BF16 = jnp.bfloat16

D_MODEL = 1024
D_FF = 2816
N_Q_HEADS = 8
N_KV_HEADS = 2
GROUP = N_Q_HEADS // N_KV_HEADS
HEAD_DIM = 64
WINDOW = 128
ATTN_Q_WIDTH = N_Q_HEADS * HEAD_DIM
KV_WIDTH = N_KV_HEADS * HEAD_DIM
N_GDN_HEADS = 4
GDN_DK = 128
GDN_DV = 128
GDN_QK_WIDTH = N_GDN_HEADS * GDN_DK
GDN_V_WIDTH = N_GDN_HEADS * GDN_DV
GDN_CONV_DIM = 2 * GDN_QK_WIDTH + GDN_V_WIDTH
CONV_WIDTH = 4
CHUNK = 64
EPS = 1e-6

LANES = 128
SUBLANES = 8
VMEM_LIMIT = 56 * 1024 * 1024

P_GQKV = 0
P_AQ = P_GQKV + GDN_CONV_DIM
P_GZ = P_AQ + ATTN_Q_WIDTH
P_KV = P_GZ + GDN_V_WIDTH
P_GBA = P_KV + 2 * KV_WIDTH
P_WIDTH = P_GBA + LANES

FF_CHUNK = 256
TOKEN_TILE = 512
GDN_TILE = 256
GDN_SEQS = 4
SWA_TILE = 512
CONV_ROWS = 64
SAMPLE_BLOCK = 8


def _sigmoid(x):
    return 1.0 / (1.0 + jnp.exp(-x))


def _silu(x):
    return x * _sigmoid(x)


def _softplus(x):
    return jnp.maximum(x, 0.0) + jnp.log1p(jnp.exp(-jnp.abs(x)))


def _rms(x, g):
    return x * lax.rsqrt(jnp.mean(x * x, axis=-1, keepdims=True) + EPS) * g


def _dot(a, b):
    return jnp.dot(a.astype(BF16), b.astype(BF16), preferred_element_type=F32)


def _dot_nt(a, b):
    return lax.dot_general(a.astype(BF16), b.astype(BF16), (((1,), (1,)), ((), ())),
                           preferred_element_type=F32)


def _dot_exact(a, b):
    return jnp.dot(a, b, preferred_element_type=F32, precision=lax.Precision.HIGHEST)


def _resident(shape):
    return pl.BlockSpec(shape, lambda *_: (0,) * len(shape), pipeline_mode=pl.Buffered(1))


def _params(*semantics):
    return pltpu.CompilerParams(dimension_semantics=semantics, vmem_limit_bytes=VMEM_LIMIT)


FF_STEPS = D_FF // FF_CHUNK


def _ffn_half_step(x, g, wgu_ref, wo_ref):
    n = _rms(x, g).astype(BF16)
    acc = jnp.zeros(x.shape, F32)
    for c in range(FF_STEPS):
        lo = c * FF_CHUNK
        gate = jnp.dot(n, wgu_ref[:, lo:lo + FF_CHUNK], preferred_element_type=F32)
        up = jnp.dot(n, wgu_ref[:, D_FF + lo:D_FF + lo + FF_CHUNK], preferred_element_type=F32)
        h = (_silu(gate) * up).astype(BF16)
        acc = acc + jnp.dot(h, wo_ref[lo:lo + FF_CHUNK, :], preferred_element_type=F32)
    return x + 0.5 * acc


_W_AQ, _W_KV = 0, ATTN_Q_WIDTH
_W_GQKV = _W_KV + 2 * KV_WIDTH
_W_GZ = _W_GQKV + GDN_CONV_DIM
_W_GBA = _W_GZ + GDN_V_WIDTH
_PROJ_GROUPS = ((_W_GQKV, GDN_CONV_DIM, P_GQKV), (_W_GBA, LANES, P_GBA), (_W_AQ, ATTN_Q_WIDTH, P_AQ),
                (_W_GZ, GDN_V_WIDTH, P_GZ), (_W_KV, 2 * KV_WIDTH, P_KV))


def _gdn_gates(gba, alog, dtb):
    beta = _sigmoid(gba)
    g = -jnp.exp(alog) * _softplus(gba + dtb)
    return beta, g


def _l2norm(x):
    return x * lax.rsqrt(jnp.sum(x * x, axis=-1, keepdims=True) + EPS)


def _ffn_mix_in_kernel(x_ref, g1_ref, wgu_ref, wo_ref, gm_ref, win_ref, x1_ref, p_ref):
    x1 = _ffn_half_step(x_ref[...], g1_ref[...], wgu_ref, wo_ref)
    x1_ref[...] = x1
    n = _rms(x1, gm_ref[...]).astype(BF16)
    for src, width, dst in _PROJ_GROUPS:
        p_ref[:, dst:dst + width] = jnp.dot(n, win_ref[:, src:src + width],
                                            preferred_element_type=F32)


def _ffn_weight_specs():
    return [_resident((1, D_MODEL)), _resident((D_MODEL, 2 * D_FF)), _resident((D_FF, D_MODEL)),
            _resident((1, D_MODEL)), _resident((D_MODEL, P_WIDTH))]


def _ffn_mix_in(x, g1, wgu, wo, gm, win, tm):
    n_tok = x.shape[0]
    row = lambda w: pl.BlockSpec((tm, w), lambda i: (i, 0))
    return pl.pallas_call(
        _ffn_mix_in_kernel,
        grid=(n_tok // tm,),
        in_specs=[row(D_MODEL)] + _ffn_weight_specs(),
        out_specs=[row(D_MODEL), row(P_WIDTH)],
        out_shape=[jax.ShapeDtypeStruct((n_tok, D_MODEL), F32),
                   jax.ShapeDtypeStruct((n_tok, P_WIDTH), F32)],
        compiler_params=_params("parallel"),
        name="ffn_mix_in",
    )(x, g1, wgu, wo, gm, win)


def _mix_out_ffn_kernel(x1_ref, attn_ref, gdn_ref, wa_ref, wb_ref, g2_ref, wgu_ref, wo_ref, gf_ref,
                        y_ref):
    x2 = x1_ref[...] + _dot(attn_ref[...], wa_ref[...]) + _dot(gdn_ref[...], wb_ref[...])
    x3 = _ffn_half_step(x2, g2_ref[...], wgu_ref, wo_ref)
    y_ref[...] = _rms(x3, gf_ref[...])


def _mix_out_ffn(x1, attn, gdn, wa, wb, g2, wgu, wo, gf, tm):
    n_tok = x1.shape[0]
    row = lambda w: pl.BlockSpec((tm, w), lambda i: (i, 0))
    return pl.pallas_call(
        _mix_out_ffn_kernel,
        grid=(n_tok // tm,),
        in_specs=[row(D_MODEL), row(ATTN_Q_WIDTH), row(GDN_V_WIDTH),
                  _resident((ATTN_Q_WIDTH, D_MODEL)), _resident((GDN_V_WIDTH, D_MODEL)),
                  _resident((1, D_MODEL)), _resident((D_MODEL, 2 * D_FF)),
                  _resident((D_FF, D_MODEL)), _resident((1, D_MODEL))],
        out_specs=row(D_MODEL),
        out_shape=jax.ShapeDtypeStruct((n_tok, D_MODEL), F32),
        compiler_params=_params("parallel"),
        name="mix_out_ffn",
    )(x1, attn, gdn, wa, wb, g2, wgu, wo, gf)


def _alibi_slope(head):
    return 2.0 ** (-8.0 * (head + 1) / N_Q_HEADS)


def _swa_prompt_kernel(sink_ref, q_ref, kvc_ref, kvp_ref, o_ref, bias_ref, s_ref):
    blk = pl.program_id(1)
    nkeys = 2 * WINDOW

    @pl.when((pl.program_id(0) == 0) & (blk == 0))
    def _():
        key = lax.broadcasted_iota(jnp.int32, (nkeys, WINDOW), 0)
        qry = lax.broadcasted_iota(jnp.int32, (nkeys, WINDOW), 1)
        dist = qry - key + WINDOW
        valid = (dist >= 0) & (dist <= WINDOW)
        for h in range(N_Q_HEADS):
            g = h % GROUP
            bias_ref[h // GROUP, :, g * WINDOW:(g + 1) * WINDOW] = jnp.where(
                valid, -_alibi_slope(h) * dist.astype(F32), -jnp.inf)

    first_pen = jnp.where(blk > 0, 0.0, -jnp.inf)
    scale = HEAD_DIM ** -0.5
    nsub = q_ref.shape[0] // WINDOW
    jobs = [(j, kvh) for j in range(nsub) for kvh in range(N_KV_HEADS)]
    group_heads = lambda kvh: range(kvh * GROUP, (kvh + 1) * GROUP)
    def keys_values(j, col):
        if j == 0:
            return jnp.concatenate([kvp_ref[:, col], kvc_ref[0:WINDOW, col]], axis=0)
        return kvc_ref[(j - 1) * WINDOW:(j + 1) * WINDOW, col]

    values_t = []
    for job, (j, kvh) in enumerate(jobs):
        rows = slice(j * WINDOW, (j + 1) * WINDOW)
        k_all = keys_values(j, slice(kvh * HEAD_DIM, (kvh + 1) * HEAD_DIM)).astype(BF16)
        values_t.append(keys_values(j, slice(KV_WIDTH + kvh * HEAD_DIM,
                                             KV_WIDTH + (kvh + 1) * HEAD_DIM)).T.astype(BF16))
        q_stack = jnp.concatenate(
            [(q_ref[rows, h * HEAD_DIM:(h + 1) * HEAD_DIM] * scale).astype(BF16)
             for h in group_heads(kvh)], axis=0)
        s_ref[job] = lax.dot_general(k_all, q_stack, (((1,), (1,)), ((), ())),
                                     preferred_element_type=F32)
    for job, (j, kvh) in enumerate(jobs):
        rows = slice(j * WINDOW, (j + 1) * WINDOW)
        v_t = values_t[job]
        outs = []
        for g, h in enumerate(group_heads(kvh)):
            lanes = slice(g * WINDOW, (g + 1) * WINDOW)
            s = s_ref[job, :, lanes] + bias_ref[kvh, :, lanes]
            if j == 0:
                s = jnp.concatenate([s[:WINDOW] + first_pen, s[WINDOW:]], axis=0)
            sink = sink_ref[h]
            m = jnp.maximum(jnp.max(s, axis=0, keepdims=True), sink)
            p = jnp.exp(s - m)
            denom = jnp.sum(p, axis=0, keepdims=True) + jnp.exp(sink - m)
            outs.append(jnp.dot(v_t, p.astype(BF16), preferred_element_type=F32) / denom)
        for pair in range(GROUP // 2):
            lo = (kvh * GROUP + 2 * pair) * HEAD_DIM
            o_ref[rows, lo:lo + 2 * HEAD_DIM] = jnp.concatenate(outs[2 * pair:2 * pair + 2], axis=0).T


def _swa_prompt(sinks, p3):
    batch, seq, _ = p3.shape
    tq = SWA_TILE
    nsub = tq // WINDOW
    q_blk = P_AQ // ATTN_Q_WIDTH
    kv_blk = P_KV // (2 * KV_WIDTH)
    return pl.pallas_call(
        _swa_prompt_kernel,
        grid=(batch, seq // tq),
        in_specs=[pl.BlockSpec(memory_space=pltpu.SMEM),
                  pl.BlockSpec((None, tq, ATTN_Q_WIDTH), lambda b, i: (b, i, q_blk)),
                  pl.BlockSpec((None, tq, 2 * KV_WIDTH), lambda b, i: (b, i, kv_blk)),
                  pl.BlockSpec((None, WINDOW, 2 * KV_WIDTH),
                               lambda b, i: (b, jnp.maximum(i * nsub - 1, 0), kv_blk))],
        out_specs=pl.BlockSpec((None, tq, ATTN_Q_WIDTH), lambda b, i: (b, i, 0)),
        out_shape=jax.ShapeDtypeStruct((batch, seq, ATTN_Q_WIDTH), F32),
        scratch_shapes=[pltpu.VMEM((N_KV_HEADS, 2 * WINDOW, GROUP * WINDOW), F32),
                        pltpu.VMEM((nsub * N_KV_HEADS, 2 * WINDOW, GROUP * WINDOW), F32)],
        compiler_params=_params("arbitrary", "arbitrary"),
        name="swa_prompt",
    )(sinks, p3, p3, p3)


INV_LEVELS = tuple(2 ** i for i in range(1, CHUNK.bit_length()))


def _gdn_prompt_kernel(raw_ref, prev_ref, gz_ref, gba_ref, convw_ref, alog_ref, dtb_ref, ng_ref,
                       o_ref, s_ref, mask_ref, act_ref):
    t = pl.program_id(1)
    tt = GDN_TILE
    nchunk = tt // CHUNK
    heads = range(N_GDN_HEADS)

    r = lax.broadcasted_iota(jnp.int32, (tt, tt), 0)
    c = lax.broadcasted_iota(jnp.int32, (tt, tt), 1)
    xor_idx = r ^ c

    @pl.when((pl.program_id(0) == 0) & (t == 0))
    def _():
        mask_ref[0] = jnp.where((xor_idx < CHUNK) & (r >= c), 1.0, 0.0).astype(BF16)
        for i, s in enumerate(INV_LEVELS):
            mask_ref[i + 1] = jnp.where((xor_idx >= s // 2) & (xor_idx < s) & (r > c),
                                        1.0, 0.0).astype(BF16)

    @pl.when(t == 0)
    def _():
        s_ref[...] = jnp.zeros(s_ref.shape, F32)

    strict = (xor_idx < CHUNK) & (r > c)
    diag = r == c

    nseq = raw_ref.shape[0]
    chains = [(sq, h) for sq in range(nseq) for h in heads]
    ids = range(len(chains))

    base = SUBLANES - (CONV_WIDTH - 1)
    for sq in range(nseq):
        for cg in range(GDN_CONV_DIM // LANES):
            cols = slice(cg * LANES, (cg + 1) * LANES)
            w = convw_ref[:, cols]
            for r0 in range(0, tt, CONV_ROWS):
                if r0 == 0:
                    before = jnp.where(t > 0, prev_ref[sq, :, cols], 0.0)
                else:
                    before = raw_ref[sq, r0 - SUBLANES:r0, cols]
                cur = raw_ref[sq, r0:r0 + CONV_ROWS, cols]
                ext = jnp.concatenate([before, cur], axis=0)
                acc = w[0:1] * ext
                for j in range(1, CONV_WIDTH):
                    acc = pltpu.roll(acc, 1, axis=0) + w[j:j + 1] * ext
                act = _silu(acc[SUBLANES:])
                if cg < GDN_QK_WIDTH // LANES:
                    act = _l2norm(act) * (GDN_DK ** -0.5)
                elif cg < 2 * GDN_QK_WIDTH // LANES:
                    act = _l2norm(act)
                act_ref[sq, r0:r0 + CONV_ROWS, cols] = act

    gates, gc, gc_t, gtot = [], [], [], []
    for sq in range(nseq):
        beta_all, g_all = _gdn_gates(gba_ref[sq], alog_ref[...], dtb_ref[...])
        g_hi = g_all.astype(BF16)
        g_lo = (g_all - g_hi.astype(F32)).astype(BF16)
        cs = jnp.dot(mask_ref[0], jnp.concatenate([g_hi, g_lo], axis=1),
                     preferred_element_type=F32)
        gc_sq = cs[:, :LANES] + cs[:, LANES:]
        gates.append(beta_all)
        gc.append(gc_sq)
        gc_t.append(gc_sq.T)
        gtot.append(jnp.concatenate(
            [jnp.broadcast_to(gc_sq[(ci + 1) * CHUNK - 1:(ci + 1) * CHUNK, :], (CHUNK, LANES))
             for ci in range(nchunk)], axis=0))

    q_l, k_l, v_l, beta_l, gcol_l, gend_l, a_l, qk_l, x_l = [], [], [], [], [], [], [], [], []
    for sq, h in chains:
        q = act_ref[sq, :, h * GDN_DK:(h + 1) * GDN_DK]
        k = act_ref[sq, :, GDN_QK_WIDTH + h * GDN_DK:GDN_QK_WIDTH + (h + 1) * GDN_DK]
        beta = gates[sq][:, h:h + 1]
        gcol = gc[sq][:, N_GDN_HEADS + h:N_GDN_HEADS + h + 1]
        grow = gc_t[sq][N_GDN_HEADS + h:N_GDN_HEADS + h + 1, :]
        decay = jnp.exp(jnp.where(strict, gcol - grow, -jnp.inf))
        kb = k.astype(BF16)
        qk_kk = _dot_nt(jnp.concatenate([q.astype(BF16), kb], axis=0), kb)
        a = ((beta * qk_kk[tt:]) * decay).astype(BF16)
        qk_l.append((qk_kk[:tt] * jnp.where(diag, 1.0, decay)).astype(BF16))
        x_l.append(jnp.where(diag, 1.0, 0.0).astype(BF16) - a * mask_ref[1])
        q_l.append(q); k_l.append(k); beta_l.append(beta); gcol_l.append(gcol); a_l.append(a)
        gend_l.append(gtot[sq][:, N_GDN_HEADS + h:N_GDN_HEADS + h + 1])
        v_l.append(act_ref[sq, :, 2 * GDN_QK_WIDTH + h * GDN_DV:2 * GDN_QK_WIDTH + (h + 1) * GDN_DV])

    for i in range(1, len(INV_LEVELS)):
        for n in ids:
            x = x_l[n]
            y = jnp.dot(x, a_l[n], preferred_element_type=F32).astype(BF16)
            z = jnp.dot(y, x, preferred_element_type=F32).astype(BF16)
            x_l[n] = x - z * mask_ref[i + 1]

    u_l, w_l, qd_l, kd_l = [], [], [], []
    for n in ids:
        eg = jnp.exp(gcol_l[n])
        rhs = jnp.concatenate([v_l[n] * beta_l[n], k_l[n] * (beta_l[n] * eg)], axis=1)
        uw = jnp.dot(x_l[n], rhs.astype(BF16), preferred_element_type=F32)
        u_l.append(uw[:, :GDN_DV])
        w_l.append(uw[:, GDN_DV:].astype(BF16))
        qd_l.append((q_l[n] * eg).astype(BF16))
        kd_l.append((k_l[n] * jnp.exp(gend_l[n] - gcol_l[n])).astype(BF16))

    states = [s_ref[sq, h] for sq, h in chains]
    v_new = [[] for _ in ids]
    o_inter = [[] for _ in ids]
    for ci in range(nchunk):
        rows = slice(ci * CHUNK, (ci + 1) * CHUNK)
        for n in ids:
            sb = states[n].astype(BF16)
            ws_qs = jnp.dot(jnp.concatenate([w_l[n][rows], qd_l[n][rows]], axis=0), sb,
                            preferred_element_type=F32)
            vn = u_l[n][rows] - ws_qs[:CHUNK]
            o_inter[n].append(ws_qs[CHUNK:])
            g_last = jnp.exp(gend_l[n][ci * CHUNK:ci * CHUNK + 1, :])
            upd = lax.dot_general(kd_l[n][rows], vn.astype(BF16), (((0,), (0,)), ((), ())),
                                  preferred_element_type=F32)
            states[n] = states[n] * g_last + upd
            v_new[n].append(vn)

    for n, (sq, h) in enumerate(chains):
        s_ref[sq, h] = states[n]
        vn_all = jnp.concatenate(v_new[n], axis=0).astype(BF16)
        o = jnp.concatenate(o_inter[n], axis=0) + jnp.dot(qk_l[n], vn_all,
                                                           preferred_element_type=F32)
        gz = gz_ref[sq, :, h * GDN_DV:(h + 1) * GDN_DV]
        o_ref[sq, :, h * GDN_DV:(h + 1) * GDN_DV] = _rms(o, ng_ref[...]) * _silu(gz)


def _gdn_prompt(p3, convw, alog, dtb, ng):
    batch, seq, _ = p3.shape
    tt = GDN_TILE
    ns = GDN_SEQS if batch % GDN_SEQS == 0 else 1
    prev_per_tile = tt // SUBLANES
    small = lambda shape: pl.BlockSpec(shape, lambda b, t: (0,) * len(shape))
    return pl.pallas_call(
        _gdn_prompt_kernel,
        grid=(batch // ns, seq // tt),
        in_specs=[pl.BlockSpec((ns, tt, GDN_CONV_DIM), lambda b, t: (b, t, P_GQKV // GDN_CONV_DIM)),
                  pl.BlockSpec((ns, SUBLANES, GDN_CONV_DIM),
                               lambda b, t: (b, jnp.maximum(t * prev_per_tile - 1, 0),
                                             P_GQKV // GDN_CONV_DIM)),
                  pl.BlockSpec((ns, tt, GDN_V_WIDTH), lambda b, t: (b, t, P_GZ // GDN_V_WIDTH)),
                  pl.BlockSpec((ns, tt, LANES), lambda b, t: (b, t, P_GBA // LANES)),
                  small((CONV_WIDTH, GDN_CONV_DIM)), small((1, LANES)), small((1, LANES)),
                  small((1, GDN_DV))],
        out_specs=[pl.BlockSpec((ns, tt, GDN_V_WIDTH), lambda b, t: (b, t, 0)),
                   pl.BlockSpec((ns, N_GDN_HEADS, GDN_DK, GDN_DV), lambda b, t: (b, 0, 0, 0))],
        out_shape=[jax.ShapeDtypeStruct((batch, seq, GDN_V_WIDTH), F32),
                   jax.ShapeDtypeStruct((batch, N_GDN_HEADS, GDN_DK, GDN_DV), F32)],
        scratch_shapes=[pltpu.VMEM((len(INV_LEVELS) + 1, tt, tt), BF16),
                        pltpu.VMEM((ns, tt, GDN_CONV_DIM), F32)],
        compiler_params=_params("arbitrary", "arbitrary"),
        name="gdn_prompt",
    )(p3, p3, p3, p3, convw, alog, dtb, ng)


def _sample_prep_kernel(p_ref, hist_ref, convw_ref, alog_ref, dtb_ref,
                        conv_ref, qt_ref, kt_ref, v_ref, beta_ref, decay_ref, kvt_ref):
    nblk = qt_ref.shape[0]
    kv_t = p_ref[:, P_KV:P_KV + 2 * KV_WIDTH].T
    for i in range(nblk):
        kvt_ref[i] = kv_t[:, i * SAMPLE_BLOCK:(i + 1) * SAMPLE_BLOCK]
    raw = p_ref[:, P_GQKV:P_GQKV + GDN_CONV_DIM]
    w = convw_ref[...]
    conv = w[0:1] * hist_ref[:, 0:GDN_CONV_DIM]
    for j in range(1, CONV_WIDTH - 1):
        conv = conv + w[j:j + 1] * hist_ref[:, j * GDN_CONV_DIM:(j + 1) * GDN_CONV_DIM]
    conv = conv + w[CONV_WIDTH - 1:CONV_WIDTH] * raw
    act = _silu(conv)
    conv_ref[:, 0:(CONV_WIDTH - 2) * GDN_CONV_DIM] = hist_ref[:, GDN_CONV_DIM:]
    conv_ref[:, (CONV_WIDTH - 2) * GDN_CONV_DIM:] = raw
    v_ref[...] = act[:, 2 * GDN_QK_WIDTH:]
    beta, g = _gdn_gates(p_ref[:, P_GBA:P_GBA + LANES], alog_ref[...], dtb_ref[...])
    beta_ref[...] = beta
    decay_ref[...] = jnp.exp(g)
    for h in range(N_GDN_HEADS):
        qt = (_l2norm(act[:, h * GDN_DK:(h + 1) * GDN_DK]) * (GDN_DK ** -0.5)).T
        kt = _l2norm(act[:, GDN_QK_WIDTH + h * GDN_DK:GDN_QK_WIDTH + (h + 1) * GDN_DK]).T
        for i in range(nblk):
            qt_ref[i, h] = qt[:, i * SAMPLE_BLOCK:(i + 1) * SAMPLE_BLOCK]
            kt_ref[i, h] = kt[:, i * SAMPLE_BLOCK:(i + 1) * SAMPLE_BLOCK]


def _sample_prep(p, hist, convw, alog, dtb):
    nseq = p.shape[0]
    nblk = nseq // SAMPLE_BLOCK
    hist_w = (CONV_WIDTH - 1) * GDN_CONV_DIM
    cols = jax.ShapeDtypeStruct((nblk, N_GDN_HEADS, GDN_DK, SAMPLE_BLOCK), F32)
    return pl.pallas_call(
        _sample_prep_kernel,
        out_shape=[jax.ShapeDtypeStruct((nseq, hist_w), F32), cols, cols,
                   jax.ShapeDtypeStruct((nseq, GDN_V_WIDTH), F32),
                   jax.ShapeDtypeStruct((nseq, LANES), F32),
                   jax.ShapeDtypeStruct((nseq, LANES), F32),
                   jax.ShapeDtypeStruct((nblk, 2 * KV_WIDTH, SAMPLE_BLOCK), F32)],
        compiler_params=pltpu.CompilerParams(vmem_limit_bytes=VMEM_LIMIT),
        name="sample_prep",
    )(p, hist, convw, alog, dtb)


def _per_head(values):
    h = lax.broadcasted_iota(jnp.int32, (N_Q_HEADS, 1), 0)
    col = jnp.full((N_Q_HEADS, 1), values[N_Q_HEADS - 1], F32)
    for i in range(N_Q_HEADS - 2, -1, -1):
        col = jnp.where(h == i, values[i], col)
    return col


def _sample_mix_kernel(sink_ref, q_ref, p_ref, kc_ref, vc_ref, kvt_ref, qt_ref, kt_ref, v_ref,
                       beta_ref, decay_ref, s_ref, ng_ref, attn_ref, gdn_ref, nk_ref, nv_ref, ns_ref):
    time = lax.broadcasted_iota(jnp.int32, (KV_WIDTH, WINDOW), 1)
    newest = time == WINDOW - 1
    key_pos = lax.broadcasted_iota(jnp.int32, (1, WINDOW), 1)
    dist_hist = (WINDOW - key_pos).astype(F32)
    head = lax.broadcasted_iota(jnp.int32, (N_Q_HEADS, HEAD_DIM), 0)
    kv_of_head = [head // GROUP == kvh for kvh in range(N_KV_HEADS)]
    slope = _per_head([_alibi_slope(h) for h in range(N_Q_HEADS)])
    sink = _per_head([sink_ref[h] for h in range(N_Q_HEADS)])
    scale = HEAD_DIM ** -0.5
    for b in range(SAMPLE_BLOCK):
        k_hist = kc_ref[b]
        v_hist = vc_ref[b]
        k_new = p_ref[b:b + 1, P_KV:P_KV + KV_WIDTH]
        v_new = p_ref[b:b + 1, P_KV + KV_WIDTH:P_KV + 2 * KV_WIDTH]
        nk_ref[b] = jnp.where(newest, kvt_ref[0:KV_WIDTH, b:b + 1],
                              pltpu.roll(k_hist, WINDOW - 1, axis=1))
        nv_ref[b] = jnp.where(newest, kvt_ref[KV_WIDTH:2 * KV_WIDTH, b:b + 1],
                              pltpu.roll(v_hist, WINDOW - 1, axis=1))
        q = q_ref[b]
        q_wide = jnp.concatenate([jnp.where(sel, q, 0.0) for sel in kv_of_head], axis=1)
        s_hist = _dot(q_wide, k_hist) * scale - slope * dist_hist
        s_new = jnp.sum(q_wide * k_new, axis=-1, keepdims=True) * scale
        m = jnp.maximum(jnp.maximum(jnp.max(s_hist, axis=-1, keepdims=True), s_new), sink)
        p_hist = jnp.exp(s_hist - m)
        p_new = jnp.exp(s_new - m)
        denom = jnp.sum(p_hist, axis=-1, keepdims=True) + p_new + jnp.exp(sink - m)
        o_wide = (_dot_nt(p_hist, v_hist) + p_new * v_new) / denom
        o = o_wide[:, 0:HEAD_DIM]
        for kvh in range(1, N_KV_HEADS):
            o = jnp.where(kv_of_head[kvh], o_wide[:, kvh * HEAD_DIM:(kvh + 1) * HEAD_DIM], o)
        attn_ref[b] = o

    heads = range(N_GDN_HEADS)
    for b in range(SAMPLE_BLOCK):
        kcol = [jnp.broadcast_to(kt_ref[h, :, b:b + 1], (GDN_DK, GDN_DV)) for h in heads]
        qcol = [jnp.broadcast_to(qt_ref[h, :, b:b + 1], (GDN_DK, GDN_DV)) for h in heads]
        decayed = [s_ref[b, h] * decay_ref[b:b + 1, N_GDN_HEADS + h:N_GDN_HEADS + h + 1]
                   for h in heads]
        ks = [jnp.sum(decayed[h] * kcol[h], axis=0, keepdims=True) for h in heads]
        delta = [beta_ref[b:b + 1, h:h + 1] * (v_ref[b:b + 1, h * GDN_DV:(h + 1) * GDN_DV] - ks[h])
                 for h in heads]
        state = [decayed[h] + kcol[h] * delta[h] for h in heads]
        outs = []
        for h in heads:
            ns_ref[b, h] = state[h]
            o = jnp.sum(state[h] * qcol[h], axis=0, keepdims=True)
            gz = p_ref[b:b + 1, P_GZ + h * GDN_DV:P_GZ + (h + 1) * GDN_DV]
            outs.append(_rms(o, ng_ref[...]) * _silu(gz))
        gdn_ref[b:b + 1, :] = jnp.concatenate(outs, axis=1)


def _sample_mix(sinks, q, p, k_hist, v_hist, kv_t, qt, kt, v, beta, decay, state, ng):
    nseq = p.shape[0]
    bb = SAMPLE_BLOCK
    rows = lambda w: pl.BlockSpec((bb, w), lambda i: (i, 0))
    heads = pl.BlockSpec((bb, N_Q_HEADS, HEAD_DIM), lambda i: (i, 0, 0))
    cache = pl.BlockSpec((bb, KV_WIDTH, WINDOW), lambda i: (i, 0, 0))
    cols = pl.BlockSpec((None, N_GDN_HEADS, GDN_DK, bb), lambda i: (i, 0, 0, 0))
    kv_cols = pl.BlockSpec((None, 2 * KV_WIDTH, bb), lambda i: (i, 0, 0))
    st = pl.BlockSpec((bb, N_GDN_HEADS, GDN_DK, GDN_DV), lambda i: (i, 0, 0, 0))
    return pl.pallas_call(
        _sample_mix_kernel,
        grid=(nseq // bb,),
        in_specs=[pl.BlockSpec(memory_space=pltpu.SMEM), heads, rows(P_WIDTH), cache, cache,
                  kv_cols, cols, cols, rows(GDN_V_WIDTH), rows(LANES), rows(LANES), st,
                  pl.BlockSpec((1, GDN_DV), lambda i: (0, 0))],
        out_specs=[heads, rows(GDN_V_WIDTH), cache, cache, st],
        out_shape=[jax.ShapeDtypeStruct((nseq, N_Q_HEADS, HEAD_DIM), F32),
                   jax.ShapeDtypeStruct((nseq, GDN_V_WIDTH), F32),
                   jax.ShapeDtypeStruct(k_hist.shape, F32),
                   jax.ShapeDtypeStruct(v_hist.shape, F32),
                   jax.ShapeDtypeStruct(state.shape, F32)],
        compiler_params=_params("parallel"),
        name="sample_mix",
    )(sinks, q, p, k_hist, v_hist, kv_t, qt, kt, v, beta, decay, state, ng)


def _lane_pad(vec, offset):
    return jnp.zeros((1, LANES), F32).at[0, offset:offset + vec.shape[0]].set(vec)


def kernel(x_prompt, x_sample, cache_attn_k, cache_attn_v, state_conv, state_gdn, ffn1_norm_g,
           ffn1_w_in, ffn1_w_out, mix_norm_g, w_in_mix, attn_sinks, conv_w, gdn_A_log, gdn_dt_bias,
           gdn_norm_g, w_out_mix, ffn2_norm_g, ffn2_w_in, ffn2_w_out, final_norm_g):
    depth = ffn1_w_in.shape[0]
    assert depth == 1, "single-layer trunk"
    batch, seq, _ = x_prompt.shape
    nseq = x_sample.shape[0]
    assert x_sample.shape[1] == 1 and cache_attn_k.shape[2] == WINDOW
    assert seq % GDN_TILE == 0 and seq % WINDOW == 0 and nseq % SAMPLE_BLOCK == 0
    l = 0
    row = lambda v: v.reshape(1, -1)
    g1, gm, g2, gf = row(ffn1_norm_g[l]), row(mix_norm_g[l]), row(ffn2_norm_g[l]), row(final_norm_g)
    ng = row(gdn_norm_g[l])
    wgu1, wo1 = ffn1_w_in[l].astype(BF16), ffn1_w_out[l].astype(BF16)
    wgu2, wo2 = ffn2_w_in[l].astype(BF16), ffn2_w_out[l].astype(BF16)
    win = jnp.pad(w_in_mix[l].astype(BF16), ((0, 0), (0, P_WIDTH - w_in_mix.shape[2])))
    wa = w_out_mix[l, :ATTN_Q_WIDTH].astype(BF16)
    wb = w_out_mix[l, ATTN_Q_WIDTH:].astype(BF16)
    alog = _lane_pad(gdn_A_log[l], N_GDN_HEADS)
    dtb = _lane_pad(gdn_dt_bias[l], N_GDN_HEADS)
    sinks = attn_sinks[l]
    convw = conv_w[l]

    xp = x_prompt.reshape(batch * seq, D_MODEL)
    x1p, pp = _ffn_mix_in(xp, g1, wgu1, wo1, gm, win, TOKEN_TILE)
    pp3 = pp.reshape(batch, seq, P_WIDTH)
    attn_p = _swa_prompt(sinks, pp3)
    gdn_p, state_p = _gdn_prompt(pp3, convw, alog, dtb, ng)
    y_p = _mix_out_ffn(x1p, attn_p.reshape(batch * seq, ATTN_Q_WIDTH),
                       gdn_p.reshape(batch * seq, GDN_V_WIDTH), wa, wb, g2, wgu2, wo2, gf, TOKEN_TILE)
    tail = pp3[:, seq - WINDOW:, P_KV:P_KV + 2 * KV_WIDTH]
    new_k_p = tail[:, :, :KV_WIDTH].reshape(1, batch, WINDOW, N_KV_HEADS, HEAD_DIM)
    new_v_p = tail[:, :, KV_WIDTH:].reshape(1, batch, WINDOW, N_KV_HEADS, HEAD_DIM)
    new_conv_p = pp3[:, seq - (CONV_WIDTH - 1):, P_GQKV:P_GQKV + GDN_CONV_DIM][None]

    xs = x_sample.reshape(nseq, D_MODEL)
    x1s, ps = _ffn_mix_in(xs, g1, wgu1, wo1, gm, win, nseq)
    hist = state_conv[l].reshape(nseq, (CONV_WIDTH - 1) * GDN_CONV_DIM)
    new_conv_s, qt, kt, v_s, beta_s, decay_s, kv_t = _sample_prep(ps, hist, convw, alog, dtb)
    time_minor = lambda c: jnp.transpose(c.reshape(nseq, WINDOW, KV_WIDTH), (0, 2, 1))
    time_major = lambda c: jnp.transpose(c, (0, 2, 1)).reshape(1, nseq, WINDOW, N_KV_HEADS, HEAD_DIM)
    q_s = ps[:, P_AQ:P_AQ + ATTN_Q_WIDTH].reshape(nseq, N_Q_HEADS, HEAD_DIM)
    attn_s, gdn_s, new_k_s, new_v_s, state_s = _sample_mix(
        sinks, q_s, ps, time_minor(cache_attn_k[l]), time_minor(cache_attn_v[l]), kv_t, qt, kt,
        v_s, beta_s, decay_s, state_gdn[l], ng)
    y_s = _mix_out_ffn(x1s, attn_s.reshape(nseq, ATTN_Q_WIDTH), gdn_s, wa, wb, g2, wgu2, wo2, gf,
                       nseq)

    return (y_p.reshape(batch, seq, D_MODEL), y_s.reshape(nseq, 1, D_MODEL),
            new_k_p, new_v_p, new_conv_p, state_p[None],
            time_major(new_k_s), time_major(new_v_s),
            new_conv_s.reshape(1, nseq, CONV_WIDTH - 1, GDN_CONV_DIM), state_s[None])
```

```python
import functools

import jax
import jax.numpy as jnp
from jax import lax
from jax.experimental import pallas as pl
from jax.experimental.pallas import tpu as pltpu

F32 = jnp.float32
BF16 = jnp.bfloat16

D_MODEL = 1024
D_FF = 2816
N_Q_HEADS = 8
N_KV_HEADS = 2
GROUP = N_Q_HEADS // N_KV_HEADS
HEAD_DIM = 64
WINDOW = 128
ATTN_Q_WIDTH = N_Q_HEADS * HEAD_DIM
KV_WIDTH = N_KV_HEADS * HEAD_DIM
N_GDN_HEADS = 4
GDN_DK = 128
GDN_DV = 128
GDN_QK_WIDTH = N_GDN_HEADS * GDN_DK
GDN_V_WIDTH = N_GDN_HEADS * GDN_DV
GDN_CONV_DIM = 2 * GDN_QK_WIDTH + GDN_V_WIDTH
CONV_WIDTH = 4
CHUNK = 64
EPS = 1e-6
LOG2E = 1.4426950408889634

LANES = 128
SUBLANES = 8
VMEM_LIMIT = 56 * 1024 * 1024

P_GQKV = 0
P_AQ = P_GQKV + GDN_CONV_DIM
P_GZ = P_AQ + ATTN_Q_WIDTH
P_KV = P_GZ + GDN_V_WIDTH
P_GBA = P_KV + 2 * KV_WIDTH
P_WIDTH = P_GBA + LANES

FF_CHUNK = 256
TOKEN_TILE = 512
GDN_TILE = 256
GDN_SEQS = 4
SWA_TILE = 1024
CONV_ROWS = 64
SAMPLE_BLOCK = 8


def _sigmoid(x):
    return 1.0 / (1.0 + jnp.exp(-x))


def _silu(x):
    half = 0.5 * x
    return half + half * jnp.tanh(half)


def _softplus(x):
    return jnp.maximum(x, 0.0) + jnp.log1p(jnp.exp(-jnp.abs(x)))


def _rms(x, g):
    return x * lax.rsqrt(jnp.mean(x * x, axis=-1, keepdims=True) + EPS) * g


def _dot(a, b):
    return jnp.dot(a.astype(BF16), b.astype(BF16), preferred_element_type=F32)


def _dot_nt(a, b):
    return lax.dot_general(a.astype(BF16), b.astype(BF16), (((1,), (1,)), ((), ())),
                           preferred_element_type=F32)


def _dot_exact(a, b):
    return jnp.dot(a, b, preferred_element_type=F32, precision=lax.Precision.HIGHEST)


def _resident(shape):
    return pl.BlockSpec(shape, lambda *_: (0,) * len(shape), pipeline_mode=pl.Buffered(1))


def _params(*semantics):
    return pltpu.CompilerParams(dimension_semantics=semantics, vmem_limit_bytes=VMEM_LIMIT)


FF_STEPS = D_FF // FF_CHUNK


def _ffn_half_step(x, g, wgu_ref, wo_ref):
    n = _rms(x, g).astype(BF16)
    acc = jnp.zeros(x.shape, F32)
    for c in range(FF_STEPS):
        lo = c * FF_CHUNK
        gate = jnp.dot(n, wgu_ref[:, lo:lo + FF_CHUNK], preferred_element_type=F32)
        up = jnp.dot(n, wgu_ref[:, D_FF + lo:D_FF + lo + FF_CHUNK], preferred_element_type=F32)
        h = (_silu(gate) * up).astype(BF16)
        acc = acc + jnp.dot(h, wo_ref[lo:lo + FF_CHUNK, :], preferred_element_type=F32)
    return x + 0.5 * acc


_W_AQ, _W_KV = 0, ATTN_Q_WIDTH
_W_GQKV = _W_KV + 2 * KV_WIDTH
_W_GZ = _W_GQKV + GDN_CONV_DIM
_W_GBA = _W_GZ + GDN_V_WIDTH
_PROJ_GROUPS = ((_W_GQKV, GDN_CONV_DIM, P_GQKV), (_W_GBA, LANES, P_GBA), (_W_AQ, ATTN_Q_WIDTH, P_AQ),
                (_W_GZ, GDN_V_WIDTH, P_GZ), (_W_KV, 2 * KV_WIDTH, P_KV))


def _gdn_gates(gba, alog, dtb):
    beta = _sigmoid(gba)
    g = -jnp.exp(alog) * _softplus(gba + dtb)
    return beta, g


def _l2norm(x):
    return x * lax.rsqrt(jnp.sum(x * x, axis=-1, keepdims=True) + EPS)


def _ffn_mix_in_kernel(x_ref, g1_ref, wgu_ref, wo_ref, gm_ref, win_ref, x1_ref, p_ref):
    x1 = _ffn_half_step(x_ref[...], g1_ref[...], wgu_ref, wo_ref)
    x1_ref[...] = x1
    n = _rms(x1, gm_ref[...]).astype(BF16)
    for src, width, dst in _PROJ_GROUPS:
        p_ref[:, dst:dst + width] = jnp.dot(n, win_ref[:, src:src + width],
                                            preferred_element_type=F32)


def _ffn_weight_specs():
    return [_resident((1, D_MODEL)), _resident((D_MODEL, 2 * D_FF)), _resident((D_FF, D_MODEL)),
            _resident((1, D_MODEL)), _resident((D_MODEL, P_WIDTH))]


def _ffn_mix_in(x, g1, wgu, wo, gm, win, tm):
    n_tok = x.shape[0]
    row = lambda w: pl.BlockSpec((tm, w), lambda i: (i, 0))
    return pl.pallas_call(
        _ffn_mix_in_kernel,
        grid=(n_tok // tm,),
        in_specs=[row(D_MODEL)] + _ffn_weight_specs(),
        out_specs=[row(D_MODEL), row(P_WIDTH)],
        out_shape=[jax.ShapeDtypeStruct((n_tok, D_MODEL), F32),
                   jax.ShapeDtypeStruct((n_tok, P_WIDTH), F32)],
        compiler_params=_params("parallel"),
        name="ffn_mix_in",
    )(x, g1, wgu, wo, gm, win)


def _mix_out_ffn_kernel(x1_ref, attn_ref, gdn_ref, wa_ref, wb_ref, g2_ref, wgu_ref, wo_ref, gf_ref,
                        y_ref):
    x2 = x1_ref[...] + _dot(attn_ref[...], wa_ref[...]) + _dot(gdn_ref[...], wb_ref[...])
    x3 = _ffn_half_step(x2, g2_ref[...], wgu_ref, wo_ref)
    y_ref[...] = _rms(x3, gf_ref[...])


def _mix_out_ffn(x1, attn, gdn, wa, wb, g2, wgu, wo, gf, tm):
    n_tok = x1.shape[0]
    row = lambda w: pl.BlockSpec((tm, w), lambda i: (i, 0))
    return pl.pallas_call(
        _mix_out_ffn_kernel,
        grid=(n_tok // tm,),
        in_specs=[row(D_MODEL), row(ATTN_Q_WIDTH), row(GDN_V_WIDTH),
                  _resident((ATTN_Q_WIDTH, D_MODEL)), _resident((GDN_V_WIDTH, D_MODEL)),
                  _resident((1, D_MODEL)), _resident((D_MODEL, 2 * D_FF)),
                  _resident((D_FF, D_MODEL)), _resident((1, D_MODEL))],
        out_specs=row(D_MODEL),
        out_shape=jax.ShapeDtypeStruct((n_tok, D_MODEL), F32),
        compiler_params=_params("parallel"),
        name="mix_out_ffn",
    )(x1, attn, gdn, wa, wb, g2, wgu, wo, gf)


def _alibi_slope(head):
    return 2.0 ** (-8.0 * (head + 1) / N_Q_HEADS)


def _swa_prompt_kernel(sink_ref, q_ref, kvc_ref, kvp_ref, o_ref, bias_ref, s_ref):
    blk = pl.program_id(1)
    nkeys = 2 * WINDOW

    @pl.when((pl.program_id(0) == 0) & (blk == 0))
    def _():
        key = lax.broadcasted_iota(jnp.int32, (nkeys, WINDOW), 0)
        qry = lax.broadcasted_iota(jnp.int32, (nkeys, WINDOW), 1)
        dist = qry - key + WINDOW
        valid = (dist >= 0) & (dist <= WINDOW)
        for h in range(N_Q_HEADS):
            g = h % GROUP
            bias_ref[h // GROUP, :, g * WINDOW:(g + 1) * WINDOW] = jnp.where(
                valid, -(LOG2E * _alibi_slope(h)) * dist.astype(F32), -jnp.inf)

    first_pen = jnp.where(blk > 0, 0.0, -jnp.inf)
    scale = LOG2E * HEAD_DIM ** -0.5
    nsub = q_ref.shape[0] // WINDOW
    jobs = [(j, kvh) for j in range(nsub) for kvh in range(N_KV_HEADS)]
    group_heads = lambda kvh: range(kvh * GROUP, (kvh + 1) * GROUP)
    def keys_values(j, col):
        if j == 0:
            return jnp.concatenate([kvp_ref[:, col], kvc_ref[0:WINDOW, col]], axis=0)
        return kvc_ref[(j - 1) * WINDOW:(j + 1) * WINDOW, col]

    values_t = []
    for job, (j, kvh) in enumerate(jobs):
        rows = slice(j * WINDOW, (j + 1) * WINDOW)
        k_all = keys_values(j, slice(kvh * HEAD_DIM, (kvh + 1) * HEAD_DIM)).astype(BF16)
        values_t.append(keys_values(j, slice(KV_WIDTH + kvh * HEAD_DIM,
                                             KV_WIDTH + (kvh + 1) * HEAD_DIM)).T.astype(BF16))
        q_stack = jnp.concatenate(
            [(q_ref[rows, h * HEAD_DIM:(h + 1) * HEAD_DIM] * scale).astype(BF16)
             for h in group_heads(kvh)], axis=0)
        s_ref[job] = lax.dot_general(k_all, q_stack, (((1,), (1,)), ((), ())),
                                     preferred_element_type=F32)
    for job, (j, kvh) in enumerate(jobs):
        rows = slice(j * WINDOW, (j + 1) * WINDOW)
        v_t = values_t[job]
        outs = []
        for g, h in enumerate(group_heads(kvh)):
            lanes = slice(g * WINDOW, (g + 1) * WINDOW)
            s = s_ref[job, :, lanes] + bias_ref[kvh, :, lanes]
            if j == 0:
                s = jnp.concatenate([s[:WINDOW] + first_pen, s[WINDOW:]], axis=0)
            sink = LOG2E * sink_ref[h]
            m = jnp.maximum(jnp.max(s, axis=0, keepdims=True), sink)
            p = jnp.exp2(s - m)
            denom = jnp.sum(p, axis=0, keepdims=True) + jnp.exp2(sink - m)
            outs.append(jnp.dot(v_t, p.astype(BF16), preferred_element_type=F32) / denom)
        for pair in range(GROUP // 2):
            lo = (kvh * GROUP + 2 * pair) * HEAD_DIM
            o_ref[rows, lo:lo + 2 * HEAD_DIM] = jnp.concatenate(outs[2 * pair:2 * pair + 2], axis=0).T


def _swa_prompt(sinks, p3):
    batch, seq, _ = p3.shape
    tq = SWA_TILE
    nsub = tq // WINDOW
    q_blk = P_AQ // ATTN_Q_WIDTH
    kv_blk = P_KV // (2 * KV_WIDTH)
    return pl.pallas_call(
        _swa_prompt_kernel,
        grid=(batch, seq // tq),
        in_specs=[pl.BlockSpec(memory_space=pltpu.SMEM),
                  pl.BlockSpec((None, tq, ATTN_Q_WIDTH), lambda b, i: (b, i, q_blk)),
                  pl.BlockSpec((None, tq, 2 * KV_WIDTH), lambda b, i: (b, i, kv_blk)),
                  pl.BlockSpec((None, WINDOW, 2 * KV_WIDTH),
                               lambda b, i: (b, jnp.maximum(i * nsub - 1, 0), kv_blk))],
        out_specs=pl.BlockSpec((None, tq, ATTN_Q_WIDTH), lambda b, i: (b, i, 0)),
        out_shape=jax.ShapeDtypeStruct((batch, seq, ATTN_Q_WIDTH), F32),
        scratch_shapes=[pltpu.VMEM((N_KV_HEADS, 2 * WINDOW, GROUP * WINDOW), F32),
                        pltpu.VMEM((nsub * N_KV_HEADS, 2 * WINDOW, GROUP * WINDOW), F32)],
        compiler_params=_params("arbitrary", "arbitrary"),
        name="swa_prompt",
    )(sinks, p3, p3, p3)


INV_LEVELS = tuple(2 ** i for i in range(1, CHUNK.bit_length()))


def _gdn_prompt_kernel(raw_ref, prev_ref, gz_ref, gba_ref, convw_ref, alog_ref, dtb_ref, ng_ref,
                       o_ref, s_ref, mask_ref, act_ref):
    t = pl.program_id(1)
    tt = GDN_TILE
    nchunk = tt // CHUNK
    heads = range(N_GDN_HEADS)

    r = lax.broadcasted_iota(jnp.int32, (tt, tt), 0)
    c = lax.broadcasted_iota(jnp.int32, (tt, tt), 1)
    xor_idx = r ^ c

    @pl.when((pl.program_id(0) == 0) & (t == 0))
    def _():
        mask_ref[0] = jnp.where((xor_idx < CHUNK) & (r >= c), 1.0, 0.0).astype(BF16)
        for i, s in enumerate(INV_LEVELS):
            mask_ref[i + 1] = jnp.where((xor_idx >= s // 2) & (xor_idx < s) & (r > c),
                                        1.0, 0.0).astype(BF16)

    @pl.when(t == 0)
    def _():
        s_ref[...] = jnp.zeros(s_ref.shape, F32)

    strict = (xor_idx < CHUNK) & (r > c)
    diag = r == c

    nseq = raw_ref.shape[0]
    chains = [(sq, h) for sq in range(nseq) for h in heads]
    ids = range(len(chains))

    base = SUBLANES - (CONV_WIDTH - 1)
    for sq in range(nseq):
        for cg in range(GDN_CONV_DIM // LANES):
            cols = slice(cg * LANES, (cg + 1) * LANES)
            w = convw_ref[:, cols]
            for r0 in range(0, tt, CONV_ROWS):
                if r0 == 0:
                    before = jnp.where(t > 0, prev_ref[sq, :, cols], 0.0)
                else:
                    before = raw_ref[sq, r0 - SUBLANES:r0, cols]
                cur = raw_ref[sq, r0:r0 + CONV_ROWS, cols]
                ext = jnp.concatenate([before, cur], axis=0)
                acc = w[0:1] * ext
                for j in range(1, CONV_WIDTH):
                    acc = pltpu.roll(acc, 1, axis=0) + w[j:j + 1] * ext
                act = _silu(acc[SUBLANES:])
                if cg < GDN_QK_WIDTH // LANES:
                    act = _l2norm(act) * (GDN_DK ** -0.5)
                elif cg < 2 * GDN_QK_WIDTH // LANES:
                    act = _l2norm(act)
                act_ref[sq, r0:r0 + CONV_ROWS, cols] = act

    gates, gc, gc_t, gtot = [], [], [], []
    for sq in range(nseq):
        beta_all, g_all = _gdn_gates(gba_ref[sq], alog_ref[...], dtb_ref[...])
        g_hi = g_all.astype(BF16)
        g_lo = (g_all - g_hi.astype(F32)).astype(BF16)
        cs = jnp.dot(mask_ref[0], jnp.concatenate([g_hi, g_lo], axis=1),
                     preferred_element_type=F32)
        gc_sq = cs[:, :LANES] + cs[:, LANES:]
        gates.append(beta_all)
        gc.append(gc_sq)
        gc_t.append(gc_sq.T)
        gtot.append(jnp.concatenate(
            [jnp.broadcast_to(gc_sq[(ci + 1) * CHUNK - 1:(ci + 1) * CHUNK, :], (CHUNK, LANES))
             for ci in range(nchunk)], axis=0))

    q_l, k_l, v_l, beta_l, gcol_l, gend_l, a_l, qk_l, x_l = [], [], [], [], [], [], [], [], []
    for sq, h in chains:
        q = act_ref[sq, :, h * GDN_DK:(h + 1) * GDN_DK]
        k = act_ref[sq, :, GDN_QK_WIDTH + h * GDN_DK:GDN_QK_WIDTH + (h + 1) * GDN_DK]
        beta = gates[sq][:, h:h + 1]
        gcol = gc[sq][:, N_GDN_HEADS + h:N_GDN_HEADS + h + 1]
        grow = gc_t[sq][N_GDN_HEADS + h:N_GDN_HEADS + h + 1, :]
        decay = jnp.exp(jnp.where(strict, gcol - grow, -jnp.inf))
        kb = k.astype(BF16)
        qk_kk = _dot_nt(jnp.concatenate([q.astype(BF16), kb], axis=0), kb)
        a = ((beta * qk_kk[tt:]) * decay).astype(BF16)
        qk_l.append((qk_kk[:tt] * jnp.where(diag, 1.0, decay)).astype(BF16))
        x_l.append(jnp.where(diag, 1.0, 0.0).astype(BF16) - a * mask_ref[1])
        q_l.append(q); k_l.append(k); beta_l.append(beta); gcol_l.append(gcol); a_l.append(a)
        gend_l.append(gtot[sq][:, N_GDN_HEADS + h:N_GDN_HEADS + h + 1])
        v_l.append(act_ref[sq, :, 2 * GDN_QK_WIDTH + h * GDN_DV:2 * GDN_QK_WIDTH + (h + 1) * GDN_DV])

    for i in range(1, len(INV_LEVELS)):
        for n in ids:
            x = x_l[n]
            y = jnp.dot(x, a_l[n], preferred_element_type=F32).astype(BF16)
            z = jnp.dot(y, x, preferred_element_type=F32).astype(BF16)
            x_l[n] = x - z * mask_ref[i + 1]

    u_l, w_l, qd_l, kd_l = [], [], [], []
    for n in ids:
        eg = jnp.exp(gcol_l[n])
        rhs = jnp.concatenate([v_l[n] * beta_l[n], k_l[n] * (beta_l[n] * eg)], axis=1)
        uw = jnp.dot(x_l[n], rhs.astype(BF16), preferred_element_type=F32)
        u_l.append(uw[:, :GDN_DV])
        w_l.append(uw[:, GDN_DV:].astype(BF16))
        qd_l.append((q_l[n] * eg).astype(BF16))
        kd_l.append((k_l[n] * jnp.exp(gend_l[n] - gcol_l[n])).astype(BF16))

    states = [s_ref[sq, h] for sq, h in chains]
    v_new = [[] for _ in ids]
    o_inter = [[] for _ in ids]
    for ci in range(nchunk):
        rows = slice(ci * CHUNK, (ci + 1) * CHUNK)
        for n in ids:
            sb = states[n].astype(BF16)
            ws_qs = jnp.dot(jnp.concatenate([w_l[n][rows], qd_l[n][rows]], axis=0), sb,
                            preferred_element_type=F32)
            vn = u_l[n][rows] - ws_qs[:CHUNK]
            o_inter[n].append(ws_qs[CHUNK:])
            g_last = jnp.exp(gend_l[n][ci * CHUNK:ci * CHUNK + 1, :])
            upd = lax.dot_general(kd_l[n][rows], vn.astype(BF16), (((0,), (0,)), ((), ())),
                                  preferred_element_type=F32)
            states[n] = states[n] * g_last + upd
            v_new[n].append(vn)

    for n, (sq, h) in enumerate(chains):
        s_ref[sq, h] = states[n]
        vn_all = jnp.concatenate(v_new[n], axis=0).astype(BF16)
        o = jnp.concatenate(o_inter[n], axis=0) + jnp.dot(qk_l[n], vn_all,
                                                           preferred_element_type=F32)
        gz = gz_ref[sq, :, h * GDN_DV:(h + 1) * GDN_DV]
        o_ref[sq, :, h * GDN_DV:(h + 1) * GDN_DV] = _rms(o, ng_ref[...]) * _silu(gz)


def _gdn_prompt(p3, convw, alog, dtb, ng):
    batch, seq, _ = p3.shape
    tt = GDN_TILE
    ns = GDN_SEQS if batch % GDN_SEQS == 0 else 1
    prev_per_tile = tt // SUBLANES
    small = lambda shape: pl.BlockSpec(shape, lambda b, t: (0,) * len(shape))
    return pl.pallas_call(
        _gdn_prompt_kernel,
        grid=(batch // ns, seq // tt),
        in_specs=[pl.BlockSpec((ns, tt, GDN_CONV_DIM), lambda b, t: (b, t, P_GQKV // GDN_CONV_DIM)),
                  pl.BlockSpec((ns, SUBLANES, GDN_CONV_DIM),
                               lambda b, t: (b, jnp.maximum(t * prev_per_tile - 1, 0),
                                             P_GQKV // GDN_CONV_DIM)),
                  pl.BlockSpec((ns, tt, GDN_V_WIDTH), lambda b, t: (b, t, P_GZ // GDN_V_WIDTH)),
                  pl.BlockSpec((ns, tt, LANES), lambda b, t: (b, t, P_GBA // LANES)),
                  small((CONV_WIDTH, GDN_CONV_DIM)), small((1, LANES)), small((1, LANES)),
                  small((1, GDN_DV))],
        out_specs=[pl.BlockSpec((ns, tt, GDN_V_WIDTH), lambda b, t: (b, t, 0)),
                   pl.BlockSpec((ns, N_GDN_HEADS, GDN_DK, GDN_DV), lambda b, t: (b, 0, 0, 0))],
        out_shape=[jax.ShapeDtypeStruct((batch, seq, GDN_V_WIDTH), F32),
                   jax.ShapeDtypeStruct((batch, N_GDN_HEADS, GDN_DK, GDN_DV), F32)],
        scratch_shapes=[pltpu.VMEM((len(INV_LEVELS) + 1, tt, tt), BF16),
                        pltpu.VMEM((ns, tt, GDN_CONV_DIM), F32)],
        compiler_params=_params("arbitrary", "arbitrary"),
        name="gdn_prompt",
    )(p3, p3, p3, p3, convw, alog, dtb, ng)


def _sample_prep_kernel(p_ref, hist_ref, convw_ref, alog_ref, dtb_ref,
                        conv_ref, qt_ref, kt_ref, v_ref, beta_ref, decay_ref, kvt_ref):
    nblk = qt_ref.shape[0]
    kv_t = p_ref[:, P_KV:P_KV + 2 * KV_WIDTH].T
    for i in range(nblk):
        kvt_ref[i] = kv_t[:, i * SAMPLE_BLOCK:(i + 1) * SAMPLE_BLOCK]
    raw = p_ref[:, P_GQKV:P_GQKV + GDN_CONV_DIM]
    w = convw_ref[...]
    conv = w[0:1] * hist_ref[:, 0:GDN_CONV_DIM]
    for j in range(1, CONV_WIDTH - 1):
        conv = conv + w[j:j + 1] * hist_ref[:, j * GDN_CONV_DIM:(j + 1) * GDN_CONV_DIM]
    conv = conv + w[CONV_WIDTH - 1:CONV_WIDTH] * raw
    act = _silu(conv)
    conv_ref[:, 0:(CONV_WIDTH - 2) * GDN_CONV_DIM] = hist_ref[:, GDN_CONV_DIM:]
    conv_ref[:, (CONV_WIDTH - 2) * GDN_CONV_DIM:] = raw
    v_ref[...] = act[:, 2 * GDN_QK_WIDTH:]
    beta, g = _gdn_gates(p_ref[:, P_GBA:P_GBA + LANES], alog_ref[...], dtb_ref[...])
    beta_ref[...] = beta
    decay_ref[...] = jnp.exp(g)
    for h in range(N_GDN_HEADS):
        qt = (_l2norm(act[:, h * GDN_DK:(h + 1) * GDN_DK]) * (GDN_DK ** -0.5)).T
        kt = _l2norm(act[:, GDN_QK_WIDTH + h * GDN_DK:GDN_QK_WIDTH + (h + 1) * GDN_DK]).T
        for i in range(nblk):
            qt_ref[i, h] = qt[:, i * SAMPLE_BLOCK:(i + 1) * SAMPLE_BLOCK]
            kt_ref[i, h] = kt[:, i * SAMPLE_BLOCK:(i + 1) * SAMPLE_BLOCK]


def _sample_prep(p, hist, convw, alog, dtb):
    nseq = p.shape[0]
    nblk = nseq // SAMPLE_BLOCK
    hist_w = (CONV_WIDTH - 1) * GDN_CONV_DIM
    cols = jax.ShapeDtypeStruct((nblk, N_GDN_HEADS, GDN_DK, SAMPLE_BLOCK), F32)
    return pl.pallas_call(
        _sample_prep_kernel,
        out_shape=[jax.ShapeDtypeStruct((nseq, hist_w), F32), cols, cols,
                   jax.ShapeDtypeStruct((nseq, GDN_V_WIDTH), F32),
                   jax.ShapeDtypeStruct((nseq, LANES), F32),
                   jax.ShapeDtypeStruct((nseq, LANES), F32),
                   jax.ShapeDtypeStruct((nblk, 2 * KV_WIDTH, SAMPLE_BLOCK), F32)],
        compiler_params=pltpu.CompilerParams(vmem_limit_bytes=VMEM_LIMIT),
        name="sample_prep",
    )(p, hist, convw, alog, dtb)


def _per_head(values):
    h = lax.broadcasted_iota(jnp.int32, (N_Q_HEADS, 1), 0)
    col = jnp.full((N_Q_HEADS, 1), values[N_Q_HEADS - 1], F32)
    for i in range(N_Q_HEADS - 2, -1, -1):
        col = jnp.where(h == i, values[i], col)
    return col


def _sample_mix_kernel(sink_ref, q_ref, p_ref, kc_ref, vc_ref, kvt_ref, qt_ref, kt_ref, v_ref,
                       beta_ref, decay_ref, s_ref, ng_ref, attn_ref, gdn_ref, nk_ref, nv_ref, ns_ref):
    time = lax.broadcasted_iota(jnp.int32, (KV_WIDTH, WINDOW), 1)
    newest = time == WINDOW - 1
    key_pos = lax.broadcasted_iota(jnp.int32, (1, WINDOW), 1)
    dist_hist = (WINDOW - key_pos).astype(F32)
    head = lax.broadcasted_iota(jnp.int32, (N_Q_HEADS, HEAD_DIM), 0)
    kv_of_head = [head // GROUP == kvh for kvh in range(N_KV_HEADS)]
    slope = _per_head([_alibi_slope(h) for h in range(N_Q_HEADS)])
    sink = _per_head([sink_ref[h] for h in range(N_Q_HEADS)])
    scale = HEAD_DIM ** -0.5
    for b in range(SAMPLE_BLOCK):
        k_hist = kc_ref[b]
        v_hist = vc_ref[b]
        k_new = p_ref[b:b + 1, P_KV:P_KV + KV_WIDTH]
        v_new = p_ref[b:b + 1, P_KV + KV_WIDTH:P_KV + 2 * KV_WIDTH]
        nk_ref[b] = jnp.where(newest, kvt_ref[0:KV_WIDTH, b:b + 1],
                              pltpu.roll(k_hist, WINDOW - 1, axis=1))
        nv_ref[b] = jnp.where(newest, kvt_ref[KV_WIDTH:2 * KV_WIDTH, b:b + 1],
                              pltpu.roll(v_hist, WINDOW - 1, axis=1))
        q = q_ref[b]
        q_wide = jnp.concatenate([jnp.where(sel, q, 0.0) for sel in kv_of_head], axis=1)
        s_hist = _dot(q_wide, k_hist) * scale - slope * dist_hist
        s_new = jnp.sum(q_wide * k_new, axis=-1, keepdims=True) * scale
        m = jnp.maximum(jnp.maximum(jnp.max(s_hist, axis=-1, keepdims=True), s_new), sink)
        p_hist = jnp.exp(s_hist - m)
        p_new = jnp.exp(s_new - m)
        denom = jnp.sum(p_hist, axis=-1, keepdims=True) + p_new + jnp.exp(sink - m)
        o_wide = (_dot_nt(p_hist, v_hist) + p_new * v_new) / denom
        o = o_wide[:, 0:HEAD_DIM]
        for kvh in range(1, N_KV_HEADS):
            o = jnp.where(kv_of_head[kvh], o_wide[:, kvh * HEAD_DIM:(kvh + 1) * HEAD_DIM], o)
        attn_ref[b] = o

    heads = range(N_GDN_HEADS)
    for b in range(SAMPLE_BLOCK):
        kcol = [jnp.broadcast_to(kt_ref[h, :, b:b + 1], (GDN_DK, GDN_DV)) for h in heads]
        qcol = [jnp.broadcast_to(qt_ref[h, :, b:b + 1], (GDN_DK, GDN_DV)) for h in heads]
        decayed = [s_ref[b, h] * decay_ref[b:b + 1, N_GDN_HEADS + h:N_GDN_HEADS + h + 1]
                   for h in heads]
        ks = [jnp.sum(decayed[h] * kcol[h], axis=0, keepdims=True) for h in heads]
        delta = [beta_ref[b:b + 1, h:h + 1] * (v_ref[b:b + 1, h * GDN_DV:(h + 1) * GDN_DV] - ks[h])
                 for h in heads]
        state = [decayed[h] + kcol[h] * delta[h] for h in heads]
        outs = []
        for h in heads:
            ns_ref[b, h] = state[h]
            o = jnp.sum(state[h] * qcol[h], axis=0, keepdims=True)
            gz = p_ref[b:b + 1, P_GZ + h * GDN_DV:P_GZ + (h + 1) * GDN_DV]
            outs.append(_rms(o, ng_ref[...]) * _silu(gz))
        gdn_ref[b:b + 1, :] = jnp.concatenate(outs, axis=1)


def _sample_mix(sinks, q, p, k_hist, v_hist, kv_t, qt, kt, v, beta, decay, state, ng):
    nseq = p.shape[0]
    bb = SAMPLE_BLOCK
    rows = lambda w: pl.BlockSpec((bb, w), lambda i: (i, 0))
    heads = pl.BlockSpec((bb, N_Q_HEADS, HEAD_DIM), lambda i: (i, 0, 0))
    cache = pl.BlockSpec((bb, KV_WIDTH, WINDOW), lambda i: (i, 0, 0))
    cols = pl.BlockSpec((None, N_GDN_HEADS, GDN_DK, bb), lambda i: (i, 0, 0, 0))
    kv_cols = pl.BlockSpec((None, 2 * KV_WIDTH, bb), lambda i: (i, 0, 0))
    st = pl.BlockSpec((bb, N_GDN_HEADS, GDN_DK, GDN_DV), lambda i: (i, 0, 0, 0))
    return pl.pallas_call(
        _sample_mix_kernel,
        grid=(nseq // bb,),
        in_specs=[pl.BlockSpec(memory_space=pltpu.SMEM), heads, rows(P_WIDTH), cache, cache,
                  kv_cols, cols, cols, rows(GDN_V_WIDTH), rows(LANES), rows(LANES), st,
                  pl.BlockSpec((1, GDN_DV), lambda i: (0, 0))],
        out_specs=[heads, rows(GDN_V_WIDTH), cache, cache, st],
        out_shape=[jax.ShapeDtypeStruct((nseq, N_Q_HEADS, HEAD_DIM), F32),
                   jax.ShapeDtypeStruct((nseq, GDN_V_WIDTH), F32),
                   jax.ShapeDtypeStruct(k_hist.shape, F32),
                   jax.ShapeDtypeStruct(v_hist.shape, F32),
                   jax.ShapeDtypeStruct(state.shape, F32)],
        compiler_params=_params("parallel"),
        name="sample_mix",
    )(sinks, q, p, k_hist, v_hist, kv_t, qt, kt, v, beta, decay, state, ng)


def _lane_pad(vec, offset):
    return jnp.zeros((1, LANES), F32).at[0, offset:offset + vec.shape[0]].set(vec)


def kernel(x_prompt, x_sample, cache_attn_k, cache_attn_v, state_conv, state_gdn, ffn1_norm_g,
           ffn1_w_in, ffn1_w_out, mix_norm_g, w_in_mix, attn_sinks, conv_w, gdn_A_log, gdn_dt_bias,
           gdn_norm_g, w_out_mix, ffn2_norm_g, ffn2_w_in, ffn2_w_out, final_norm_g):
    depth = ffn1_w_in.shape[0]
    assert depth == 1, "single-layer trunk"
    batch, seq, _ = x_prompt.shape
    nseq = x_sample.shape[0]
    assert x_sample.shape[1] == 1 and cache_attn_k.shape[2] == WINDOW
    assert seq % GDN_TILE == 0 and seq % WINDOW == 0 and nseq % SAMPLE_BLOCK == 0
    l = 0
    row = lambda v: v.reshape(1, -1)
    g1, gm, g2, gf = row(ffn1_norm_g[l]), row(mix_norm_g[l]), row(ffn2_norm_g[l]), row(final_norm_g)
    ng = row(gdn_norm_g[l])
    wgu1, wo1 = ffn1_w_in[l].astype(BF16), ffn1_w_out[l].astype(BF16)
    wgu2, wo2 = ffn2_w_in[l].astype(BF16), ffn2_w_out[l].astype(BF16)
    win = jnp.pad(w_in_mix[l].astype(BF16), ((0, 0), (0, P_WIDTH - w_in_mix.shape[2])))
    wa = w_out_mix[l, :ATTN_Q_WIDTH].astype(BF16)
    wb = w_out_mix[l, ATTN_Q_WIDTH:].astype(BF16)
    alog = _lane_pad(gdn_A_log[l], N_GDN_HEADS)
    dtb = _lane_pad(gdn_dt_bias[l], N_GDN_HEADS)
    sinks = attn_sinks[l]
    convw = conv_w[l]

    xp = x_prompt.reshape(batch * seq, D_MODEL)
    x1p, pp = _ffn_mix_in(xp, g1, wgu1, wo1, gm, win, TOKEN_TILE)
    pp3 = pp.reshape(batch, seq, P_WIDTH)
    attn_p = _swa_prompt(sinks, pp3)
    gdn_p, state_p = _gdn_prompt(pp3, convw, alog, dtb, ng)
    y_p = _mix_out_ffn(x1p, attn_p.reshape(batch * seq, ATTN_Q_WIDTH),
                       gdn_p.reshape(batch * seq, GDN_V_WIDTH), wa, wb, g2, wgu2, wo2, gf, TOKEN_TILE)
    tail = pp3[:, seq - WINDOW:, P_KV:P_KV + 2 * KV_WIDTH]
    new_k_p = tail[:, :, :KV_WIDTH].reshape(1, batch, WINDOW, N_KV_HEADS, HEAD_DIM)
    new_v_p = tail[:, :, KV_WIDTH:].reshape(1, batch, WINDOW, N_KV_HEADS, HEAD_DIM)
    new_conv_p = pp3[:, seq - (CONV_WIDTH - 1):, P_GQKV:P_GQKV + GDN_CONV_DIM][None]

    xs = x_sample.reshape(nseq, D_MODEL)
    x1s, ps = _ffn_mix_in(xs, g1, wgu1, wo1, gm, win, nseq)
    hist = state_conv[l].reshape(nseq, (CONV_WIDTH - 1) * GDN_CONV_DIM)
    new_conv_s, qt, kt, v_s, beta_s, decay_s, kv_t = _sample_prep(ps, hist, convw, alog, dtb)
    time_minor = lambda c: jnp.transpose(c.reshape(nseq, WINDOW, KV_WIDTH), (0, 2, 1))
    time_major = lambda c: jnp.transpose(c, (0, 2, 1)).reshape(1, nseq, WINDOW, N_KV_HEADS, HEAD_DIM)
    q_s = ps[:, P_AQ:P_AQ + ATTN_Q_WIDTH].reshape(nseq, N_Q_HEADS, HEAD_DIM)
    attn_s, gdn_s, new_k_s, new_v_s, state_s = _sample_mix(
        sinks, q_s, ps, time_minor(cache_attn_k[l]), time_minor(cache_attn_v[l]), kv_t, qt, kt,
        v_s, beta_s, decay_s, state_gdn[l], ng)
    y_s = _mix_out_ffn(x1s, attn_s.reshape(nseq, ATTN_Q_WIDTH), gdn_s, wa, wb, g2, wgu2, wo2, gf,
                       nseq)

    return (y_p.reshape(batch, seq, D_MODEL), y_s.reshape(nseq, 1, D_MODEL),
            new_k_p, new_v_p, new_conv_p, state_p[None],
            time_major(new_k_s), time_major(new_v_s),
            new_conv_s.reshape(1, nseq, CONV_WIDTH - 1, GDN_CONV_DIM), state_s[None])
```

```python
import jax
import jax.numpy as jnp
from jax import lax
from jax.experimental import pallas as pl
from jax.experimental.pallas import tpu as pltpu

F32 = jnp.float32
BF16 = jnp.bfloat16

D_MODEL = 1024
D_FF = 2816
N_Q_HEADS = 8
N_KV_HEADS = 2
GROUP = N_Q_HEADS // N_KV_HEADS
HEAD_DIM = 64
WINDOW = 128
ATTN_Q_WIDTH = N_Q_HEADS * HEAD_DIM
KV_WIDTH = N_KV_HEADS * HEAD_DIM
N_GDN_HEADS = 4
GDN_DK = 128
GDN_DV = 128
GDN_QK_WIDTH = N_GDN_HEADS * GDN_DK
GDN_V_WIDTH = N_GDN_HEADS * GDN_DV
GDN_CONV_DIM = 2 * GDN_QK_WIDTH + GDN_V_WIDTH
CONV_WIDTH = 4
CHUNK = 64
EPS = 1e-6
LOG2E = 1.4426950408889634

LANES = 128
SUBLANES = 8
VMEM_LIMIT = 56 * 1024 * 1024

P_GQKV = 0
P_AQ = P_GQKV + GDN_CONV_DIM
P_GZ = P_AQ + ATTN_Q_WIDTH
P_KV = P_GZ + GDN_V_WIDTH
P_GBA = P_KV + 2 * KV_WIDTH
P_WIDTH = P_GBA + LANES

FF_CHUNK = 256
TOKEN_TILE = 512
GDN_TILE = 256
GDN_SEQS = 4
SWA_TILE = 1024
CONV_ROWS = 128
SAMPLE_BLOCK = 16


def _sigmoid(x):
    return 1.0 / (1.0 + jnp.exp(-x))


def _silu(x):
    half = 0.5 * x
    return half + half * jnp.tanh(half)


def _softplus(x):
    return jnp.maximum(x, 0.0) + jnp.log1p(jnp.exp(-jnp.abs(x)))


def _rms(x, g):
    return x * lax.rsqrt(jnp.mean(x * x, axis=-1, keepdims=True) + EPS) * g


def _dot(a, b):
    return jnp.dot(a.astype(BF16), b.astype(BF16), preferred_element_type=F32)


def _dot_nt(a, b):
    return lax.dot_general(a.astype(BF16), b.astype(BF16), (((1,), (1,)), ((), ())),
                           preferred_element_type=F32)


def _resident(shape):
    return pl.BlockSpec(shape, lambda *_: (0,) * len(shape), pipeline_mode=pl.Buffered(1))


def _params(*semantics):
    return pltpu.CompilerParams(dimension_semantics=semantics, vmem_limit_bytes=VMEM_LIMIT)


FF_STEPS = D_FF // FF_CHUNK


def _ffn_half_step(x, g, wgu_ref, wo_ref):
    n = _rms(x, g).astype(BF16)
    acc = jnp.zeros(x.shape, F32)
    for c in range(FF_STEPS):
        lo = c * FF_CHUNK
        gate = jnp.dot(n, wgu_ref[:, lo:lo + FF_CHUNK], preferred_element_type=F32)
        up = jnp.dot(n, wgu_ref[:, D_FF + lo:D_FF + lo + FF_CHUNK], preferred_element_type=F32)
        h = (_silu(gate) * up).astype(BF16)
        acc = acc + jnp.dot(h, wo_ref[lo:lo + FF_CHUNK, :], preferred_element_type=F32)
    return x + 0.5 * acc


_W_AQ, _W_KV = 0, ATTN_Q_WIDTH
_W_GQKV = _W_KV + 2 * KV_WIDTH
_W_GZ = _W_GQKV + GDN_CONV_DIM
_W_GBA = _W_GZ + GDN_V_WIDTH
_PROJ_GROUPS = ((_W_GQKV, GDN_CONV_DIM, P_GQKV), (_W_GBA, LANES, P_GBA), (_W_AQ, ATTN_Q_WIDTH, P_AQ),
                (_W_GZ, GDN_V_WIDTH, P_GZ), (_W_KV, 2 * KV_WIDTH, P_KV))


def _gdn_gates(gba, alog, dtb):
    beta = _sigmoid(gba)
    g = -jnp.exp(alog) * _softplus(gba + dtb)
    return beta, g


def _l2norm(x):
    return x * lax.rsqrt(jnp.sum(x * x, axis=-1, keepdims=True) + EPS)


def _ffn_mix_in_kernel(x_ref, g1_ref, wgu_ref, wo_ref, gm_ref, win_ref, x1_ref, p_ref):
    x1 = _ffn_half_step(x_ref[...], g1_ref[...], wgu_ref, wo_ref)
    x1_ref[...] = x1
    n = _rms(x1, gm_ref[...]).astype(BF16)
    for src, width, dst in _PROJ_GROUPS:
        p_ref[:, dst:dst + width] = jnp.dot(n, win_ref[:, src:src + width],
                                            preferred_element_type=F32)


def _ffn_weight_specs():
    return [_resident((1, D_MODEL)), _resident((D_MODEL, 2 * D_FF)), _resident((D_FF, D_MODEL)),
            _resident((1, D_MODEL)), _resident((D_MODEL, P_WIDTH))]


def _ffn_mix_in(x, g1, wgu, wo, gm, win, tm):
    n_tok = x.shape[0]
    row = lambda w: pl.BlockSpec((tm, w), lambda i: (i, 0))
    return pl.pallas_call(
        _ffn_mix_in_kernel,
        grid=(n_tok // tm,),
        in_specs=[row(D_MODEL)] + _ffn_weight_specs(),
        out_specs=[row(D_MODEL), row(P_WIDTH)],
        out_shape=[jax.ShapeDtypeStruct((n_tok, D_MODEL), F32),
                   jax.ShapeDtypeStruct((n_tok, P_WIDTH), F32)],
        compiler_params=_params("parallel"),
        name="ffn_mix_in",
    )(x, g1, wgu, wo, gm, win)


def _mix_out_ffn_kernel(x1_ref, attn_ref, gdn_ref, wa_ref, wb_ref, g2_ref, wgu_ref, wo_ref, gf_ref,
                        y_ref):
    x2 = x1_ref[...] + _dot(attn_ref[...], wa_ref[...]) + _dot(gdn_ref[...], wb_ref[...])
    x3 = _ffn_half_step(x2, g2_ref[...], wgu_ref, wo_ref)
    y_ref[...] = _rms(x3, gf_ref[...])


def _mix_out_ffn(x1, attn, gdn, wa, wb, g2, wgu, wo, gf, tm):
    n_tok = x1.shape[0]
    row = lambda w: pl.BlockSpec((tm, w), lambda i: (i, 0))
    return pl.pallas_call(
        _mix_out_ffn_kernel,
        grid=(n_tok // tm,),
        in_specs=[row(D_MODEL), row(ATTN_Q_WIDTH), row(GDN_V_WIDTH),
                  _resident((ATTN_Q_WIDTH, D_MODEL)), _resident((GDN_V_WIDTH, D_MODEL)),
                  _resident((1, D_MODEL)), _resident((D_MODEL, 2 * D_FF)),
                  _resident((D_FF, D_MODEL)), _resident((1, D_MODEL))],
        out_specs=row(D_MODEL),
        out_shape=jax.ShapeDtypeStruct((n_tok, D_MODEL), F32),
        compiler_params=_params("parallel"),
        name="mix_out_ffn",
    )(x1, attn, gdn, wa, wb, g2, wgu, wo, gf)


def _alibi_slope(head):
    return 2.0 ** (-8.0 * (head + 1) / N_Q_HEADS)


def _swa_prompt_kernel(sink_ref, q_ref, kvc_ref, kvp_ref, o_ref, bias_ref, s_ref):
    blk = pl.program_id(1)
    nkeys = 2 * WINDOW

    @pl.when((pl.program_id(0) == 0) & (blk == 0))
    def _():
        key = lax.broadcasted_iota(jnp.int32, (nkeys, WINDOW), 0)
        qry = lax.broadcasted_iota(jnp.int32, (nkeys, WINDOW), 1)
        dist = qry - key + WINDOW
        valid = (dist >= 0) & (dist <= WINDOW)
        for h in range(N_Q_HEADS):
            g = h % GROUP
            bias_ref[h // GROUP, :, g * WINDOW:(g + 1) * WINDOW] = jnp.where(
                valid, -(LOG2E * _alibi_slope(h)) * dist.astype(F32), -jnp.inf)

    first_pen = jnp.where(blk > 0, 0.0, -jnp.inf)
    scale = LOG2E * HEAD_DIM ** -0.5
    nsub = q_ref.shape[0] // WINDOW
    jobs = [(j, kvh) for j in range(nsub) for kvh in range(N_KV_HEADS)]
    group_heads = lambda kvh: range(kvh * GROUP, (kvh + 1) * GROUP)
    def keys_values(j, col):
        if j == 0:
            return jnp.concatenate([kvp_ref[:, col], kvc_ref[0:WINDOW, col]], axis=0)
        return kvc_ref[(j - 1) * WINDOW:(j + 1) * WINDOW, col]

    values_t = []
    for job, (j, kvh) in enumerate(jobs):
        rows = slice(j * WINDOW, (j + 1) * WINDOW)
        k_all = keys_values(j, slice(kvh * HEAD_DIM, (kvh + 1) * HEAD_DIM)).astype(BF16)
        v_t = keys_values(j, slice(KV_WIDTH + kvh * HEAD_DIM, KV_WIDTH + (kvh + 1) * HEAD_DIM)).T
        values_t.append(jnp.concatenate([v_t, jnp.ones((SUBLANES, 2 * WINDOW), F32)],
                                        axis=0).astype(BF16))
        q_stack = jnp.concatenate(
            [(q_ref[rows, h * HEAD_DIM:(h + 1) * HEAD_DIM] * scale).astype(BF16)
             for h in group_heads(kvh)], axis=0)
        s_ref[job] = lax.dot_general(k_all, q_stack, (((1,), (1,)), ((), ())),
                                     preferred_element_type=F32)
    for job, (j, kvh) in enumerate(jobs):
        rows = slice(j * WINDOW, (j + 1) * WINDOW)
        v_t = values_t[job]
        outs = []
        for g, h in enumerate(group_heads(kvh)):
            lanes = slice(g * WINDOW, (g + 1) * WINDOW)
            s = s_ref[job, :, lanes] + bias_ref[kvh, :, lanes]
            if j == 0:
                s = jnp.concatenate([s[:WINDOW] + first_pen, s[WINDOW:]], axis=0)
            sink = LOG2E * sink_ref[h]
            m = jnp.maximum(jnp.max(s, axis=0, keepdims=True), sink)
            p = jnp.exp2(s - m).astype(BF16)
            pv = jnp.dot(v_t, p, preferred_element_type=F32)
            denom = pv[HEAD_DIM:HEAD_DIM + 1] + jnp.exp2(sink - m)
            outs.append(pv[:HEAD_DIM] / denom)
        for pair in range(GROUP // 2):
            lo = (kvh * GROUP + 2 * pair) * HEAD_DIM
            o_ref[rows, lo:lo + 2 * HEAD_DIM] = jnp.concatenate(outs[2 * pair:2 * pair + 2], axis=0).T


def _swa_prompt(sinks, p3):
    batch, seq, _ = p3.shape
    tq = SWA_TILE
    nsub = tq // WINDOW
    q_blk = P_AQ // ATTN_Q_WIDTH
    kv_blk = P_KV // (2 * KV_WIDTH)
    return pl.pallas_call(
        _swa_prompt_kernel,
        grid=(batch, seq // tq),
        in_specs=[pl.BlockSpec(memory_space=pltpu.SMEM),
                  pl.BlockSpec((None, tq, ATTN_Q_WIDTH), lambda b, i: (b, i, q_blk)),
                  pl.BlockSpec((None, tq, 2 * KV_WIDTH), lambda b, i: (b, i, kv_blk)),
                  pl.BlockSpec((None, WINDOW, 2 * KV_WIDTH),
                               lambda b, i: (b, jnp.maximum(i * nsub - 1, 0), kv_blk))],
        out_specs=pl.BlockSpec((None, tq, ATTN_Q_WIDTH), lambda b, i: (b, i, 0)),
        out_shape=jax.ShapeDtypeStruct((batch, seq, ATTN_Q_WIDTH), F32),
        scratch_shapes=[pltpu.VMEM((N_KV_HEADS, 2 * WINDOW, GROUP * WINDOW), F32),
                        pltpu.VMEM((nsub * N_KV_HEADS, 2 * WINDOW, GROUP * WINDOW), F32)],
        compiler_params=_params("arbitrary", "arbitrary"),
        name="swa_prompt",
    )(sinks, p3, p3, p3)


INV_LEVELS = tuple(2 ** i for i in range(1, CHUNK.bit_length()))


def _gdn_prompt_kernel(raw_ref, prev_ref, gz_ref, gba_ref, convw_ref, alog_ref, dtb_ref, ng_ref,
                       o_ref, s_ref, mask_ref, act_ref):
    t = pl.program_id(1)
    tt = GDN_TILE
    nchunk = tt // CHUNK
    heads = range(N_GDN_HEADS)

    r = lax.broadcasted_iota(jnp.int32, (tt, tt), 0)
    c = lax.broadcasted_iota(jnp.int32, (tt, tt), 1)
    xor_idx = r ^ c

    @pl.when((pl.program_id(0) == 0) & (t == 0))
    def _():
        mask_ref[0] = jnp.where((xor_idx < CHUNK) & (r >= c), 1.0, 0.0).astype(BF16)
        for i, s in enumerate(INV_LEVELS):
            mask_ref[i + 1] = jnp.where((xor_idx >= s // 2) & (xor_idx < s) & (r > c),
                                        1.0, 0.0).astype(BF16)

    @pl.when(t == 0)
    def _():
        s_ref[...] = jnp.zeros(s_ref.shape, F32)

    strict = (xor_idx < CHUNK) & (r > c)
    diag = r == c

    nseq = raw_ref.shape[0]
    chains = [(sq, h) for sq in range(nseq) for h in heads]
    ids = range(len(chains))

    for sq in range(nseq):
        for cg in range(GDN_CONV_DIM // LANES):
            cols = slice(cg * LANES, (cg + 1) * LANES)
            w = convw_ref[:, cols]
            for r0 in range(0, tt, CONV_ROWS):
                if r0 == 0:
                    before = jnp.where(t > 0, prev_ref[sq, :, cols], 0.0)
                else:
                    before = raw_ref[sq, r0 - SUBLANES:r0, cols]
                cur = raw_ref[sq, r0:r0 + CONV_ROWS, cols]
                ext = jnp.concatenate([before, cur], axis=0)
                acc = w[0:1] * ext
                for j in range(1, CONV_WIDTH):
                    acc = pltpu.roll(acc, 1, axis=0) + w[j:j + 1] * ext
                act = _silu(acc[SUBLANES:])
                if cg < GDN_QK_WIDTH // LANES:
                    act = _l2norm(act) * (GDN_DK ** -0.5)
                elif cg < 2 * GDN_QK_WIDTH // LANES:
                    act = _l2norm(act)
                act_ref[sq, r0:r0 + CONV_ROWS, cols] = act

    gates, gc, gc_t, gtot = [], [], [], []
    for sq in range(nseq):
        beta_all, g_all = _gdn_gates(gba_ref[sq], alog_ref[...], dtb_ref[...])
        g_hi = g_all.astype(BF16)
        g_lo = (g_all - g_hi.astype(F32)).astype(BF16)
        cs = jnp.dot(mask_ref[0], jnp.concatenate([g_hi, g_lo], axis=1),
                     preferred_element_type=F32)
        gc_sq = cs[:, :LANES] + cs[:, LANES:]
        gates.append(beta_all)
        gc.append(gc_sq)
        gc_t.append(gc_sq.T)
        gtot.append(jnp.concatenate(
            [jnp.broadcast_to(gc_sq[(ci + 1) * CHUNK - 1:(ci + 1) * CHUNK, :], (CHUNK, LANES))
             for ci in range(nchunk)], axis=0))

    q_l, k_l, v_l, beta_l, gcol_l, gend_l, a_l, qk_l, x_l = [], [], [], [], [], [], [], [], []
    for sq, h in chains:
        q = act_ref[sq, :, h * GDN_DK:(h + 1) * GDN_DK]
        k = act_ref[sq, :, GDN_QK_WIDTH + h * GDN_DK:GDN_QK_WIDTH + (h + 1) * GDN_DK]
        beta = gates[sq][:, h:h + 1]
        gcol = gc[sq][:, N_GDN_HEADS + h:N_GDN_HEADS + h + 1]
        grow = gc_t[sq][N_GDN_HEADS + h:N_GDN_HEADS + h + 1, :]
        decay = jnp.exp(jnp.where(strict, gcol - grow, -jnp.inf))
        kb = k.astype(BF16)
        qk_kk = _dot_nt(jnp.concatenate([q.astype(BF16), kb], axis=0), kb)
        a = ((beta * qk_kk[tt:]) * decay).astype(BF16)
        qk_l.append((qk_kk[:tt] * jnp.where(diag, 1.0, decay)).astype(BF16))
        x_l.append(jnp.where(diag, 1.0, 0.0).astype(BF16) - a * mask_ref[1])
        q_l.append(q); k_l.append(k); beta_l.append(beta); gcol_l.append(gcol); a_l.append(a)
        gend_l.append(gtot[sq][:, N_GDN_HEADS + h:N_GDN_HEADS + h + 1])
        v_l.append(act_ref[sq, :, 2 * GDN_QK_WIDTH + h * GDN_DV:2 * GDN_QK_WIDTH + (h + 1) * GDN_DV])

    for i in range(1, len(INV_LEVELS)):
        for n in ids:
            x = x_l[n]
            y = jnp.dot(x, a_l[n], preferred_element_type=F32).astype(BF16)
            z = jnp.dot(y, x, preferred_element_type=F32).astype(BF16)
            x_l[n] = x - z * mask_ref[i + 1]

    u_l, w_l, qd_l, kd_l = [], [], [], []
    for n in ids:
        eg = jnp.exp(gcol_l[n])
        rhs = jnp.concatenate([v_l[n] * beta_l[n], k_l[n] * (beta_l[n] * eg)], axis=1)
        uw = jnp.dot(x_l[n], rhs.astype(BF16), preferred_element_type=F32)
        u_l.append(uw[:, :GDN_DV])
        w_l.append(uw[:, GDN_DV:].astype(BF16))
        qd_l.append((q_l[n] * eg).astype(BF16))
        kd_l.append((k_l[n] * jnp.exp(gend_l[n] - gcol_l[n])).astype(BF16))

    states = [s_ref[sq, h] for sq, h in chains]
    v_new = [[] for _ in ids]
    o_inter = [[] for _ in ids]
    for ci in range(nchunk):
        rows = slice(ci * CHUNK, (ci + 1) * CHUNK)
        for n in ids:
            sb = states[n].astype(BF16)
            ws_qs = jnp.dot(jnp.concatenate([w_l[n][rows], qd_l[n][rows]], axis=0), sb,
                            preferred_element_type=F32)
            vn = u_l[n][rows] - ws_qs[:CHUNK]
            o_inter[n].append(ws_qs[CHUNK:])
            g_last = jnp.exp(gend_l[n][ci * CHUNK:ci * CHUNK + 1, :])
            upd = lax.dot_general(kd_l[n][rows], vn.astype(BF16), (((0,), (0,)), ((), ())),
                                  preferred_element_type=F32)
            states[n] = states[n] * g_last + upd
            v_new[n].append(vn)

    for n, (sq, h) in enumerate(chains):
        s_ref[sq, h] = states[n]
        vn_all = jnp.concatenate(v_new[n], axis=0).astype(BF16)
        o = jnp.concatenate(o_inter[n], axis=0) + jnp.dot(qk_l[n], vn_all,
                                                           preferred_element_type=F32)
        gz = gz_ref[sq, :, h * GDN_DV:(h + 1) * GDN_DV]
        o_ref[sq, :, h * GDN_DV:(h + 1) * GDN_DV] = _rms(o, ng_ref[...]) * _silu(gz)


def _gdn_prompt(p3, convw, alog, dtb, ng):
    batch, seq, _ = p3.shape
    tt = GDN_TILE
    ns = GDN_SEQS if batch % GDN_SEQS == 0 else 1
    prev_per_tile = tt // SUBLANES
    small = lambda shape: pl.BlockSpec(shape, lambda b, t: (0,) * len(shape))
    return pl.pallas_call(
        _gdn_prompt_kernel,
        grid=(batch // ns, seq // tt),
        in_specs=[pl.BlockSpec((ns, tt, GDN_CONV_DIM), lambda b, t: (b, t, P_GQKV // GDN_CONV_DIM)),
                  pl.BlockSpec((ns, SUBLANES, GDN_CONV_DIM),
                               lambda b, t: (b, jnp.maximum(t * prev_per_tile - 1, 0),
                                             P_GQKV // GDN_CONV_DIM)),
                  pl.BlockSpec((ns, tt, GDN_V_WIDTH), lambda b, t: (b, t, P_GZ // GDN_V_WIDTH)),
                  pl.BlockSpec((ns, tt, LANES), lambda b, t: (b, t, P_GBA // LANES)),
                  small((CONV_WIDTH, GDN_CONV_DIM)), small((1, LANES)), small((1, LANES)),
                  small((1, GDN_DV))],
        out_specs=[pl.BlockSpec((ns, tt, GDN_V_WIDTH), lambda b, t: (b, t, 0)),
                   pl.BlockSpec((ns, N_GDN_HEADS, GDN_DK, GDN_DV), lambda b, t: (b, 0, 0, 0))],
        out_shape=[jax.ShapeDtypeStruct((batch, seq, GDN_V_WIDTH), F32),
                   jax.ShapeDtypeStruct((batch, N_GDN_HEADS, GDN_DK, GDN_DV), F32)],
        scratch_shapes=[pltpu.VMEM((len(INV_LEVELS) + 1, tt, tt), BF16),
                        pltpu.VMEM((ns, tt, GDN_CONV_DIM), F32)],
        compiler_params=_params("arbitrary", "arbitrary"),
        name="gdn_prompt",
    )(p3, p3, p3, p3, convw, alog, dtb, ng)


def _sample_prep_kernel(p_ref, hist_ref, convw_ref, alog_ref, dtb_ref,
                        conv_ref, qt_ref, kt_ref, v_ref, beta_ref, decay_ref, kvt_ref):
    nblk = qt_ref.shape[0]
    kv_t = p_ref[:, P_KV:P_KV + 2 * KV_WIDTH].T
    for i in range(nblk):
        kvt_ref[i] = kv_t[:, i * SAMPLE_BLOCK:(i + 1) * SAMPLE_BLOCK]
    raw = p_ref[:, P_GQKV:P_GQKV + GDN_CONV_DIM]
    w = convw_ref[...]
    conv = w[0:1] * hist_ref[:, 0:GDN_CONV_DIM]
    for j in range(1, CONV_WIDTH - 1):
        conv = conv + w[j:j + 1] * hist_ref[:, j * GDN_CONV_DIM:(j + 1) * GDN_CONV_DIM]
    conv = conv + w[CONV_WIDTH - 1:CONV_WIDTH] * raw
    act = _silu(conv)
    conv_ref[:, 0:(CONV_WIDTH - 2) * GDN_CONV_DIM] = hist_ref[:, GDN_CONV_DIM:]
    conv_ref[:, (CONV_WIDTH - 2) * GDN_CONV_DIM:] = raw
    v_ref[...] = act[:, 2 * GDN_QK_WIDTH:]
    beta, g = _gdn_gates(p_ref[:, P_GBA:P_GBA + LANES], alog_ref[...], dtb_ref[...])
    beta_ref[...] = beta
    decay_ref[...] = jnp.exp(g)
    for h in range(N_GDN_HEADS):
        qt = (_l2norm(act[:, h * GDN_DK:(h + 1) * GDN_DK]) * (GDN_DK ** -0.5)).T
        kt = _l2norm(act[:, GDN_QK_WIDTH + h * GDN_DK:GDN_QK_WIDTH + (h + 1) * GDN_DK]).T
        for i in range(nblk):
            qt_ref[i, h] = qt[:, i * SAMPLE_BLOCK:(i + 1) * SAMPLE_BLOCK]
            kt_ref[i, h] = kt[:, i * SAMPLE_BLOCK:(i + 1) * SAMPLE_BLOCK]


def _sample_prep(p, hist, convw, alog, dtb):
    nseq = p.shape[0]
    nblk = nseq // SAMPLE_BLOCK
    hist_w = (CONV_WIDTH - 1) * GDN_CONV_DIM
    cols = jax.ShapeDtypeStruct((nblk, N_GDN_HEADS, GDN_DK, SAMPLE_BLOCK), F32)
    return pl.pallas_call(
        _sample_prep_kernel,
        out_shape=[jax.ShapeDtypeStruct((nseq, hist_w), F32), cols, cols,
                   jax.ShapeDtypeStruct((nseq, GDN_V_WIDTH), F32),
                   jax.ShapeDtypeStruct((nseq, LANES), F32),
                   jax.ShapeDtypeStruct((nseq, LANES), F32),
                   jax.ShapeDtypeStruct((nblk, 2 * KV_WIDTH, SAMPLE_BLOCK), F32)],
        compiler_params=pltpu.CompilerParams(vmem_limit_bytes=VMEM_LIMIT),
        name="sample_prep",
    )(p, hist, convw, alog, dtb)


def _per_head(values):
    h = lax.broadcasted_iota(jnp.int32, (N_Q_HEADS, 1), 0)
    col = jnp.full((N_Q_HEADS, 1), values[N_Q_HEADS - 1], F32)
    for i in range(N_Q_HEADS - 2, -1, -1):
        col = jnp.where(h == i, values[i], col)
    return col


def _sample_mix_kernel(sink_ref, q_ref, p_ref, kc_ref, vc_ref, kvt_ref, qt_ref, kt_ref, v_ref,
                       beta_ref, decay_ref, s_ref, ng_ref, attn_ref, gdn_ref, nk_ref, nv_ref, ns_ref):
    time = lax.broadcasted_iota(jnp.int32, (KV_WIDTH, WINDOW), 1)
    newest = time == WINDOW - 1
    key_pos = lax.broadcasted_iota(jnp.int32, (1, WINDOW), 1)
    dist_hist = (WINDOW - key_pos).astype(F32)
    head = lax.broadcasted_iota(jnp.int32, (N_Q_HEADS, HEAD_DIM), 0)
    kv_of_head = [head // GROUP == kvh for kvh in range(N_KV_HEADS)]
    slope = _per_head([_alibi_slope(h) for h in range(N_Q_HEADS)])
    sink = _per_head([sink_ref[h] for h in range(N_Q_HEADS)])
    scale = HEAD_DIM ** -0.5
    for b in range(SAMPLE_BLOCK):
        k_hist = kc_ref[b]
        v_hist = vc_ref[b]
        k_new = p_ref[b:b + 1, P_KV:P_KV + KV_WIDTH]
        v_new = p_ref[b:b + 1, P_KV + KV_WIDTH:P_KV + 2 * KV_WIDTH]
        nk_ref[b] = jnp.where(newest, kvt_ref[0:KV_WIDTH, b:b + 1],
                              pltpu.roll(k_hist, WINDOW - 1, axis=1))
        nv_ref[b] = jnp.where(newest, kvt_ref[KV_WIDTH:2 * KV_WIDTH, b:b + 1],
                              pltpu.roll(v_hist, WINDOW - 1, axis=1))
        q = q_ref[b]
        q_wide = jnp.concatenate([jnp.where(sel, q, 0.0) for sel in kv_of_head], axis=1)
        s_hist = _dot(q_wide, k_hist) * scale - slope * dist_hist
        s_new = jnp.sum(q_wide * k_new, axis=-1, keepdims=True) * scale
        m = jnp.maximum(jnp.maximum(jnp.max(s_hist, axis=-1, keepdims=True), s_new), sink)
        p_hist = jnp.exp(s_hist - m)
        p_new = jnp.exp(s_new - m)
        denom = jnp.sum(p_hist, axis=-1, keepdims=True) + p_new + jnp.exp(sink - m)
        o_wide = (_dot_nt(p_hist, v_hist) + p_new * v_new) / denom
        o = o_wide[:, 0:HEAD_DIM]
        for kvh in range(1, N_KV_HEADS):
            o = jnp.where(kv_of_head[kvh], o_wide[:, kvh * HEAD_DIM:(kvh + 1) * HEAD_DIM], o)
        attn_ref[b] = o

    heads = range(N_GDN_HEADS)
    for b in range(SAMPLE_BLOCK):
        kcol = [jnp.broadcast_to(kt_ref[h, :, b:b + 1], (GDN_DK, GDN_DV)) for h in heads]
        qcol = [jnp.broadcast_to(qt_ref[h, :, b:b + 1], (GDN_DK, GDN_DV)) for h in heads]
        decayed = [s_ref[b, h] * decay_ref[b:b + 1, N_GDN_HEADS + h:N_GDN_HEADS + h + 1]
                   for h in heads]
        ks = [jnp.sum(decayed[h] * kcol[h], axis=0, keepdims=True) for h in heads]
        delta = [beta_ref[b:b + 1, h:h + 1] * (v_ref[b:b + 1, h * GDN_DV:(h + 1) * GDN_DV] - ks[h])
                 for h in heads]
        state = [decayed[h] + kcol[h] * delta[h] for h in heads]
        outs = []
        for h in heads:
            ns_ref[b, h] = state[h]
            o = jnp.sum(state[h] * qcol[h], axis=0, keepdims=True)
            gz = p_ref[b:b + 1, P_GZ + h * GDN_DV:P_GZ + (h + 1) * GDN_DV]
            outs.append(_rms(o, ng_ref[...]) * _silu(gz))
        gdn_ref[b:b + 1, :] = jnp.concatenate(outs, axis=1)


def _sample_mix(sinks, q, p, k_hist, v_hist, kv_t, qt, kt, v, beta, decay, state, ng):
    nseq = p.shape[0]
    bb = SAMPLE_BLOCK
    rows = lambda w: pl.BlockSpec((bb, w), lambda i: (i, 0))
    heads = pl.BlockSpec((bb, N_Q_HEADS, HEAD_DIM), lambda i: (i, 0, 0))
    cache = pl.BlockSpec((bb, KV_WIDTH, WINDOW), lambda i: (i, 0, 0))
    cols = pl.BlockSpec((None, N_GDN_HEADS, GDN_DK, bb), lambda i: (i, 0, 0, 0))
    kv_cols = pl.BlockSpec((None, 2 * KV_WIDTH, bb), lambda i: (i, 0, 0))
    st = pl.BlockSpec((bb, N_GDN_HEADS, GDN_DK, GDN_DV), lambda i: (i, 0, 0, 0))
    return pl.pallas_call(
        _sample_mix_kernel,
        grid=(nseq // bb,),
        in_specs=[pl.BlockSpec(memory_space=pltpu.SMEM), heads, rows(P_WIDTH), cache, cache,
                  kv_cols, cols, cols, rows(GDN_V_WIDTH), rows(LANES), rows(LANES), st,
                  pl.BlockSpec((1, GDN_DV), lambda i: (0, 0))],
        out_specs=[heads, rows(GDN_V_WIDTH), cache, cache, st],
        out_shape=[jax.ShapeDtypeStruct((nseq, N_Q_HEADS, HEAD_DIM), F32),
                   jax.ShapeDtypeStruct((nseq, GDN_V_WIDTH), F32),
                   jax.ShapeDtypeStruct(k_hist.shape, F32),
                   jax.ShapeDtypeStruct(v_hist.shape, F32),
                   jax.ShapeDtypeStruct(state.shape, F32)],
        compiler_params=_params("parallel"),
        name="sample_mix",
    )(sinks, q, p, k_hist, v_hist, kv_t, qt, kt, v, beta, decay, state, ng)


def _lane_pad(vec, offset):
    return jnp.zeros((1, LANES), F32).at[0, offset:offset + vec.shape[0]].set(vec)


def kernel(x_prompt, x_sample, cache_attn_k, cache_attn_v, state_conv, state_gdn, ffn1_norm_g,
           ffn1_w_in, ffn1_w_out, mix_norm_g, w_in_mix, attn_sinks, conv_w, gdn_A_log, gdn_dt_bias,
           gdn_norm_g, w_out_mix, ffn2_norm_g, ffn2_w_in, ffn2_w_out, final_norm_g):
    depth = ffn1_w_in.shape[0]
    assert depth == 1, "single-layer trunk"
    batch, seq, _ = x_prompt.shape
    nseq = x_sample.shape[0]
    assert x_sample.shape[1] == 1 and cache_attn_k.shape[2] == WINDOW
    assert seq % GDN_TILE == 0 and seq % WINDOW == 0 and nseq % SAMPLE_BLOCK == 0
    l = 0
    row = lambda v: v.reshape(1, -1)
    g1, gm, g2, gf = row(ffn1_norm_g[l]), row(mix_norm_g[l]), row(ffn2_norm_g[l]), row(final_norm_g)
    ng = row(gdn_norm_g[l])
    wgu1, wo1 = ffn1_w_in[l].astype(BF16), ffn1_w_out[l].astype(BF16)
    wgu2, wo2 = ffn2_w_in[l].astype(BF16), ffn2_w_out[l].astype(BF16)
    win = jnp.pad(w_in_mix[l].astype(BF16), ((0, 0), (0, P_WIDTH - w_in_mix.shape[2])))
    wa = w_out_mix[l, :ATTN_Q_WIDTH].astype(BF16)
    wb = w_out_mix[l, ATTN_Q_WIDTH:].astype(BF16)
    alog = _lane_pad(gdn_A_log[l], N_GDN_HEADS)
    dtb = _lane_pad(gdn_dt_bias[l], N_GDN_HEADS)
    sinks = attn_sinks[l]
    convw = conv_w[l]

    xp = x_prompt.reshape(batch * seq, D_MODEL)
    x1p, pp = _ffn_mix_in(xp, g1, wgu1, wo1, gm, win, TOKEN_TILE)
    pp3 = pp.reshape(batch, seq, P_WIDTH)
    attn_p = _swa_prompt(sinks, pp3)
    gdn_p, state_p = _gdn_prompt(pp3, convw, alog, dtb, ng)
    y_p = _mix_out_ffn(x1p, attn_p.reshape(batch * seq, ATTN_Q_WIDTH),
                       gdn_p.reshape(batch * seq, GDN_V_WIDTH), wa, wb, g2, wgu2, wo2, gf, TOKEN_TILE)
    tail = pp3[:, seq - WINDOW:, P_KV:P_KV + 2 * KV_WIDTH]
    new_k_p = tail[:, :, :KV_WIDTH].reshape(1, batch, WINDOW, N_KV_HEADS, HEAD_DIM)
    new_v_p = tail[:, :, KV_WIDTH:].reshape(1, batch, WINDOW, N_KV_HEADS, HEAD_DIM)
    new_conv_p = pp3[:, seq - (CONV_WIDTH - 1):, P_GQKV:P_GQKV + GDN_CONV_DIM][None]

    xs = x_sample.reshape(nseq, D_MODEL)
    x1s, ps = _ffn_mix_in(xs, g1, wgu1, wo1, gm, win, nseq)
    hist = state_conv[l].reshape(nseq, (CONV_WIDTH - 1) * GDN_CONV_DIM)
    new_conv_s, qt, kt, v_s, beta_s, decay_s, kv_t = _sample_prep(ps, hist, convw, alog, dtb)
    time_minor = lambda c: jnp.transpose(c.reshape(nseq, WINDOW, KV_WIDTH), (0, 2, 1))
    time_major = lambda c: jnp.transpose(c, (0, 2, 1)).reshape(1, nseq, WINDOW, N_KV_HEADS, HEAD_DIM)
    q_s = ps[:, P_AQ:P_AQ + ATTN_Q_WIDTH].reshape(nseq, N_Q_HEADS, HEAD_DIM)
    attn_s, gdn_s, new_k_s, new_v_s, state_s = _sample_mix(
        sinks, q_s, ps, time_minor(cache_attn_k[l]), time_minor(cache_attn_v[l]), kv_t, qt, kt,
        v_s, beta_s, decay_s, state_gdn[l], ng)
    y_s = _mix_out_ffn(x1s, attn_s.reshape(nseq, ATTN_Q_WIDTH), gdn_s, wa, wb, g2, wgu2, wo2, gf,
                       nseq)

    return (y_p.reshape(batch, seq, D_MODEL), y_s.reshape(nseq, 1, D_MODEL),
            new_k_p, new_v_p, new_conv_p, state_p[None],
            time_major(new_k_s), time_major(new_v_s),
            new_conv_s.reshape(1, nseq, CONV_WIDTH - 1, GDN_CONV_DIM), state_s[None])
```

```python
import jax
import jax.numpy as jnp
from jax import lax
from jax.experimental import pallas as pl
from jax.experimental.pallas import tpu as pltpu

F32 = jnp.float32
BF16 = jnp.bfloat16

D_MODEL = 1024
D_FF = 2816
N_Q_HEADS = 8
N_KV_HEADS = 2
GROUP = N_Q_HEADS // N_KV_HEADS
HEAD_DIM = 64
WINDOW = 128
ATTN_Q_WIDTH = N_Q_HEADS * HEAD_DIM
KV_WIDTH = N_KV_HEADS * HEAD_DIM
N_GDN_HEADS = 4
GDN_DK = 128
GDN_DV = 128
GDN_QK_WIDTH = N_GDN_HEADS * GDN_DK
GDN_V_WIDTH = N_GDN_HEADS * GDN_DV
GDN_CONV_DIM = 2 * GDN_QK_WIDTH + GDN_V_WIDTH
CONV_WIDTH = 4
CHUNK = 64
EPS = 1e-6
LOG2E = 1.4426950408889634

LANES = 128
SUBLANES = 8
VMEM_LIMIT = 56 * 1024 * 1024

P_GQKV = 0
P_AQ = P_GQKV + GDN_CONV_DIM
P_GZ = P_AQ + ATTN_Q_WIDTH
P_KV = P_GZ + GDN_V_WIDTH
P_GBA = P_KV + 2 * KV_WIDTH
P_WIDTH = P_GBA + LANES

FF_CHUNK = 256
TOKEN_TILE = 512
GDN_TILE = 256
GDN_SEQS = 4
SWA_TILE = 1024
CONV_ROWS = 128
SAMPLE_BLOCK = 16


def _sigmoid(x):
    return 1.0 / (1.0 + jnp.exp(-x))


def _silu(x):
    half = 0.5 * x
    return half + half * jnp.tanh(half)


def _softplus(x):
    return jnp.maximum(x, 0.0) + jnp.log1p(jnp.exp(-jnp.abs(x)))


def _rms(x, g):
    return x * lax.rsqrt(jnp.mean(x * x, axis=-1, keepdims=True) + EPS) * g


def _dot(a, b):
    return jnp.dot(a.astype(BF16), b.astype(BF16), preferred_element_type=F32)


def _dot_nt(a, b):
    return lax.dot_general(a.astype(BF16), b.astype(BF16), (((1,), (1,)), ((), ())),
                           preferred_element_type=F32)


def _resident(shape):
    return pl.BlockSpec(shape, lambda *_: (0,) * len(shape), pipeline_mode=pl.Buffered(1))


def _params(*semantics):
    return pltpu.CompilerParams(dimension_semantics=semantics, vmem_limit_bytes=VMEM_LIMIT)


FF_STEPS = D_FF // FF_CHUNK


def _ffn_half_step(x, g, wgu_ref, wo_ref):
    n = _rms(x, g).astype(BF16)
    hidden = []
    for c in range(FF_STEPS):
        lo = c * FF_CHUNK
        gate = jnp.dot(n, wgu_ref[:, lo:lo + FF_CHUNK], preferred_element_type=F32)
        up = jnp.dot(n, wgu_ref[:, D_FF + lo:D_FF + lo + FF_CHUNK], preferred_element_type=F32)
        hidden.append((_silu(gate) * up).astype(BF16))
    out = jnp.dot(jnp.concatenate(hidden, axis=1), wo_ref[...], preferred_element_type=F32)
    return x + 0.5 * out


_W_AQ, _W_KV = 0, ATTN_Q_WIDTH
_W_GQKV = _W_KV + 2 * KV_WIDTH
_W_GZ = _W_GQKV + GDN_CONV_DIM
_W_GBA = _W_GZ + GDN_V_WIDTH
_PROJ_GROUPS = ((_W_GQKV, GDN_CONV_DIM, P_GQKV), (_W_GBA, LANES, P_GBA), (_W_AQ, ATTN_Q_WIDTH, P_AQ),
                (_W_GZ, GDN_V_WIDTH, P_GZ), (_W_KV, 2 * KV_WIDTH, P_KV))


def _gdn_gates(gba, alog, dtb):
    beta = _sigmoid(gba)
    g = -jnp.exp(alog) * _softplus(gba + dtb)
    return beta, g


def _l2norm(x):
    return x * lax.rsqrt(jnp.sum(x * x, axis=-1, keepdims=True) + EPS)


def _ffn_mix_in_kernel(x_ref, g1_ref, wgu_ref, wo_ref, gm_ref, win_ref, x1_ref, p_ref):
    x1 = _ffn_half_step(x_ref[...], g1_ref[...], wgu_ref, wo_ref)
    x1_ref[...] = x1
    n = _rms(x1, gm_ref[...]).astype(BF16)
    for src, width, dst in _PROJ_GROUPS:
        p_ref[:, dst:dst + width] = jnp.dot(n, win_ref[:, src:src + width],
                                            preferred_element_type=F32)


def _ffn_weight_specs():
    return [_resident((1, D_MODEL)), _resident((D_MODEL, 2 * D_FF)), _resident((D_FF, D_MODEL)),
            _resident((1, D_MODEL)), _resident((D_MODEL, P_WIDTH))]


def _ffn_mix_in(x, g1, wgu, wo, gm, win, tm):
    n_tok = x.shape[0]
    row = lambda w: pl.BlockSpec((tm, w), lambda i: (i, 0))
    return pl.pallas_call(
        _ffn_mix_in_kernel,
        grid=(n_tok // tm,),
        in_specs=[row(D_MODEL)] + _ffn_weight_specs(),
        out_specs=[row(D_MODEL), row(P_WIDTH)],
        out_shape=[jax.ShapeDtypeStruct((n_tok, D_MODEL), F32),
                   jax.ShapeDtypeStruct((n_tok, P_WIDTH), F32)],
        compiler_params=_params("parallel"),
        name="ffn_mix_in",
    )(x, g1, wgu, wo, gm, win)


def _mix_out_ffn_kernel(x1_ref, attn_ref, gdn_ref, wmix_ref, g2_ref, wgu_ref, wo_ref, gf_ref, y_ref):
    mixed = jnp.concatenate([attn_ref[...].astype(BF16), gdn_ref[...].astype(BF16)], axis=1)
    x2 = x1_ref[...] + jnp.dot(mixed, wmix_ref[...], preferred_element_type=F32)
    x3 = _ffn_half_step(x2, g2_ref[...], wgu_ref, wo_ref)
    y_ref[...] = _rms(x3, gf_ref[...])


def _mix_out_ffn(x1, attn, gdn, wmix, g2, wgu, wo, gf, tm):
    n_tok = x1.shape[0]
    row = lambda w: pl.BlockSpec((tm, w), lambda i: (i, 0))
    return pl.pallas_call(
        _mix_out_ffn_kernel,
        grid=(n_tok // tm,),
        in_specs=[row(D_MODEL), row(ATTN_Q_WIDTH), row(GDN_V_WIDTH),
                  _resident((ATTN_Q_WIDTH + GDN_V_WIDTH, D_MODEL)),
                  _resident((1, D_MODEL)), _resident((D_MODEL, 2 * D_FF)),
                  _resident((D_FF, D_MODEL)), _resident((1, D_MODEL))],
        out_specs=row(D_MODEL),
        out_shape=jax.ShapeDtypeStruct((n_tok, D_MODEL), F32),
        compiler_params=_params("parallel"),
        name="mix_out_ffn",
    )(x1, attn, gdn, wmix, g2, wgu, wo, gf)


def _alibi_slope(head):
    return 2.0 ** (-8.0 * (head + 1) / N_Q_HEADS)


def _swa_prompt_kernel(sink_ref, q_ref, kvc_ref, kvp_ref, o_ref, bias_ref, s_ref):
    blk = pl.program_id(1)
    nkeys = 2 * WINDOW

    @pl.when((pl.program_id(0) == 0) & (blk == 0))
    def _():
        key = lax.broadcasted_iota(jnp.int32, (nkeys, WINDOW), 0)
        qry = lax.broadcasted_iota(jnp.int32, (nkeys, WINDOW), 1)
        dist = qry - key + WINDOW
        valid = (dist >= 0) & (dist <= WINDOW)
        for h in range(N_Q_HEADS):
            g = h % GROUP
            bias_ref[h // GROUP, :, g * WINDOW:(g + 1) * WINDOW] = jnp.where(
                valid, -(LOG2E * _alibi_slope(h)) * dist.astype(F32), -jnp.inf)

    first_pen = jnp.where(blk > 0, 0.0, -jnp.inf)
    scale = LOG2E * HEAD_DIM ** -0.5
    nsub = q_ref.shape[0] // WINDOW
    jobs = [(j, kvh) for j in range(nsub) for kvh in range(N_KV_HEADS)]
    group_heads = lambda kvh: range(kvh * GROUP, (kvh + 1) * GROUP)
    def keys_values(j, col):
        if j == 0:
            return jnp.concatenate([kvp_ref[:, col], kvc_ref[0:WINDOW, col]], axis=0)
        return kvc_ref[(j - 1) * WINDOW:(j + 1) * WINDOW, col]

    values_t = []
    for job, (j, kvh) in enumerate(jobs):
        rows = slice(j * WINDOW, (j + 1) * WINDOW)
        k_all = keys_values(j, slice(kvh * HEAD_DIM, (kvh + 1) * HEAD_DIM)).astype(BF16)
        v_t = keys_values(j, slice(KV_WIDTH + kvh * HEAD_DIM, KV_WIDTH + (kvh + 1) * HEAD_DIM)).T
        values_t.append(jnp.concatenate([v_t, jnp.ones((SUBLANES, 2 * WINDOW), F32)],
                                        axis=0).astype(BF16))
        q_stack = jnp.concatenate(
            [(q_ref[rows, h * HEAD_DIM:(h + 1) * HEAD_DIM] * scale).astype(BF16)
             for h in group_heads(kvh)], axis=0)
        s_ref[job] = lax.dot_general(k_all, q_stack, (((1,), (1,)), ((), ())),
                                     preferred_element_type=F32)
    for job, (j, kvh) in enumerate(jobs):
        rows = slice(j * WINDOW, (j + 1) * WINDOW)
        v_t = values_t[job]
        outs = []
        for g, h in enumerate(group_heads(kvh)):
            lanes = slice(g * WINDOW, (g + 1) * WINDOW)
            s = s_ref[job, :, lanes] + bias_ref[kvh, :, lanes]
            if j == 0:
                s = jnp.concatenate([s[:WINDOW] + first_pen, s[WINDOW:]], axis=0)
            sink = LOG2E * sink_ref[h]
            m = jnp.maximum(jnp.max(s, axis=0, keepdims=True), sink)
            p = jnp.exp2(s - m).astype(BF16)
            pv = jnp.dot(v_t, p, preferred_element_type=F32)
            denom = pv[HEAD_DIM:HEAD_DIM + 1] + jnp.exp2(sink - m)
            outs.append(pv[:HEAD_DIM] / denom)
        for pair in range(GROUP // 2):
            lo = (kvh * GROUP + 2 * pair) * HEAD_DIM
            o_ref[rows, lo:lo + 2 * HEAD_DIM] = jnp.concatenate(outs[2 * pair:2 * pair + 2], axis=0).T


def _swa_prompt(sinks, p3):
    batch, seq, _ = p3.shape
    tq = SWA_TILE
    nsub = tq // WINDOW
    q_blk = P_AQ // ATTN_Q_WIDTH
    kv_blk = P_KV // (2 * KV_WIDTH)
    return pl.pallas_call(
        _swa_prompt_kernel,
        grid=(batch, seq // tq),
        in_specs=[pl.BlockSpec(memory_space=pltpu.SMEM),
                  pl.BlockSpec((None, tq, ATTN_Q_WIDTH), lambda b, i: (b, i, q_blk)),
                  pl.BlockSpec((None, tq, 2 * KV_WIDTH), lambda b, i: (b, i, kv_blk)),
                  pl.BlockSpec((None, WINDOW, 2 * KV_WIDTH),
                               lambda b, i: (b, jnp.maximum(i * nsub - 1, 0), kv_blk))],
        out_specs=pl.BlockSpec((None, tq, ATTN_Q_WIDTH), lambda b, i: (b, i, 0)),
        out_shape=jax.ShapeDtypeStruct((batch, seq, ATTN_Q_WIDTH), F32),
        scratch_shapes=[pltpu.VMEM((N_KV_HEADS, 2 * WINDOW, GROUP * WINDOW), F32),
                        pltpu.VMEM((nsub * N_KV_HEADS, 2 * WINDOW, GROUP * WINDOW), F32)],
        compiler_params=_params("arbitrary", "arbitrary"),
        name="swa_prompt",
    )(sinks, p3, p3, p3)


INV_LEVELS = tuple(2 ** i for i in range(1, CHUNK.bit_length()))


def _gdn_prompt_kernel(raw_ref, prev_ref, gz_ref, gba_ref, convw_ref, alog_ref, dtb_ref, ng_ref,
                       o_ref, s_ref, mask_ref, act_ref):
    t = pl.program_id(1)
    tt = GDN_TILE
    nchunk = tt // CHUNK
    heads = range(N_GDN_HEADS)

    r = lax.broadcasted_iota(jnp.int32, (tt, tt), 0)
    c = lax.broadcasted_iota(jnp.int32, (tt, tt), 1)
    xor_idx = r ^ c

    @pl.when((pl.program_id(0) == 0) & (t == 0))
    def _():
        mask_ref[0] = jnp.where((xor_idx < CHUNK) & (r >= c), 1.0, 0.0).astype(BF16)
        for i, s in enumerate(INV_LEVELS):
            mask_ref[i + 1] = jnp.where((xor_idx >= s // 2) & (xor_idx < s) & (r > c),
                                        1.0, 0.0).astype(BF16)

    @pl.when(t == 0)
    def _():
        s_ref[...] = jnp.zeros(s_ref.shape, F32)

    strict = (xor_idx < CHUNK) & (r > c)
    diag = r == c

    nseq = raw_ref.shape[0]
    chains = [(sq, h) for sq in range(nseq) for h in heads]
    ids = range(len(chains))

    for sq in range(nseq):
        for cg in range(GDN_CONV_DIM // LANES):
            cols = slice(cg * LANES, (cg + 1) * LANES)
            w = convw_ref[:, cols]
            for r0 in range(0, tt, CONV_ROWS):
                if r0 == 0:
                    before = jnp.where(t > 0, prev_ref[sq, :, cols], 0.0)
                else:
                    before = raw_ref[sq, r0 - SUBLANES:r0, cols]
                cur = raw_ref[sq, r0:r0 + CONV_ROWS, cols]
                ext = jnp.concatenate([before, cur], axis=0)
                acc = w[0:1] * ext
                for j in range(1, CONV_WIDTH):
                    acc = pltpu.roll(acc, 1, axis=0) + w[j:j + 1] * ext
                act = _silu(acc[SUBLANES:])
                if cg < GDN_QK_WIDTH // LANES:
                    act = _l2norm(act) * (GDN_DK ** -0.5)
                elif cg < 2 * GDN_QK_WIDTH // LANES:
                    act = _l2norm(act)
                act_ref[sq, r0:r0 + CONV_ROWS, cols] = act

    gates, gc, gc_t, gtot = [], [], [], []
    for sq in range(nseq):
        beta_all, g_all = _gdn_gates(gba_ref[sq], alog_ref[...], dtb_ref[...])
        g_hi = g_all.astype(BF16)
        g_lo = (g_all - g_hi.astype(F32)).astype(BF16)
        cs = jnp.dot(mask_ref[0], jnp.concatenate([g_hi, g_lo], axis=1),
                     preferred_element_type=F32)
        gc_sq = (cs[:, :LANES] + cs[:, LANES:]) * LOG2E
        gates.append(beta_all)
        gc.append(gc_sq)
        gc_t.append(gc_sq.T)
        gtot.append(jnp.concatenate(
            [jnp.broadcast_to(gc_sq[(ci + 1) * CHUNK - 1:(ci + 1) * CHUNK, :], (CHUNK, LANES))
             for ci in range(nchunk)], axis=0))

    q_l, k_l, v_l, beta_l, gcol_l, gend_l, a_l, qk_l, x_l = [], [], [], [], [], [], [], [], []
    for sq, h in chains:
        q = act_ref[sq, :, h * GDN_DK:(h + 1) * GDN_DK]
        k = act_ref[sq, :, GDN_QK_WIDTH + h * GDN_DK:GDN_QK_WIDTH + (h + 1) * GDN_DK]
        beta = gates[sq][:, h:h + 1]
        gcol = gc[sq][:, N_GDN_HEADS + h:N_GDN_HEADS + h + 1]
        grow = gc_t[sq][N_GDN_HEADS + h:N_GDN_HEADS + h + 1, :]
        decay = jnp.exp2(jnp.where(strict, gcol - grow, -jnp.inf))
        kb = k.astype(BF16)
        qk_kk = _dot_nt(jnp.concatenate([q.astype(BF16), kb], axis=0), kb)
        a = ((beta * qk_kk[tt:]) * decay).astype(BF16)
        qk_l.append((qk_kk[:tt] * jnp.where(diag, 1.0, decay)).astype(BF16))
        x_l.append(jnp.where(diag, 1.0, 0.0).astype(BF16) - a * mask_ref[1])
        q_l.append(q); k_l.append(k); beta_l.append(beta); gcol_l.append(gcol); a_l.append(a)
        gend_l.append(gtot[sq][:, N_GDN_HEADS + h:N_GDN_HEADS + h + 1])
        v_l.append(act_ref[sq, :, 2 * GDN_QK_WIDTH + h * GDN_DV:2 * GDN_QK_WIDTH + (h + 1) * GDN_DV])

    for i in range(1, len(INV_LEVELS)):
        for n in ids:
            x = x_l[n]
            y = jnp.dot(x, a_l[n], preferred_element_type=F32).astype(BF16)
            z = jnp.dot(y, x, preferred_element_type=F32).astype(BF16)
            x_l[n] = x - z * mask_ref[i + 1]

    u_l, w_l, qd_l, kd_l = [], [], [], []
    for n in ids:
        eg = jnp.exp2(gcol_l[n])
        rhs = jnp.concatenate([v_l[n] * beta_l[n], k_l[n] * (beta_l[n] * eg)], axis=1)
        uw = jnp.dot(x_l[n], rhs.astype(BF16), preferred_element_type=F32)
        u_l.append(uw[:, :GDN_DV])
        w_l.append(uw[:, GDN_DV:].astype(BF16))
        qd_l.append((q_l[n] * eg).astype(BF16))
        kd_l.append((k_l[n] * jnp.exp2(gend_l[n] - gcol_l[n])).astype(BF16))

    states = [s_ref[sq, h] for sq, h in chains]
    v_new = [[] for _ in ids]
    o_inter = [[] for _ in ids]
    for ci in range(nchunk):
        rows = slice(ci * CHUNK, (ci + 1) * CHUNK)
        for n in ids:
            sb = states[n].astype(BF16)
            ws_qs = jnp.dot(jnp.concatenate([w_l[n][rows], qd_l[n][rows]], axis=0), sb,
                            preferred_element_type=F32)
            vn = u_l[n][rows] - ws_qs[:CHUNK]
            o_inter[n].append(ws_qs[CHUNK:])
            g_last = jnp.exp2(gend_l[n][ci * CHUNK:ci * CHUNK + 1, :])
            upd = lax.dot_general(kd_l[n][rows], vn.astype(BF16), (((0,), (0,)), ((), ())),
                                  preferred_element_type=F32)
            states[n] = states[n] * g_last + upd
            v_new[n].append(vn)

    for n, (sq, h) in enumerate(chains):
        s_ref[sq, h] = states[n]
        vn_all = jnp.concatenate(v_new[n], axis=0).astype(BF16)
        o = jnp.concatenate(o_inter[n], axis=0) + jnp.dot(qk_l[n], vn_all,
                                                           preferred_element_type=F32)
        gz = gz_ref[sq, :, h * GDN_DV:(h + 1) * GDN_DV]
        o_ref[sq, :, h * GDN_DV:(h + 1) * GDN_DV] = _rms(o, ng_ref[...]) * _silu(gz)


def _gdn_prompt(p3, convw, alog, dtb, ng):
    batch, seq, _ = p3.shape
    tt = GDN_TILE
    ns = GDN_SEQS if batch % GDN_SEQS == 0 else 1
    prev_per_tile = tt // SUBLANES
    small = lambda shape: pl.BlockSpec(shape, lambda b, t: (0,) * len(shape))
    return pl.pallas_call(
        _gdn_prompt_kernel,
        grid=(batch // ns, seq // tt),
        in_specs=[pl.BlockSpec((ns, tt, GDN_CONV_DIM), lambda b, t: (b, t, P_GQKV // GDN_CONV_DIM)),
                  pl.BlockSpec((ns, SUBLANES, GDN_CONV_DIM),
                               lambda b, t: (b, jnp.maximum(t * prev_per_tile - 1, 0),
                                             P_GQKV // GDN_CONV_DIM)),
                  pl.BlockSpec((ns, tt, GDN_V_WIDTH), lambda b, t: (b, t, P_GZ // GDN_V_WIDTH)),
                  pl.BlockSpec((ns, tt, LANES), lambda b, t: (b, t, P_GBA // LANES)),
                  small((CONV_WIDTH, GDN_CONV_DIM)), small((1, LANES)), small((1, LANES)),
                  small((1, GDN_DV))],
        out_specs=[pl.BlockSpec((ns, tt, GDN_V_WIDTH), lambda b, t: (b, t, 0)),
                   pl.BlockSpec((ns, N_GDN_HEADS, GDN_DK, GDN_DV), lambda b, t: (b, 0, 0, 0))],
        out_shape=[jax.ShapeDtypeStruct((batch, seq, GDN_V_WIDTH), F32),
                   jax.ShapeDtypeStruct((batch, N_GDN_HEADS, GDN_DK, GDN_DV), F32)],
        scratch_shapes=[pltpu.VMEM((len(INV_LEVELS) + 1, tt, tt), BF16),
                        pltpu.VMEM((ns, tt, GDN_CONV_DIM), F32)],
        compiler_params=_params("arbitrary", "arbitrary"),
        name="gdn_prompt",
    )(p3, p3, p3, p3, convw, alog, dtb, ng)


def _sample_prep_kernel(p_ref, hist_ref, convw_ref, alog_ref, dtb_ref,
                        conv_ref, qt_ref, kt_ref, v_ref, beta_ref, decay_ref, kvt_ref):
    nblk = qt_ref.shape[0]
    kv_t = p_ref[:, P_KV:P_KV + 2 * KV_WIDTH].T
    for i in range(nblk):
        kvt_ref[i] = kv_t[:, i * SAMPLE_BLOCK:(i + 1) * SAMPLE_BLOCK]
    raw = p_ref[:, P_GQKV:P_GQKV + GDN_CONV_DIM]
    w = convw_ref[...]
    conv = w[0:1] * hist_ref[:, 0:GDN_CONV_DIM]
    for j in range(1, CONV_WIDTH - 1):
        conv = conv + w[j:j + 1] * hist_ref[:, j * GDN_CONV_DIM:(j + 1) * GDN_CONV_DIM]
    conv = conv + w[CONV_WIDTH - 1:CONV_WIDTH] * raw
    act = _silu(conv)
    conv_ref[:, 0:(CONV_WIDTH - 2) * GDN_CONV_DIM] = hist_ref[:, GDN_CONV_DIM:]
    conv_ref[:, (CONV_WIDTH - 2) * GDN_CONV_DIM:] = raw
    v_ref[...] = act[:, 2 * GDN_QK_WIDTH:]
    beta, g = _gdn_gates(p_ref[:, P_GBA:P_GBA + LANES], alog_ref[...], dtb_ref[...])
    beta_ref[...] = beta
    decay_ref[...] = jnp.exp(g)
    for h in range(N_GDN_HEADS):
        qt = (_l2norm(act[:, h * GDN_DK:(h + 1) * GDN_DK]) * (GDN_DK ** -0.5)).T
        kt = _l2norm(act[:, GDN_QK_WIDTH + h * GDN_DK:GDN_QK_WIDTH + (h + 1) * GDN_DK]).T
        for i in range(nblk):
            qt_ref[i, h] = qt[:, i * SAMPLE_BLOCK:(i + 1) * SAMPLE_BLOCK]
            kt_ref[i, h] = kt[:, i * SAMPLE_BLOCK:(i + 1) * SAMPLE_BLOCK]


def _sample_prep(p, hist, convw, alog, dtb):
    nseq = p.shape[0]
    nblk = nseq // SAMPLE_BLOCK
    hist_w = (CONV_WIDTH - 1) * GDN_CONV_DIM
    cols = jax.ShapeDtypeStruct((nblk, N_GDN_HEADS, GDN_DK, SAMPLE_BLOCK), F32)
    return pl.pallas_call(
        _sample_prep_kernel,
        out_shape=[jax.ShapeDtypeStruct((nseq, hist_w), F32), cols, cols,
                   jax.ShapeDtypeStruct((nseq, GDN_V_WIDTH), F32),
                   jax.ShapeDtypeStruct((nseq, LANES), F32),
                   jax.ShapeDtypeStruct((nseq, LANES), F32),
                   jax.ShapeDtypeStruct((nblk, 2 * KV_WIDTH, SAMPLE_BLOCK), F32)],
        compiler_params=pltpu.CompilerParams(vmem_limit_bytes=VMEM_LIMIT),
        name="sample_prep",
    )(p, hist, convw, alog, dtb)


def _per_head(values):
    h = lax.broadcasted_iota(jnp.int32, (N_Q_HEADS, 1), 0)
    col = jnp.full((N_Q_HEADS, 1), values[N_Q_HEADS - 1], F32)
    for i in range(N_Q_HEADS - 2, -1, -1):
        col = jnp.where(h == i, values[i], col)
    return col


def _sample_mix_kernel(sink_ref, q_ref, p_ref, kc_ref, vc_ref, kvt_ref, qt_ref, kt_ref, v_ref,
                       beta_ref, decay_ref, s_ref, ng_ref, attn_ref, gdn_ref, nk_ref, nv_ref, ns_ref):
    time = lax.broadcasted_iota(jnp.int32, (KV_WIDTH, WINDOW), 1)
    newest = time == WINDOW - 1
    key_pos = lax.broadcasted_iota(jnp.int32, (1, WINDOW), 1)
    dist_hist = (WINDOW - key_pos).astype(F32)
    head = lax.broadcasted_iota(jnp.int32, (N_Q_HEADS, HEAD_DIM), 0)
    kv_of_head = [head // GROUP == kvh for kvh in range(N_KV_HEADS)]
    slope = _per_head([_alibi_slope(h) for h in range(N_Q_HEADS)])
    sink = _per_head([sink_ref[h] for h in range(N_Q_HEADS)])
    scale = HEAD_DIM ** -0.5
    for b in range(SAMPLE_BLOCK):
        k_hist = kc_ref[b]
        v_hist = vc_ref[b]
        k_new = p_ref[b:b + 1, P_KV:P_KV + KV_WIDTH]
        v_new = p_ref[b:b + 1, P_KV + KV_WIDTH:P_KV + 2 * KV_WIDTH]
        nk_ref[b] = jnp.where(newest, kvt_ref[0:KV_WIDTH, b:b + 1],
                              pltpu.roll(k_hist, WINDOW - 1, axis=1))
        nv_ref[b] = jnp.where(newest, kvt_ref[KV_WIDTH:2 * KV_WIDTH, b:b + 1],
                              pltpu.roll(v_hist, WINDOW - 1, axis=1))
        q = q_ref[b]
        q_wide = jnp.concatenate([jnp.where(sel, q, 0.0) for sel in kv_of_head], axis=1)
        s_hist = _dot(q_wide, k_hist) * scale - slope * dist_hist
        s_new = jnp.sum(q_wide * k_new, axis=-1, keepdims=True) * scale
        m = jnp.maximum(jnp.maximum(jnp.max(s_hist, axis=-1, keepdims=True), s_new), sink)
        p_hist = jnp.exp(s_hist - m)
        p_new = jnp.exp(s_new - m)
        denom = jnp.sum(p_hist, axis=-1, keepdims=True) + p_new + jnp.exp(sink - m)
        o_wide = (_dot_nt(p_hist, v_hist) + p_new * v_new) / denom
        o = o_wide[:, 0:HEAD_DIM]
        for kvh in range(1, N_KV_HEADS):
            o = jnp.where(kv_of_head[kvh], o_wide[:, kvh * HEAD_DIM:(kvh + 1) * HEAD_DIM], o)
        attn_ref[b] = o

    heads = range(N_GDN_HEADS)
    for b in range(SAMPLE_BLOCK):
        kcol = [jnp.broadcast_to(kt_ref[h, :, b:b + 1], (GDN_DK, GDN_DV)) for h in heads]
        qcol = [jnp.broadcast_to(qt_ref[h, :, b:b + 1], (GDN_DK, GDN_DV)) for h in heads]
        decayed = [s_ref[b, h] * decay_ref[b:b + 1, N_GDN_HEADS + h:N_GDN_HEADS + h + 1]
                   for h in heads]
        ks = [jnp.sum(decayed[h] * kcol[h], axis=0, keepdims=True) for h in heads]
        delta = [beta_ref[b:b + 1, h:h + 1] * (v_ref[b:b + 1, h * GDN_DV:(h + 1) * GDN_DV] - ks[h])
                 for h in heads]
        state = [decayed[h] + kcol[h] * delta[h] for h in heads]
        outs = []
        for h in heads:
            ns_ref[b, h] = state[h]
            o = jnp.sum(state[h] * qcol[h], axis=0, keepdims=True)
            gz = p_ref[b:b + 1, P_GZ + h * GDN_DV:P_GZ + (h + 1) * GDN_DV]
            outs.append(_rms(o, ng_ref[...]) * _silu(gz))
        gdn_ref[b:b + 1, :] = jnp.concatenate(outs, axis=1)


def _sample_mix(sinks, q, p, k_hist, v_hist, kv_t, qt, kt, v, beta, decay, state, ng):
    nseq = p.shape[0]
    bb = SAMPLE_BLOCK
    rows = lambda w: pl.BlockSpec((bb, w), lambda i: (i, 0))
    heads = pl.BlockSpec((bb, N_Q_HEADS, HEAD_DIM), lambda i: (i, 0, 0))
    cache = pl.BlockSpec((bb, KV_WIDTH, WINDOW), lambda i: (i, 0, 0))
    cols = pl.BlockSpec((None, N_GDN_HEADS, GDN_DK, bb), lambda i: (i, 0, 0, 0))
    kv_cols = pl.BlockSpec((None, 2 * KV_WIDTH, bb), lambda i: (i, 0, 0))
    st = pl.BlockSpec((bb, N_GDN_HEADS, GDN_DK, GDN_DV), lambda i: (i, 0, 0, 0))
    return pl.pallas_call(
        _sample_mix_kernel,
        grid=(nseq // bb,),
        in_specs=[pl.BlockSpec(memory_space=pltpu.SMEM), heads, rows(P_WIDTH), cache, cache,
                  kv_cols, cols, cols, rows(GDN_V_WIDTH), rows(LANES), rows(LANES), st,
                  pl.BlockSpec((1, GDN_DV), lambda i: (0, 0))],
        out_specs=[heads, rows(GDN_V_WIDTH), cache, cache, st],
        out_shape=[jax.ShapeDtypeStruct((nseq, N_Q_HEADS, HEAD_DIM), F32),
                   jax.ShapeDtypeStruct((nseq, GDN_V_WIDTH), F32),
                   jax.ShapeDtypeStruct(k_hist.shape, F32),
                   jax.ShapeDtypeStruct(v_hist.shape, F32),
                   jax.ShapeDtypeStruct(state.shape, F32)],
        compiler_params=_params("parallel"),
        name="sample_mix",
    )(sinks, q, p, k_hist, v_hist, kv_t, qt, kt, v, beta, decay, state, ng)


def _lane_pad(vec, offset):
    return jnp.zeros((1, LANES), F32).at[0, offset:offset + vec.shape[0]].set(vec)


def kernel(x_prompt, x_sample, cache_attn_k, cache_attn_v, state_conv, state_gdn, ffn1_norm_g,
           ffn1_w_in, ffn1_w_out, mix_norm_g, w_in_mix, attn_sinks, conv_w, gdn_A_log, gdn_dt_bias,
           gdn_norm_g, w_out_mix, ffn2_norm_g, ffn2_w_in, ffn2_w_out, final_norm_g):
    depth = ffn1_w_in.shape[0]
    assert depth == 1, "single-layer trunk"
    batch, seq, _ = x_prompt.shape
    nseq = x_sample.shape[0]
    assert x_sample.shape[1] == 1 and cache_attn_k.shape[2] == WINDOW
    assert seq % GDN_TILE == 0 and seq % WINDOW == 0 and nseq % SAMPLE_BLOCK == 0
    l = 0
    row = lambda v: v.reshape(1, -1)
    g1, gm, g2, gf = row(ffn1_norm_g[l]), row(mix_norm_g[l]), row(ffn2_norm_g[l]), row(final_norm_g)
    ng = row(gdn_norm_g[l])
    wgu1, wo1 = ffn1_w_in[l].astype(BF16), ffn1_w_out[l].astype(BF16)
    wgu2, wo2 = ffn2_w_in[l].astype(BF16), ffn2_w_out[l].astype(BF16)
    win = jnp.pad(w_in_mix[l].astype(BF16), ((0, 0), (0, P_WIDTH - w_in_mix.shape[2])))
    wmix = w_out_mix[l].astype(BF16)
    alog = _lane_pad(gdn_A_log[l], N_GDN_HEADS)
    dtb = _lane_pad(gdn_dt_bias[l], N_GDN_HEADS)
    sinks = attn_sinks[l]
    convw = conv_w[l]

    xp = x_prompt.reshape(batch * seq, D_MODEL)
    x1p, pp = _ffn_mix_in(xp, g1, wgu1, wo1, gm, win, TOKEN_TILE)
    pp3 = pp.reshape(batch, seq, P_WIDTH)
    attn_p = _swa_prompt(sinks, pp3)
    gdn_p, state_p = _gdn_prompt(pp3, convw, alog, dtb, ng)
    y_p = _mix_out_ffn(x1p, attn_p.reshape(batch * seq, ATTN_Q_WIDTH),
                       gdn_p.reshape(batch * seq, GDN_V_WIDTH), wmix, g2, wgu2, wo2, gf, TOKEN_TILE)
    tail = pp3[:, seq - WINDOW:, P_KV:P_KV + 2 * KV_WIDTH]
    new_k_p = tail[:, :, :KV_WIDTH].reshape(1, batch, WINDOW, N_KV_HEADS, HEAD_DIM)
    new_v_p = tail[:, :, KV_WIDTH:].reshape(1, batch, WINDOW, N_KV_HEADS, HEAD_DIM)
    new_conv_p = pp3[:, seq - (CONV_WIDTH - 1):, P_GQKV:P_GQKV + GDN_CONV_DIM][None]

    xs = x_sample.reshape(nseq, D_MODEL)
    x1s, ps = _ffn_mix_in(xs, g1, wgu1, wo1, gm, win, nseq)
    hist = state_conv[l].reshape(nseq, (CONV_WIDTH - 1) * GDN_CONV_DIM)
    new_conv_s, qt, kt, v_s, beta_s, decay_s, kv_t = _sample_prep(ps, hist, convw, alog, dtb)
    time_minor = lambda c: jnp.transpose(c.reshape(nseq, WINDOW, KV_WIDTH), (0, 2, 1))
    time_major = lambda c: jnp.transpose(c, (0, 2, 1)).reshape(1, nseq, WINDOW, N_KV_HEADS, HEAD_DIM)
    q_s = ps[:, P_AQ:P_AQ + ATTN_Q_WIDTH].reshape(nseq, N_Q_HEADS, HEAD_DIM)
    attn_s, gdn_s, new_k_s, new_v_s, state_s = _sample_mix(
        sinks, q_s, ps, time_minor(cache_attn_k[l]), time_minor(cache_attn_v[l]), kv_t, qt, kt,
        v_s, beta_s, decay_s, state_gdn[l], ng)
    y_s = _mix_out_ffn(x1s, attn_s.reshape(nseq, ATTN_Q_WIDTH), gdn_s, wmix, g2, wgu2, wo2, gf,
                       nseq)

    return (y_p.reshape(batch, seq, D_MODEL), y_s.reshape(nseq, 1, D_MODEL),
            new_k_p, new_v_p, new_conv_p, state_p[None],
            time_major(new_k_s), time_major(new_v_s),
            new_conv_s.reshape(1, nseq, CONV_WIDTH - 1, GDN_CONV_DIM), state_s[None])
```

```python
import jax
import jax.numpy as jnp
from jax import lax
from jax.experimental import pallas as pl
from jax.experimental.pallas import tpu as pltpu

F32 = jnp.float32
BF16 = jnp.bfloat16

D_MODEL = 1024
D_FF = 2816
N_Q_HEADS = 8
N_KV_HEADS = 2
GROUP = N_Q_HEADS // N_KV_HEADS
HEAD_DIM = 64
WINDOW = 128
ATTN_Q_WIDTH = N_Q_HEADS * HEAD_DIM
KV_WIDTH = N_KV_HEADS * HEAD_DIM
N_GDN_HEADS = 4
GDN_DK = 128
GDN_DV = 128
GDN_QK_WIDTH = N_GDN_HEADS * GDN_DK
GDN_V_WIDTH = N_GDN_HEADS * GDN_DV
GDN_CONV_DIM = 2 * GDN_QK_WIDTH + GDN_V_WIDTH
CONV_WIDTH = 4
CHUNK = 64
EPS = 1e-6
LOG2E = 1.4426950408889634

LANES = 128
SUBLANES = 8
VMEM_LIMIT = 56 * 1024 * 1024

P_GQKV = 0
P_AQ = P_GQKV + GDN_CONV_DIM
P_GZ = P_AQ + ATTN_Q_WIDTH
P_KV = P_GZ + GDN_V_WIDTH
P_GBA = P_KV + 2 * KV_WIDTH
P_WIDTH = P_GBA + LANES

FF_CHUNK = 256
TOKEN_TILE = 512
GDN_TILE = 128
GDN_SEQS = 4
SWA_TILE = 1024
CONV_ROWS = 128
SAMPLE_BLOCK = 16


def _sigmoid(x):
    return 1.0 / (1.0 + jnp.exp(-x))


def _silu(x):
    half = 0.5 * x
    return half + half * jnp.tanh(half)


def _softplus(x):
    return jnp.maximum(x, 0.0) + jnp.log1p(jnp.exp(-jnp.abs(x)))


def _rms(x, g):
    return x * lax.rsqrt(jnp.mean(x * x, axis=-1, keepdims=True) + EPS) * g


def _dot(a, b):
    return jnp.dot(a.astype(BF16), b.astype(BF16), preferred_element_type=F32)


def _dot_nt(a, b):
    return lax.dot_general(a.astype(BF16), b.astype(BF16), (((1,), (1,)), ((), ())),
                           preferred_element_type=F32)


def _resident(shape):
    return pl.BlockSpec(shape, lambda *_: (0,) * len(shape), pipeline_mode=pl.Buffered(1))


def _params(*semantics):
    return pltpu.CompilerParams(dimension_semantics=semantics, vmem_limit_bytes=VMEM_LIMIT)


FF_STEPS = D_FF // FF_CHUNK


def _ffn_half_step(x, g, wgu_ref, wo_ref):
    n = _rms(x, g).astype(BF16)
    hidden = []
    for c in range(FF_STEPS):
        lo = c * FF_CHUNK
        gate = jnp.dot(n, wgu_ref[:, lo:lo + FF_CHUNK], preferred_element_type=F32)
        up = jnp.dot(n, wgu_ref[:, D_FF + lo:D_FF + lo + FF_CHUNK], preferred_element_type=F32)
        hidden.append((_silu(gate) * up).astype(BF16))
    out = jnp.dot(jnp.concatenate(hidden, axis=1), wo_ref[...], preferred_element_type=F32)
    return x + 0.5 * out


_W_AQ, _W_KV = 0, ATTN_Q_WIDTH
_W_GQKV = _W_KV + 2 * KV_WIDTH
_W_GZ = _W_GQKV + GDN_CONV_DIM
_W_GBA = _W_GZ + GDN_V_WIDTH
_PROJ_GROUPS = ((_W_GQKV, GDN_CONV_DIM, P_GQKV), (_W_GBA, LANES, P_GBA), (_W_AQ, ATTN_Q_WIDTH, P_AQ),
                (_W_GZ, GDN_V_WIDTH, P_GZ), (_W_KV, 2 * KV_WIDTH, P_KV))


def _gdn_gates(gba, alog, dtb):
    beta = _sigmoid(gba)
    g = -jnp.exp(alog) * _softplus(gba + dtb)
    return beta, g


def _l2norm(x):
    return x * lax.rsqrt(jnp.sum(x * x, axis=-1, keepdims=True) + EPS)


def _ffn_mix_in_kernel(x_ref, g1_ref, wgu_ref, wo_ref, gm_ref, win_ref, x1_ref, p_ref):
    x1 = _ffn_half_step(x_ref[...], g1_ref[...], wgu_ref, wo_ref)
    x1_ref[...] = x1
    n = _rms(x1, gm_ref[...]).astype(BF16)
    for src, width, dst in _PROJ_GROUPS:
        p_ref[:, dst:dst + width] = jnp.dot(n, win_ref[:, src:src + width],
                                            preferred_element_type=F32)


def _ffn_weight_specs():
    return [_resident((1, D_MODEL)), _resident((D_MODEL, 2 * D_FF)), _resident((D_FF, D_MODEL)),
            _resident((1, D_MODEL)), _resident((D_MODEL, P_WIDTH))]


def _ffn_mix_in(x, g1, wgu, wo, gm, win, tm):
    n_tok = x.shape[0]
    row = lambda w: pl.BlockSpec((tm, w), lambda i: (i, 0))
    return pl.pallas_call(
        _ffn_mix_in_kernel,
        grid=(n_tok // tm,),
        in_specs=[row(D_MODEL)] + _ffn_weight_specs(),
        out_specs=[row(D_MODEL), row(P_WIDTH)],
        out_shape=[jax.ShapeDtypeStruct((n_tok, D_MODEL), F32),
                   jax.ShapeDtypeStruct((n_tok, P_WIDTH), F32)],
        compiler_params=_params("parallel"),
        name="ffn_mix_in",
    )(x, g1, wgu, wo, gm, win)


def _mix_out_ffn_kernel(x1_ref, attn_ref, gdn_ref, wmix_ref, g2_ref, wgu_ref, wo_ref, gf_ref, y_ref):
    mixed = jnp.concatenate([attn_ref[...].astype(BF16), gdn_ref[...].astype(BF16)], axis=1)
    x2 = x1_ref[...] + jnp.dot(mixed, wmix_ref[...], preferred_element_type=F32)
    x3 = _ffn_half_step(x2, g2_ref[...], wgu_ref, wo_ref)
    y_ref[...] = _rms(x3, gf_ref[...])


def _mix_out_ffn(x1, attn, gdn, wmix, g2, wgu, wo, gf, tm):
    n_tok = x1.shape[0]
    row = lambda w: pl.BlockSpec((tm, w), lambda i: (i, 0))
    return pl.pallas_call(
        _mix_out_ffn_kernel,
        grid=(n_tok // tm,),
        in_specs=[row(D_MODEL), row(ATTN_Q_WIDTH), row(GDN_V_WIDTH),
                  _resident((ATTN_Q_WIDTH + GDN_V_WIDTH, D_MODEL)),
                  _resident((1, D_MODEL)), _resident((D_MODEL, 2 * D_FF)),
                  _resident((D_FF, D_MODEL)), _resident((1, D_MODEL))],
        out_specs=row(D_MODEL),
        out_shape=jax.ShapeDtypeStruct((n_tok, D_MODEL), F32),
        compiler_params=_params("parallel"),
        name="mix_out_ffn",
    )(x1, attn, gdn, wmix, g2, wgu, wo, gf)


def _alibi_slope(head):
    return 2.0 ** (-8.0 * (head + 1) / N_Q_HEADS)


def _swa_prompt_kernel(sink_ref, q_ref, kvc_ref, kvp_ref, o_ref, bias_ref, s_ref):
    blk = pl.program_id(1)
    nkeys = 2 * WINDOW

    @pl.when((pl.program_id(0) == 0) & (blk == 0))
    def _():
        key = lax.broadcasted_iota(jnp.int32, (nkeys, WINDOW), 0)
        qry = lax.broadcasted_iota(jnp.int32, (nkeys, WINDOW), 1)
        dist = qry - key + WINDOW
        valid = (dist >= 0) & (dist <= WINDOW)
        for h in range(N_Q_HEADS):
            g = h % GROUP
            bias_ref[h // GROUP, :, g * WINDOW:(g + 1) * WINDOW] = jnp.where(
                valid, -(LOG2E * _alibi_slope(h)) * dist.astype(F32), -jnp.inf)

    first_pen = jnp.where(blk > 0, 0.0, -jnp.inf)
    scale = LOG2E * HEAD_DIM ** -0.5
    nsub = q_ref.shape[0] // WINDOW
    jobs = [(j, kvh) for j in range(nsub) for kvh in range(N_KV_HEADS)]
    group_heads = lambda kvh: range(kvh * GROUP, (kvh + 1) * GROUP)
    def keys_values(j, col):
        if j == 0:
            return jnp.concatenate([kvp_ref[:, col], kvc_ref[0:WINDOW, col]], axis=0)
        return kvc_ref[(j - 1) * WINDOW:(j + 1) * WINDOW, col]

    values_t = []
    for job, (j, kvh) in enumerate(jobs):
        rows = slice(j * WINDOW, (j + 1) * WINDOW)
        k_all = keys_values(j, slice(kvh * HEAD_DIM, (kvh + 1) * HEAD_DIM)).astype(BF16)
        v_t = keys_values(j, slice(KV_WIDTH + kvh * HEAD_DIM, KV_WIDTH + (kvh + 1) * HEAD_DIM)).T
        values_t.append(jnp.concatenate([v_t, jnp.ones((SUBLANES, 2 * WINDOW), F32)],
                                        axis=0).astype(BF16))
        q_stack = jnp.concatenate(
            [(q_ref[rows, h * HEAD_DIM:(h + 1) * HEAD_DIM] * scale).astype(BF16)
             for h in group_heads(kvh)], axis=0)
        s_ref[job] = lax.dot_general(k_all, q_stack, (((1,), (1,)), ((), ())),
                                     preferred_element_type=F32)
    for job, (j, kvh) in enumerate(jobs):
        rows = slice(j * WINDOW, (j + 1) * WINDOW)
        v_t = values_t[job]
        outs = []
        for g, h in enumerate(group_heads(kvh)):
            lanes = slice(g * WINDOW, (g + 1) * WINDOW)
            s = s_ref[job, :, lanes] + bias_ref[kvh, :, lanes]
            if j == 0:
                s = jnp.concatenate([s[:WINDOW] + first_pen, s[WINDOW:]], axis=0)
            sink = LOG2E * sink_ref[h]
            m = jnp.maximum(jnp.max(s, axis=0, keepdims=True), sink)
            p = jnp.exp2(s - m).astype(BF16)
            pv = jnp.dot(v_t, p, preferred_element_type=F32)
            denom = pv[HEAD_DIM:HEAD_DIM + 1] + jnp.exp2(sink - m)
            outs.append(pv[:HEAD_DIM] / denom)
        for pair in range(GROUP // 2):
            lo = (kvh * GROUP + 2 * pair) * HEAD_DIM
            o_ref[rows, lo:lo + 2 * HEAD_DIM] = jnp.concatenate(outs[2 * pair:2 * pair + 2], axis=0).T


def _swa_prompt(sinks, p3):
    batch, seq, _ = p3.shape
    tq = SWA_TILE
    nsub = tq // WINDOW
    q_blk = P_AQ // ATTN_Q_WIDTH
    kv_blk = P_KV // (2 * KV_WIDTH)
    return pl.pallas_call(
        _swa_prompt_kernel,
        grid=(batch, seq // tq),
        in_specs=[pl.BlockSpec(memory_space=pltpu.SMEM),
                  pl.BlockSpec((None, tq, ATTN_Q_WIDTH), lambda b, i: (b, i, q_blk)),
                  pl.BlockSpec((None, tq, 2 * KV_WIDTH), lambda b, i: (b, i, kv_blk)),
                  pl.BlockSpec((None, WINDOW, 2 * KV_WIDTH),
                               lambda b, i: (b, jnp.maximum(i * nsub - 1, 0), kv_blk))],
        out_specs=pl.BlockSpec((None, tq, ATTN_Q_WIDTH), lambda b, i: (b, i, 0)),
        out_shape=jax.ShapeDtypeStruct((batch, seq, ATTN_Q_WIDTH), F32),
        scratch_shapes=[pltpu.VMEM((N_KV_HEADS, 2 * WINDOW, GROUP * WINDOW), F32),
                        pltpu.VMEM((nsub * N_KV_HEADS, 2 * WINDOW, GROUP * WINDOW), F32)],
        compiler_params=_params("arbitrary", "arbitrary"),
        name="swa_prompt",
    )(sinks, p3, p3, p3)


INV_LEVELS = tuple(2 ** i for i in range(1, CHUNK.bit_length()))


def _gdn_prompt_kernel(raw_ref, prev_ref, gz_ref, gba_ref, convw_ref, alog_ref, dtb_ref, ng_ref,
                       o_ref, s_ref, mask_ref, act_ref):
    t = pl.program_id(1)
    tt = GDN_TILE
    nchunk = tt // CHUNK
    heads = range(N_GDN_HEADS)

    r = lax.broadcasted_iota(jnp.int32, (tt, tt), 0)
    c = lax.broadcasted_iota(jnp.int32, (tt, tt), 1)
    xor_idx = r ^ c

    @pl.when((pl.program_id(0) == 0) & (t == 0))
    def _():
        mask_ref[0] = jnp.where((xor_idx < CHUNK) & (r >= c), 1.0, 0.0).astype(BF16)
        for i, s in enumerate(INV_LEVELS):
            mask_ref[i + 1] = jnp.where((xor_idx >= s // 2) & (xor_idx < s) & (r > c),
                                        1.0, 0.0).astype(BF16)

    @pl.when(t == 0)
    def _():
        s_ref[...] = jnp.zeros(s_ref.shape, F32)

    strict = (xor_idx < CHUNK) & (r > c)
    diag = r == c

    nseq = raw_ref.shape[0]
    chains = [(sq, h) for sq in range(nseq) for h in heads]
    ids = range(len(chains))

    for sq in range(nseq):
        for cg in range(GDN_CONV_DIM // LANES):
            cols = slice(cg * LANES, (cg + 1) * LANES)
            w = convw_ref[:, cols]
            for r0 in range(0, tt, CONV_ROWS):
                if r0 == 0:
                    before = jnp.where(t > 0, prev_ref[sq, :, cols], 0.0)
                else:
                    before = raw_ref[sq, r0 - SUBLANES:r0, cols]
                cur = raw_ref[sq, r0:r0 + CONV_ROWS, cols]
                ext = jnp.concatenate([before, cur], axis=0)
                acc = w[0:1] * ext
                for j in range(1, CONV_WIDTH):
                    acc = pltpu.roll(acc, 1, axis=0) + w[j:j + 1] * ext
                act = _silu(acc[SUBLANES:])
                if cg < GDN_QK_WIDTH // LANES:
                    act = _l2norm(act) * (GDN_DK ** -0.5)
                elif cg < 2 * GDN_QK_WIDTH // LANES:
                    act = _l2norm(act)
                act_ref[sq, r0:r0 + CONV_ROWS, cols] = act

    gates, gc, gc_t, gtot = [], [], [], []
    for sq in range(nseq):
        beta_all, g_all = _gdn_gates(gba_ref[sq], alog_ref[...], dtb_ref[...])
        g_hi = g_all.astype(BF16)
        g_lo = (g_all - g_hi.astype(F32)).astype(BF16)
        cs = jnp.dot(mask_ref[0], jnp.concatenate([g_hi, g_lo], axis=1),
                     preferred_element_type=F32)
        gc_sq = (cs[:, :LANES] + cs[:, LANES:]) * LOG2E
        gates.append(beta_all)
        gc.append(gc_sq)
        gc_t.append(gc_sq.T)
        gtot.append(jnp.concatenate(
            [jnp.broadcast_to(gc_sq[(ci + 1) * CHUNK - 1:(ci + 1) * CHUNK, :], (CHUNK, LANES))
             for ci in range(nchunk)], axis=0))

    q_l, k_l, v_l, beta_l, gcol_l, gend_l, a_l, qk_l, x_l = [], [], [], [], [], [], [], [], []
    for sq, h in chains:
        q = act_ref[sq, :, h * GDN_DK:(h + 1) * GDN_DK]
        k = act_ref[sq, :, GDN_QK_WIDTH + h * GDN_DK:GDN_QK_WIDTH + (h + 1) * GDN_DK]
        beta = gates[sq][:, h:h + 1]
        gcol = gc[sq][:, N_GDN_HEADS + h:N_GDN_HEADS + h + 1]
        grow = gc_t[sq][N_GDN_HEADS + h:N_GDN_HEADS + h + 1, :]
        decay = jnp.exp2(jnp.where(strict, gcol - grow, -jnp.inf))
        kb = k.astype(BF16)
        qk_kk = _dot_nt(jnp.concatenate([q.astype(BF16), kb], axis=0), kb)
        a = ((beta * qk_kk[tt:]) * decay).astype(BF16)
        qk_l.append((qk_kk[:tt] * jnp.where(diag, 1.0, decay)).astype(BF16))
        x_l.append(jnp.where(diag, 1.0, 0.0).astype(BF16) - a * mask_ref[1])
        q_l.append(q); k_l.append(k); beta_l.append(beta); gcol_l.append(gcol); a_l.append(a)
        gend_l.append(gtot[sq][:, N_GDN_HEADS + h:N_GDN_HEADS + h + 1])
        v_l.append(act_ref[sq, :, 2 * GDN_QK_WIDTH + h * GDN_DV:2 * GDN_QK_WIDTH + (h + 1) * GDN_DV])

    for i in range(1, len(INV_LEVELS)):
        for n in ids:
            x = x_l[n]
            y = jnp.dot(x, a_l[n], preferred_element_type=F32).astype(BF16)
            z = jnp.dot(y, x, preferred_element_type=F32).astype(BF16)
            x_l[n] = x - z * mask_ref[i + 1]

    u_l, w_l, qd_l, kd_l = [], [], [], []
    for n in ids:
        eg = jnp.exp2(gcol_l[n])
        rhs = jnp.concatenate([v_l[n] * beta_l[n], k_l[n] * (beta_l[n] * eg)], axis=1)
        uw = jnp.dot(x_l[n], rhs.astype(BF16), preferred_element_type=F32)
        u_l.append(uw[:, :GDN_DV])
        w_l.append(uw[:, GDN_DV:].astype(BF16))
        qd_l.append((q_l[n] * eg).astype(BF16))
        kd_l.append((k_l[n] * jnp.exp2(gend_l[n] - gcol_l[n])).astype(BF16))

    states = [s_ref[sq, h] for sq, h in chains]
    v_new = [[] for _ in ids]
    o_inter = [[] for _ in ids]
    for ci in range(nchunk):
        rows = slice(ci * CHUNK, (ci + 1) * CHUNK)
        for n in ids:
            sb = states[n].astype(BF16)
            ws_qs = jnp.dot(jnp.concatenate([w_l[n][rows], qd_l[n][rows]], axis=0), sb,
                            preferred_element_type=F32)
            vn = u_l[n][rows] - ws_qs[:CHUNK]
            o_inter[n].append(ws_qs[CHUNK:])
            g_last = jnp.exp2(gend_l[n][ci * CHUNK:ci * CHUNK + 1, :])
            upd = lax.dot_general(kd_l[n][rows], vn.astype(BF16), (((0,), (0,)), ((), ())),
                                  preferred_element_type=F32)
            states[n] = states[n] * g_last + upd
            v_new[n].append(vn)

    for n, (sq, h) in enumerate(chains):
        s_ref[sq, h] = states[n]
        vn_all = jnp.concatenate(v_new[n], axis=0).astype(BF16)
        o = jnp.concatenate(o_inter[n], axis=0) + jnp.dot(qk_l[n], vn_all,
                                                           preferred_element_type=F32)
        gz = gz_ref[sq, :, h * GDN_DV:(h + 1) * GDN_DV]
        o_ref[sq, :, h * GDN_DV:(h + 1) * GDN_DV] = _rms(o, ng_ref[...]) * _silu(gz)


def _gdn_prompt(p3, convw, alog, dtb, ng):
    batch, seq, _ = p3.shape
    tt = GDN_TILE
    ns = GDN_SEQS if batch % GDN_SEQS == 0 else 1
    prev_per_tile = tt // SUBLANES
    small = lambda shape: pl.BlockSpec(shape, lambda b, t: (0,) * len(shape))
    return pl.pallas_call(
        _gdn_prompt_kernel,
        grid=(batch // ns, seq // tt),
        in_specs=[pl.BlockSpec((ns, tt, GDN_CONV_DIM), lambda b, t: (b, t, P_GQKV // GDN_CONV_DIM)),
                  pl.BlockSpec((ns, SUBLANES, GDN_CONV_DIM),
                               lambda b, t: (b, jnp.maximum(t * prev_per_tile - 1, 0),
                                             P_GQKV // GDN_CONV_DIM)),
                  pl.BlockSpec((ns, tt, GDN_V_WIDTH), lambda b, t: (b, t, P_GZ // GDN_V_WIDTH)),
                  pl.BlockSpec((ns, tt, LANES), lambda b, t: (b, t, P_GBA // LANES)),
                  small((CONV_WIDTH, GDN_CONV_DIM)), small((1, LANES)), small((1, LANES)),
                  small((1, GDN_DV))],
        out_specs=[pl.BlockSpec((ns, tt, GDN_V_WIDTH), lambda b, t: (b, t, 0)),
                   pl.BlockSpec((ns, N_GDN_HEADS, GDN_DK, GDN_DV), lambda b, t: (b, 0, 0, 0))],
        out_shape=[jax.ShapeDtypeStruct((batch, seq, GDN_V_WIDTH), F32),
                   jax.ShapeDtypeStruct((batch, N_GDN_HEADS, GDN_DK, GDN_DV), F32)],
        scratch_shapes=[pltpu.VMEM((len(INV_LEVELS) + 1, tt, tt), BF16),
                        pltpu.VMEM((ns, tt, GDN_CONV_DIM), F32)],
        compiler_params=_params("arbitrary", "arbitrary"),
        name="gdn_prompt",
    )(p3, p3, p3, p3, convw, alog, dtb, ng)


def _sample_prep_kernel(p_ref, hist_ref, convw_ref, alog_ref, dtb_ref,
                        conv_ref, qt_ref, kt_ref, v_ref, beta_ref, decay_ref, kvt_ref):
    nblk = qt_ref.shape[0]
    kv_t = p_ref[:, P_KV:P_KV + 2 * KV_WIDTH].T
    for i in range(nblk):
        kvt_ref[i] = kv_t[:, i * SAMPLE_BLOCK:(i + 1) * SAMPLE_BLOCK]
    raw = p_ref[:, P_GQKV:P_GQKV + GDN_CONV_DIM]
    w = convw_ref[...]
    conv = w[0:1] * hist_ref[:, 0:GDN_CONV_DIM]
    for j in range(1, CONV_WIDTH - 1):
        conv = conv + w[j:j + 1] * hist_ref[:, j * GDN_CONV_DIM:(j + 1) * GDN_CONV_DIM]
    conv = conv + w[CONV_WIDTH - 1:CONV_WIDTH] * raw
    act = _silu(conv)
    conv_ref[:, 0:(CONV_WIDTH - 2) * GDN_CONV_DIM] = hist_ref[:, GDN_CONV_DIM:]
    conv_ref[:, (CONV_WIDTH - 2) * GDN_CONV_DIM:] = raw
    v_ref[...] = act[:, 2 * GDN_QK_WIDTH:]
    beta, g = _gdn_gates(p_ref[:, P_GBA:P_GBA + LANES], alog_ref[...], dtb_ref[...])
    beta_ref[...] = beta
    decay_ref[...] = jnp.exp(g)
    for h in range(N_GDN_HEADS):
        qt = (_l2norm(act[:, h * GDN_DK:(h + 1) * GDN_DK]) * (GDN_DK ** -0.5)).T
        kt = _l2norm(act[:, GDN_QK_WIDTH + h * GDN_DK:GDN_QK_WIDTH + (h + 1) * GDN_DK]).T
        for i in range(nblk):
            qt_ref[i, h] = qt[:, i * SAMPLE_BLOCK:(i + 1) * SAMPLE_BLOCK]
            kt_ref[i, h] = kt[:, i * SAMPLE_BLOCK:(i + 1) * SAMPLE_BLOCK]


def _sample_prep(p, hist, convw, alog, dtb):
    nseq = p.shape[0]
    nblk = nseq // SAMPLE_BLOCK
    hist_w = (CONV_WIDTH - 1) * GDN_CONV_DIM
    cols = jax.ShapeDtypeStruct((nblk, N_GDN_HEADS, GDN_DK, SAMPLE_BLOCK), F32)
    return pl.pallas_call(
        _sample_prep_kernel,
        out_shape=[jax.ShapeDtypeStruct((nseq, hist_w), F32), cols, cols,
                   jax.ShapeDtypeStruct((nseq, GDN_V_WIDTH), F32),
                   jax.ShapeDtypeStruct((nseq, LANES), F32),
                   jax.ShapeDtypeStruct((nseq, LANES), F32),
                   jax.ShapeDtypeStruct((nblk, 2 * KV_WIDTH, SAMPLE_BLOCK), F32)],
        compiler_params=pltpu.CompilerParams(vmem_limit_bytes=VMEM_LIMIT),
        name="sample_prep",
    )(p, hist, convw, alog, dtb)


def _per_head(values):
    h = lax.broadcasted_iota(jnp.int32, (N_Q_HEADS, 1), 0)
    col = jnp.full((N_Q_HEADS, 1), values[N_Q_HEADS - 1], F32)
    for i in range(N_Q_HEADS - 2, -1, -1):
        col = jnp.where(h == i, values[i], col)
    return col


def _sample_mix_kernel(sink_ref, q_ref, p_ref, kc_ref, vc_ref, kvt_ref, qt_ref, kt_ref, v_ref,
                       beta_ref, decay_ref, s_ref, ng_ref, attn_ref, gdn_ref, nk_ref, nv_ref, ns_ref):
    time = lax.broadcasted_iota(jnp.int32, (KV_WIDTH, WINDOW), 1)
    newest = time == WINDOW - 1
    key_pos = lax.broadcasted_iota(jnp.int32, (1, WINDOW), 1)
    dist_hist = (WINDOW - key_pos).astype(F32)
    head = lax.broadcasted_iota(jnp.int32, (N_Q_HEADS, HEAD_DIM), 0)
    kv_of_head = [head // GROUP == kvh for kvh in range(N_KV_HEADS)]
    slope = _per_head([_alibi_slope(h) for h in range(N_Q_HEADS)])
    sink = _per_head([sink_ref[h] for h in range(N_Q_HEADS)])
    scale = HEAD_DIM ** -0.5
    for b in range(SAMPLE_BLOCK):
        k_hist = kc_ref[b]
        v_hist = vc_ref[b]
        k_new = p_ref[b:b + 1, P_KV:P_KV + KV_WIDTH]
        v_new = p_ref[b:b + 1, P_KV + KV_WIDTH:P_KV + 2 * KV_WIDTH]
        nk_ref[b] = jnp.where(newest, kvt_ref[0:KV_WIDTH, b:b + 1],
                              pltpu.roll(k_hist, WINDOW - 1, axis=1))
        nv_ref[b] = jnp.where(newest, kvt_ref[KV_WIDTH:2 * KV_WIDTH, b:b + 1],
                              pltpu.roll(v_hist, WINDOW - 1, axis=1))
        q = q_ref[b]
        q_wide = jnp.concatenate([jnp.where(sel, q, 0.0) for sel in kv_of_head], axis=1)
        s_hist = _dot(q_wide, k_hist) * scale - slope * dist_hist
        s_new = jnp.sum(q_wide * k_new, axis=-1, keepdims=True) * scale
        m = jnp.maximum(jnp.maximum(jnp.max(s_hist, axis=-1, keepdims=True), s_new), sink)
        p_hist = jnp.exp(s_hist - m)
        p_new = jnp.exp(s_new - m)
        denom = jnp.sum(p_hist, axis=-1, keepdims=True) + p_new + jnp.exp(sink - m)
        o_wide = (_dot_nt(p_hist, v_hist) + p_new * v_new) / denom
        o = o_wide[:, 0:HEAD_DIM]
        for kvh in range(1, N_KV_HEADS):
            o = jnp.where(kv_of_head[kvh], o_wide[:, kvh * HEAD_DIM:(kvh + 1) * HEAD_DIM], o)
        attn_ref[b] = o

    heads = range(N_GDN_HEADS)
    for b in range(SAMPLE_BLOCK):
        kcol = [jnp.broadcast_to(kt_ref[h, :, b:b + 1], (GDN_DK, GDN_DV)) for h in heads]
        qcol = [jnp.broadcast_to(qt_ref[h, :, b:b + 1], (GDN_DK, GDN_DV)) for h in heads]
        decayed = [s_ref[b, h] * decay_ref[b:b + 1, N_GDN_HEADS + h:N_GDN_HEADS + h + 1]
                   for h in heads]
        ks = [jnp.sum(decayed[h] * kcol[h], axis=0, keepdims=True) for h in heads]
        delta = [beta_ref[b:b + 1, h:h + 1] * (v_ref[b:b + 1, h * GDN_DV:(h + 1) * GDN_DV] - ks[h])
                 for h in heads]
        state = [decayed[h] + kcol[h] * delta[h] for h in heads]
        outs = []
        for h in heads:
            ns_ref[b, h] = state[h]
            o = jnp.sum(state[h] * qcol[h], axis=0, keepdims=True)
            gz = p_ref[b:b + 1, P_GZ + h * GDN_DV:P_GZ + (h + 1) * GDN_DV]
            outs.append(_rms(o, ng_ref[...]) * _silu(gz))
        gdn_ref[b:b + 1, :] = jnp.concatenate(outs, axis=1)


def _sample_mix(sinks, q, p, k_hist, v_hist, kv_t, qt, kt, v, beta, decay, state, ng):
    nseq = p.shape[0]
    bb = SAMPLE_BLOCK
    rows = lambda w: pl.BlockSpec((bb, w), lambda i: (i, 0))
    heads = pl.BlockSpec((bb, N_Q_HEADS, HEAD_DIM), lambda i: (i, 0, 0))
    cache = pl.BlockSpec((bb, KV_WIDTH, WINDOW), lambda i: (i, 0, 0))
    cols = pl.BlockSpec((None, N_GDN_HEADS, GDN_DK, bb), lambda i: (i, 0, 0, 0))
    kv_cols = pl.BlockSpec((None, 2 * KV_WIDTH, bb), lambda i: (i, 0, 0))
    st = pl.BlockSpec((bb, N_GDN_HEADS, GDN_DK, GDN_DV), lambda i: (i, 0, 0, 0))
    return pl.pallas_call(
        _sample_mix_kernel,
        grid=(nseq // bb,),
        in_specs=[pl.BlockSpec(memory_space=pltpu.SMEM), heads, rows(P_WIDTH), cache, cache,
                  kv_cols, cols, cols, rows(GDN_V_WIDTH), rows(LANES), rows(LANES), st,
                  pl.BlockSpec((1, GDN_DV), lambda i: (0, 0))],
        out_specs=[heads, rows(GDN_V_WIDTH), cache, cache, st],
        out_shape=[jax.ShapeDtypeStruct((nseq, N_Q_HEADS, HEAD_DIM), F32),
                   jax.ShapeDtypeStruct((nseq, GDN_V_WIDTH), F32),
                   jax.ShapeDtypeStruct(k_hist.shape, F32),
                   jax.ShapeDtypeStruct(v_hist.shape, F32),
                   jax.ShapeDtypeStruct(state.shape, F32)],
        compiler_params=_params("parallel"),
        name="sample_mix",
    )(sinks, q, p, k_hist, v_hist, kv_t, qt, kt, v, beta, decay, state, ng)


def _lane_pad(vec, offset):
    return jnp.zeros((1, LANES), F32).at[0, offset:offset + vec.shape[0]].set(vec)


def kernel(x_prompt, x_sample, cache_attn_k, cache_attn_v, state_conv, state_gdn, ffn1_norm_g,
           ffn1_w_in, ffn1_w_out, mix_norm_g, w_in_mix, attn_sinks, conv_w, gdn_A_log, gdn_dt_bias,
           gdn_norm_g, w_out_mix, ffn2_norm_g, ffn2_w_in, ffn2_w_out, final_norm_g):
    depth = ffn1_w_in.shape[0]
    assert depth == 1, "single-layer trunk"
    batch, seq, _ = x_prompt.shape
    nseq = x_sample.shape[0]
    assert x_sample.shape[1] == 1 and cache_attn_k.shape[2] == WINDOW
    assert seq % GDN_TILE == 0 and seq % WINDOW == 0 and nseq % SAMPLE_BLOCK == 0
    l = 0
    row = lambda v: v.reshape(1, -1)
    g1, gm, g2, gf = row(ffn1_norm_g[l]), row(mix_norm_g[l]), row(ffn2_norm_g[l]), row(final_norm_g)
    ng = row(gdn_norm_g[l])
    wgu1, wo1 = ffn1_w_in[l].astype(BF16), ffn1_w_out[l].astype(BF16)
    wgu2, wo2 = ffn2_w_in[l].astype(BF16), ffn2_w_out[l].astype(BF16)
    win = jnp.pad(w_in_mix[l].astype(BF16), ((0, 0), (0, P_WIDTH - w_in_mix.shape[2])))
    wmix = w_out_mix[l].astype(BF16)
    alog = _lane_pad(gdn_A_log[l], N_GDN_HEADS)
    dtb = _lane_pad(gdn_dt_bias[l], N_GDN_HEADS)
    sinks = attn_sinks[l]
    convw = conv_w[l]

    xp = x_prompt.reshape(batch * seq, D_MODEL)
    x1p, pp = _ffn_mix_in(xp, g1, wgu1, wo1, gm, win, TOKEN_TILE)
    pp3 = pp.reshape(batch, seq, P_WIDTH)
    attn_p = _swa_prompt(sinks, pp3)
    gdn_p, state_p = _gdn_prompt(pp3, convw, alog, dtb, ng)
    y_p = _mix_out_ffn(x1p, attn_p.reshape(batch * seq, ATTN_Q_WIDTH),
                       gdn_p.reshape(batch * seq, GDN_V_WIDTH), wmix, g2, wgu2, wo2, gf, TOKEN_TILE)
    tail = pp3[:, seq - WINDOW:, P_KV:P_KV + 2 * KV_WIDTH]
    new_k_p = tail[:, :, :KV_WIDTH].reshape(1, batch, WINDOW, N_KV_HEADS, HEAD_DIM)
    new_v_p = tail[:, :, KV_WIDTH:].reshape(1, batch, WINDOW, N_KV_HEADS, HEAD_DIM)
    new_conv_p = pp3[:, seq - (CONV_WIDTH - 1):, P_GQKV:P_GQKV + GDN_CONV_DIM][None]

    xs = x_sample.reshape(nseq, D_MODEL)
    x1s, ps = _ffn_mix_in(xs, g1, wgu1, wo1, gm, win, nseq)
    hist = state_conv[l].reshape(nseq, (CONV_WIDTH - 1) * GDN_CONV_DIM)
    new_conv_s, qt, kt, v_s, beta_s, decay_s, kv_t = _sample_prep(ps, hist, convw, alog, dtb)
    time_minor = lambda c: jnp.transpose(c.reshape(nseq, WINDOW, KV_WIDTH), (0, 2, 1))
    time_major = lambda c: jnp.transpose(c, (0, 2, 1)).reshape(1, nseq, WINDOW, N_KV_HEADS, HEAD_DIM)
    q_s = ps[:, P_AQ:P_AQ + ATTN_Q_WIDTH].reshape(nseq, N_Q_HEADS, HEAD_DIM)
    attn_s, gdn_s, new_k_s, new_v_s, state_s = _sample_mix(
        sinks, q_s, ps, time_minor(cache_attn_k[l]), time_minor(cache_attn_v[l]), kv_t, qt, kt,
        v_s, beta_s, decay_s, state_gdn[l], ng)
    y_s = _mix_out_ffn(x1s, attn_s.reshape(nseq, ATTN_Q_WIDTH), gdn_s, wmix, g2, wgu2, wo2, gf,
                       nseq)

    return (y_p.reshape(batch, seq, D_MODEL), y_s.reshape(nseq, 1, D_MODEL),
            new_k_p, new_v_p, new_conv_p, state_p[None],
            time_major(new_k_s), time_major(new_v_s),
            new_conv_s.reshape(1, nseq, CONV_WIDTH - 1, GDN_CONV_DIM), state_s[None])
```

```python
import jax
import jax.numpy as jnp
from jax import lax
from jax.experimental import pallas as pl
from jax.experimental.pallas import tpu as pltpu

F32 = jnp.float32
BF16 = jnp.bfloat16

D_MODEL = 1024
D_FF = 2816
N_Q_HEADS = 8
N_KV_HEADS = 2
GROUP = N_Q_HEADS // N_KV_HEADS
HEAD_DIM = 64
WINDOW = 128
ATTN_Q_WIDTH = N_Q_HEADS * HEAD_DIM
KV_WIDTH = N_KV_HEADS * HEAD_DIM
N_GDN_HEADS = 4
GDN_DK = 128
GDN_DV = 128
GDN_QK_WIDTH = N_GDN_HEADS * GDN_DK
GDN_V_WIDTH = N_GDN_HEADS * GDN_DV
GDN_CONV_DIM = 2 * GDN_QK_WIDTH + GDN_V_WIDTH
CONV_WIDTH = 4
CHUNK = 64
EPS = 1e-6
LOG2E = 1.4426950408889634

LANES = 128
SUBLANES = 8
VMEM_LIMIT = 56 * 1024 * 1024

P_GQKV = 0
P_AQ = P_GQKV + GDN_CONV_DIM
P_GZ = P_AQ + ATTN_Q_WIDTH
P_KV = P_GZ + GDN_V_WIDTH
P_GBA = P_KV + 2 * KV_WIDTH
P_WIDTH = P_GBA + LANES

FF_CHUNK = 256
TOKEN_TILE = 512
GDN_TILE = 128
GDN_SEQS = 4
SWA_TILE = 2048
CONV_ROWS = 128
SAMPLE_BLOCK = 16


def _sigmoid(x):
    return 1.0 / (1.0 + jnp.exp(-x))


def _silu(x):
    half = 0.5 * x
    return half + half * jnp.tanh(half)


def _softplus(x):
    return jnp.maximum(x, 0.0) + jnp.log1p(jnp.exp(-jnp.abs(x)))


def _rms(x, g):
    return x * lax.rsqrt(jnp.mean(x * x, axis=-1, keepdims=True) + EPS) * g


def _dot(a, b):
    return jnp.dot(a.astype(BF16), b.astype(BF16), preferred_element_type=F32)


def _dot_nt(a, b):
    return lax.dot_general(a.astype(BF16), b.astype(BF16), (((1,), (1,)), ((), ())),
                           preferred_element_type=F32)


def _resident(shape):
    return pl.BlockSpec(shape, lambda *_: (0,) * len(shape), pipeline_mode=pl.Buffered(1))


def _params(*semantics):
    return pltpu.CompilerParams(dimension_semantics=semantics, vmem_limit_bytes=VMEM_LIMIT)


FF_STEPS = D_FF // FF_CHUNK


def _ffn_half_step(x, g, wgu_ref, wo_ref):
    n = _rms(x, g).astype(BF16)
    hidden = []
    for c in range(FF_STEPS):
        lo = c * FF_CHUNK
        gate = jnp.dot(n, wgu_ref[:, lo:lo + FF_CHUNK], preferred_element_type=F32)
        up = jnp.dot(n, wgu_ref[:, D_FF + lo:D_FF + lo + FF_CHUNK], preferred_element_type=F32)
        hidden.append((_silu(gate) * up).astype(BF16))
    out = jnp.dot(jnp.concatenate(hidden, axis=1), wo_ref[...], preferred_element_type=F32)
    return x + 0.5 * out


_W_AQ, _W_KV = 0, ATTN_Q_WIDTH
_W_GQKV = _W_KV + 2 * KV_WIDTH
_W_GZ = _W_GQKV + GDN_CONV_DIM
_W_GBA = _W_GZ + GDN_V_WIDTH
_PROJ_GROUPS = ((_W_GQKV, GDN_CONV_DIM, P_GQKV), (_W_GBA, LANES, P_GBA), (_W_AQ, ATTN_Q_WIDTH, P_AQ),
                (_W_GZ, GDN_V_WIDTH, P_GZ), (_W_KV, 2 * KV_WIDTH, P_KV))


def _gdn_gates(gba, alog, dtb):
    beta = _sigmoid(gba)
    g = -jnp.exp(alog) * _softplus(gba + dtb)
    return beta, g


def _l2norm(x, scale=None):
    inv = lax.rsqrt(jnp.sum(x * x, axis=-1, keepdims=True) + EPS)
    return x * (inv if scale is None else inv * scale)


def _ffn_mix_in_kernel(x_ref, g1_ref, wgu_ref, wo_ref, gm_ref, win_ref, x1_ref, p_ref):
    x1 = _ffn_half_step(x_ref[...], g1_ref[...], wgu_ref, wo_ref)
    x1_ref[...] = x1
    n = _rms(x1, gm_ref[...]).astype(BF16)
    for src, width, dst in _PROJ_GROUPS:
        p_ref[:, dst:dst + width] = jnp.dot(n, win_ref[:, src:src + width],
                                            preferred_element_type=F32)


def _ffn_weight_specs():
    return [_resident((1, D_MODEL)), _resident((D_MODEL, 2 * D_FF)), _resident((D_FF, D_MODEL)),
            _resident((1, D_MODEL)), _resident((D_MODEL, P_WIDTH))]


def _ffn_mix_in(x, g1, wgu, wo, gm, win, tm):
    n_tok = x.shape[0]
    row = lambda w: pl.BlockSpec((tm, w), lambda i: (i, 0))
    return pl.pallas_call(
        _ffn_mix_in_kernel,
        grid=(n_tok // tm,),
        in_specs=[row(D_MODEL)] + _ffn_weight_specs(),
        out_specs=[row(D_MODEL), row(P_WIDTH)],
        out_shape=[jax.ShapeDtypeStruct((n_tok, D_MODEL), F32),
                   jax.ShapeDtypeStruct((n_tok, P_WIDTH), F32)],
        compiler_params=_params("parallel"),
        name="ffn_mix_in",
    )(x, g1, wgu, wo, gm, win)


def _mix_out_ffn_kernel(x1_ref, attn_ref, gdn_ref, wmix_ref, g2_ref, wgu_ref, wo_ref, gf_ref, y_ref):
    mixed = jnp.concatenate([attn_ref[...].astype(BF16), gdn_ref[...].astype(BF16)], axis=1)
    x2 = x1_ref[...] + jnp.dot(mixed, wmix_ref[...], preferred_element_type=F32)
    x3 = _ffn_half_step(x2, g2_ref[...], wgu_ref, wo_ref)
    y_ref[...] = _rms(x3, gf_ref[...])


def _mix_out_ffn(x1, attn, gdn, wmix, g2, wgu, wo, gf, tm):
    n_tok = x1.shape[0]
    row = lambda w: pl.BlockSpec((tm, w), lambda i: (i, 0))
    return pl.pallas_call(
        _mix_out_ffn_kernel,
        grid=(n_tok // tm,),
        in_specs=[row(D_MODEL), row(ATTN_Q_WIDTH), row(GDN_V_WIDTH),
                  _resident((ATTN_Q_WIDTH + GDN_V_WIDTH, D_MODEL)),
                  _resident((1, D_MODEL)), _resident((D_MODEL, 2 * D_FF)),
                  _resident((D_FF, D_MODEL)), _resident((1, D_MODEL))],
        out_specs=row(D_MODEL),
        out_shape=jax.ShapeDtypeStruct((n_tok, D_MODEL), F32),
        compiler_params=_params("parallel"),
        name="mix_out_ffn",
    )(x1, attn, gdn, wmix, g2, wgu, wo, gf)


def _alibi_slope(head):
    return 2.0 ** (-8.0 * (head + 1) / N_Q_HEADS)


def _swa_prompt_kernel(sink_ref, q_ref, kvc_ref, kvp_ref, o_ref, bias_ref, s_ref):
    blk = pl.program_id(1)
    nkeys = 2 * WINDOW

    @pl.when((pl.program_id(0) == 0) & (blk == 0))
    def _():
        key = lax.broadcasted_iota(jnp.int32, (nkeys, WINDOW), 0)
        qry = lax.broadcasted_iota(jnp.int32, (nkeys, WINDOW), 1)
        dist = qry - key + WINDOW
        valid = (dist >= 0) & (dist <= WINDOW)
        for h in range(N_Q_HEADS):
            g = h % GROUP
            bias_ref[h // GROUP, :, g * WINDOW:(g + 1) * WINDOW] = jnp.where(
                valid, -(LOG2E * _alibi_slope(h)) * dist.astype(F32), -jnp.inf)

    first_pen = jnp.where(blk > 0, 0.0, -jnp.inf)
    scale = LOG2E * HEAD_DIM ** -0.5
    nsub = q_ref.shape[0] // WINDOW
    jobs = [(j, kvh) for j in range(nsub) for kvh in range(N_KV_HEADS)]
    group_heads = lambda kvh: range(kvh * GROUP, (kvh + 1) * GROUP)
    def keys_values(j, col):
        if j == 0:
            return jnp.concatenate([kvp_ref[:, col], kvc_ref[0:WINDOW, col]], axis=0)
        return kvc_ref[(j - 1) * WINDOW:(j + 1) * WINDOW, col]

    values_t = []
    for job, (j, kvh) in enumerate(jobs):
        rows = slice(j * WINDOW, (j + 1) * WINDOW)
        k_all = keys_values(j, slice(kvh * HEAD_DIM, (kvh + 1) * HEAD_DIM)).astype(BF16)
        v_t = keys_values(j, slice(KV_WIDTH + kvh * HEAD_DIM, KV_WIDTH + (kvh + 1) * HEAD_DIM)).T
        values_t.append(jnp.concatenate([v_t, jnp.ones((SUBLANES, 2 * WINDOW), F32)],
                                        axis=0).astype(BF16))
        q_stack = jnp.concatenate(
            [(q_ref[rows, h * HEAD_DIM:(h + 1) * HEAD_DIM] * scale).astype(BF16)
             for h in group_heads(kvh)], axis=0)
        s_ref[job] = lax.dot_general(k_all, q_stack, (((1,), (1,)), ((), ())),
                                     preferred_element_type=F32)
    for job, (j, kvh) in enumerate(jobs):
        rows = slice(j * WINDOW, (j + 1) * WINDOW)
        v_t = values_t[job]
        outs = []
        for g, h in enumerate(group_heads(kvh)):
            lanes = slice(g * WINDOW, (g + 1) * WINDOW)
            s = s_ref[job, :, lanes] + bias_ref[kvh, :, lanes]
            if j == 0:
                s = jnp.concatenate([s[:WINDOW] + first_pen, s[WINDOW:]], axis=0)
            sink = LOG2E * sink_ref[h]
            m = jnp.maximum(jnp.max(s, axis=0, keepdims=True), sink)
            p = jnp.exp2(s - m).astype(BF16)
            pv = jnp.dot(v_t, p, preferred_element_type=F32)
            denom = pv[HEAD_DIM:HEAD_DIM + 1] + jnp.exp2(sink - m)
            outs.append(pv[:HEAD_DIM] / denom)
        for pair in range(GROUP // 2):
            lo = (kvh * GROUP + 2 * pair) * HEAD_DIM
            o_ref[rows, lo:lo + 2 * HEAD_DIM] = jnp.concatenate(outs[2 * pair:2 * pair + 2], axis=0).T


def _swa_prompt(sinks, p3):
    batch, seq, _ = p3.shape
    tq = SWA_TILE
    nsub = tq // WINDOW
    q_blk = P_AQ // ATTN_Q_WIDTH
    kv_blk = P_KV // (2 * KV_WIDTH)
    return pl.pallas_call(
        _swa_prompt_kernel,
        grid=(batch, seq // tq),
        in_specs=[pl.BlockSpec(memory_space=pltpu.SMEM),
                  pl.BlockSpec((None, tq, ATTN_Q_WIDTH), lambda b, i: (b, i, q_blk)),
                  pl.BlockSpec((None, tq, 2 * KV_WIDTH), lambda b, i: (b, i, kv_blk)),
                  pl.BlockSpec((None, WINDOW, 2 * KV_WIDTH),
                               lambda b, i: (b, jnp.maximum(i * nsub - 1, 0), kv_blk))],
        out_specs=pl.BlockSpec((None, tq, ATTN_Q_WIDTH), lambda b, i: (b, i, 0)),
        out_shape=jax.ShapeDtypeStruct((batch, seq, ATTN_Q_WIDTH), F32),
        scratch_shapes=[pltpu.VMEM((N_KV_HEADS, 2 * WINDOW, GROUP * WINDOW), F32),
                        pltpu.VMEM((nsub * N_KV_HEADS, 2 * WINDOW, GROUP * WINDOW), F32)],
        compiler_params=_params("arbitrary", "arbitrary"),
        name="swa_prompt",
    )(sinks, p3, p3, p3)


INV_LEVELS = tuple(2 ** i for i in range(1, CHUNK.bit_length()))


def _gdn_prompt_kernel(raw_ref, prev_ref, gz_ref, gba_ref, convw_ref, alog_ref, dtb_ref, ng_ref,
                       o_ref, s_ref, mask_ref, act_ref):
    t = pl.program_id(1)
    tt = GDN_TILE
    nchunk = tt // CHUNK
    heads = range(N_GDN_HEADS)

    r = lax.broadcasted_iota(jnp.int32, (tt, tt), 0)
    c = lax.broadcasted_iota(jnp.int32, (tt, tt), 1)
    xor_idx = r ^ c

    @pl.when((pl.program_id(0) == 0) & (t == 0))
    def _():
        mask_ref[0] = jnp.where((xor_idx < CHUNK) & (r >= c), 1.0, 0.0).astype(BF16)
        for i, s in enumerate(INV_LEVELS):
            mask_ref[i + 1] = jnp.where((xor_idx >= s // 2) & (xor_idx < s) & (r > c),
                                        1.0, 0.0).astype(BF16)

    @pl.when(t == 0)
    def _():
        s_ref[...] = jnp.zeros(s_ref.shape, F32)

    strict = (xor_idx < CHUNK) & (r > c)
    diag = r == c

    nseq = raw_ref.shape[0]
    chains = [(sq, h) for sq in range(nseq) for h in heads]
    ids = range(len(chains))

    for sq in range(nseq):
        for cg in range(GDN_CONV_DIM // LANES):
            cols = slice(cg * LANES, (cg + 1) * LANES)
            w = convw_ref[:, cols]
            for r0 in range(0, tt, CONV_ROWS):
                if r0 == 0:
                    before = jnp.where(t > 0, prev_ref[sq, :, cols], 0.0)
                else:
                    before = raw_ref[sq, r0 - SUBLANES:r0, cols]
                cur = raw_ref[sq, r0:r0 + CONV_ROWS, cols]
                ext = jnp.concatenate([before, cur], axis=0)
                acc = w[0:1] * ext
                for j in range(1, CONV_WIDTH):
                    acc = pltpu.roll(acc, 1, axis=0) + w[j:j + 1] * ext
                act = _silu(acc[SUBLANES:])
                if cg < GDN_QK_WIDTH // LANES:
                    act = _l2norm(act, GDN_DK ** -0.5)
                elif cg < 2 * GDN_QK_WIDTH // LANES:
                    act = _l2norm(act)
                act_ref[sq, r0:r0 + CONV_ROWS, cols] = act

    gates, gc, gc_t, gtot = [], [], [], []
    for sq in range(nseq):
        beta_all, g_all = _gdn_gates(gba_ref[sq], alog_ref[...], dtb_ref[...])
        g_hi = g_all.astype(BF16)
        g_lo = (g_all - g_hi.astype(F32)).astype(BF16)
        cs = jnp.dot(mask_ref[0], jnp.concatenate([g_hi, g_lo], axis=1),
                     preferred_element_type=F32)
        gc_sq = (cs[:, :LANES] + cs[:, LANES:]) * LOG2E
        gates.append(beta_all)
        gc.append(gc_sq)
        gc_t.append(gc_sq.T)
        gtot.append(jnp.concatenate(
            [jnp.broadcast_to(gc_sq[(ci + 1) * CHUNK - 1:(ci + 1) * CHUNK, :], (CHUNK, LANES))
             for ci in range(nchunk)], axis=0))

    q_l, k_l, v_l, beta_l, gcol_l, gend_l, a_l, qk_l, x_l = [], [], [], [], [], [], [], [], []
    for sq, h in chains:
        q = act_ref[sq, :, h * GDN_DK:(h + 1) * GDN_DK]
        k = act_ref[sq, :, GDN_QK_WIDTH + h * GDN_DK:GDN_QK_WIDTH + (h + 1) * GDN_DK]
        beta = gates[sq][:, h:h + 1]
        gcol = gc[sq][:, N_GDN_HEADS + h:N_GDN_HEADS + h + 1]
        grow = gc_t[sq][N_GDN_HEADS + h:N_GDN_HEADS + h + 1, :]
        decay = jnp.exp2(jnp.where(strict, gcol - grow, -jnp.inf))
        kb = k.astype(BF16)
        qk_kk = _dot_nt(jnp.concatenate([q.astype(BF16), kb], axis=0), kb)
        a = ((beta * qk_kk[tt:]) * decay).astype(BF16)
        qk_l.append((qk_kk[:tt] * jnp.where(diag, 1.0, decay)).astype(BF16))
        x_l.append(jnp.where(diag, 1.0, 0.0).astype(BF16) - a * mask_ref[1])
        q_l.append(q); k_l.append(k); beta_l.append(beta); gcol_l.append(gcol); a_l.append(a)
        gend_l.append(gtot[sq][:, N_GDN_HEADS + h:N_GDN_HEADS + h + 1])
        v_l.append(act_ref[sq, :, 2 * GDN_QK_WIDTH + h * GDN_DV:2 * GDN_QK_WIDTH + (h + 1) * GDN_DV])

    for i in range(1, len(INV_LEVELS)):
        for n in ids:
            x = x_l[n]
            y = jnp.dot(x, a_l[n], preferred_element_type=F32).astype(BF16)
            z = jnp.dot(y, x, preferred_element_type=F32).astype(BF16)
            x_l[n] = x - z * mask_ref[i + 1]

    u_l, w_l, qd_l, kd_l = [], [], [], []
    for n in ids:
        eg = jnp.exp2(gcol_l[n])
        rhs = jnp.concatenate([v_l[n] * beta_l[n], k_l[n] * (beta_l[n] * eg)], axis=1)
        uw = jnp.dot(x_l[n], rhs.astype(BF16), preferred_element_type=F32)
        u_l.append(uw[:, :GDN_DV])
        w_l.append(uw[:, GDN_DV:].astype(BF16))
        qd_l.append((q_l[n] * eg).astype(BF16))
        kd_l.append((k_l[n] * jnp.exp2(gend_l[n] - gcol_l[n])).astype(BF16))

    states = [s_ref[sq, h] for sq, h in chains]
    v_new = [[] for _ in ids]
    o_inter = [[] for _ in ids]
    for ci in range(nchunk):
        rows = slice(ci * CHUNK, (ci + 1) * CHUNK)
        for n in ids:
            sb = states[n].astype(BF16)
            ws_qs = jnp.dot(jnp.concatenate([w_l[n][rows], qd_l[n][rows]], axis=0), sb,
                            preferred_element_type=F32)
            vn = u_l[n][rows] - ws_qs[:CHUNK]
            o_inter[n].append(ws_qs[CHUNK:])
            g_last = jnp.exp2(gend_l[n][ci * CHUNK:ci * CHUNK + 1, :])
            upd = lax.dot_general(kd_l[n][rows], vn.astype(BF16), (((0,), (0,)), ((), ())),
                                  preferred_element_type=F32)
            states[n] = states[n] * g_last + upd
            v_new[n].append(vn)

    for n, (sq, h) in enumerate(chains):
        s_ref[sq, h] = states[n]
        vn_all = jnp.concatenate(v_new[n], axis=0).astype(BF16)
        o = jnp.concatenate(o_inter[n], axis=0) + jnp.dot(qk_l[n], vn_all,
                                                           preferred_element_type=F32)
        gz = gz_ref[sq, :, h * GDN_DV:(h + 1) * GDN_DV]
        o_ref[sq, :, h * GDN_DV:(h + 1) * GDN_DV] = _rms(o, ng_ref[...]) * _silu(gz)


def _gdn_prompt(p3, convw, alog, dtb, ng):
    batch, seq, _ = p3.shape
    tt = GDN_TILE
    ns = GDN_SEQS if batch % GDN_SEQS == 0 else 1
    prev_per_tile = tt // SUBLANES
    small = lambda shape: pl.BlockSpec(shape, lambda b, t: (0,) * len(shape))
    return pl.pallas_call(
        _gdn_prompt_kernel,
        grid=(batch // ns, seq // tt),
        in_specs=[pl.BlockSpec((ns, tt, GDN_CONV_DIM), lambda b, t: (b, t, P_GQKV // GDN_CONV_DIM)),
                  pl.BlockSpec((ns, SUBLANES, GDN_CONV_DIM),
                               lambda b, t: (b, jnp.maximum(t * prev_per_tile - 1, 0),
                                             P_GQKV // GDN_CONV_DIM)),
                  pl.BlockSpec((ns, tt, GDN_V_WIDTH), lambda b, t: (b, t, P_GZ // GDN_V_WIDTH)),
                  pl.BlockSpec((ns, tt, LANES), lambda b, t: (b, t, P_GBA // LANES)),
                  small((CONV_WIDTH, GDN_CONV_DIM)), small((1, LANES)), small((1, LANES)),
                  small((1, GDN_DV))],
        out_specs=[pl.BlockSpec((ns, tt, GDN_V_WIDTH), lambda b, t: (b, t, 0)),
                   pl.BlockSpec((ns, N_GDN_HEADS, GDN_DK, GDN_DV), lambda b, t: (b, 0, 0, 0))],
        out_shape=[jax.ShapeDtypeStruct((batch, seq, GDN_V_WIDTH), F32),
                   jax.ShapeDtypeStruct((batch, N_GDN_HEADS, GDN_DK, GDN_DV), F32)],
        scratch_shapes=[pltpu.VMEM((len(INV_LEVELS) + 1, tt, tt), BF16),
                        pltpu.VMEM((ns, tt, GDN_CONV_DIM), F32)],
        compiler_params=_params("arbitrary", "arbitrary"),
        name="gdn_prompt",
    )(p3, p3, p3, p3, convw, alog, dtb, ng)


def _sample_prep_kernel(p_ref, hist_ref, convw_ref, alog_ref, dtb_ref,
                        conv_ref, qt_ref, kt_ref, v_ref, beta_ref, decay_ref, kvt_ref):
    nblk = qt_ref.shape[0]
    kv_t = p_ref[:, P_KV:P_KV + 2 * KV_WIDTH].T
    for i in range(nblk):
        kvt_ref[i] = kv_t[:, i * SAMPLE_BLOCK:(i + 1) * SAMPLE_BLOCK]
    raw = p_ref[:, P_GQKV:P_GQKV + GDN_CONV_DIM]
    w = convw_ref[...]
    conv = w[0:1] * hist_ref[:, 0:GDN_CONV_DIM]
    for j in range(1, CONV_WIDTH - 1):
        conv = conv + w[j:j + 1] * hist_ref[:, j * GDN_CONV_DIM:(j + 1) * GDN_CONV_DIM]
    conv = conv + w[CONV_WIDTH - 1:CONV_WIDTH] * raw
    act = _silu(conv)
    conv_ref[:, 0:(CONV_WIDTH - 2) * GDN_CONV_DIM] = hist_ref[:, GDN_CONV_DIM:]
    conv_ref[:, (CONV_WIDTH - 2) * GDN_CONV_DIM:] = raw
    v_ref[...] = act[:, 2 * GDN_QK_WIDTH:]
    beta, g = _gdn_gates(p_ref[:, P_GBA:P_GBA + LANES], alog_ref[...], dtb_ref[...])
    beta_ref[...] = beta
    decay_ref[...] = jnp.exp(g)
    for h in range(N_GDN_HEADS):
        qt = _l2norm(act[:, h * GDN_DK:(h + 1) * GDN_DK], GDN_DK ** -0.5).T
        kt = _l2norm(act[:, GDN_QK_WIDTH + h * GDN_DK:GDN_QK_WIDTH + (h + 1) * GDN_DK]).T
        for i in range(nblk):
            qt_ref[i, h] = qt[:, i * SAMPLE_BLOCK:(i + 1) * SAMPLE_BLOCK]
            kt_ref[i, h] = kt[:, i * SAMPLE_BLOCK:(i + 1) * SAMPLE_BLOCK]


def _sample_prep(p, hist, convw, alog, dtb):
    nseq = p.shape[0]
    nblk = nseq // SAMPLE_BLOCK
    hist_w = (CONV_WIDTH - 1) * GDN_CONV_DIM
    cols = jax.ShapeDtypeStruct((nblk, N_GDN_HEADS, GDN_DK, SAMPLE_BLOCK), F32)
    return pl.pallas_call(
        _sample_prep_kernel,
        out_shape=[jax.ShapeDtypeStruct((nseq, hist_w), F32), cols, cols,
                   jax.ShapeDtypeStruct((nseq, GDN_V_WIDTH), F32),
                   jax.ShapeDtypeStruct((nseq, LANES), F32),
                   jax.ShapeDtypeStruct((nseq, LANES), F32),
                   jax.ShapeDtypeStruct((nblk, 2 * KV_WIDTH, SAMPLE_BLOCK), F32)],
        compiler_params=pltpu.CompilerParams(vmem_limit_bytes=VMEM_LIMIT),
        name="sample_prep",
    )(p, hist, convw, alog, dtb)


def _per_head(values):
    h = lax.broadcasted_iota(jnp.int32, (N_Q_HEADS, 1), 0)
    col = jnp.full((N_Q_HEADS, 1), values[N_Q_HEADS - 1], F32)
    for i in range(N_Q_HEADS - 2, -1, -1):
        col = jnp.where(h == i, values[i], col)
    return col


def _sample_mix_kernel(sink_ref, q_ref, p_ref, kc_ref, vc_ref, kvt_ref, qt_ref, kt_ref, v_ref,
                       beta_ref, decay_ref, s_ref, ng_ref, attn_ref, gdn_ref, nk_ref, nv_ref, ns_ref):
    time = lax.broadcasted_iota(jnp.int32, (KV_WIDTH, WINDOW), 1)
    newest = time == WINDOW - 1
    key_pos = lax.broadcasted_iota(jnp.int32, (1, WINDOW), 1)
    dist_hist = (WINDOW - key_pos).astype(F32)
    head = lax.broadcasted_iota(jnp.int32, (N_Q_HEADS, HEAD_DIM), 0)
    kv_of_head = [head // GROUP == kvh for kvh in range(N_KV_HEADS)]
    slope = _per_head([_alibi_slope(h) for h in range(N_Q_HEADS)])
    sink = _per_head([sink_ref[h] for h in range(N_Q_HEADS)])
    scale = HEAD_DIM ** -0.5
    for b in range(SAMPLE_BLOCK):
        k_hist = kc_ref[b]
        v_hist = vc_ref[b]
        k_new = p_ref[b:b + 1, P_KV:P_KV + KV_WIDTH]
        v_new = p_ref[b:b + 1, P_KV + KV_WIDTH:P_KV + 2 * KV_WIDTH]
        nk_ref[b] = jnp.where(newest, kvt_ref[0:KV_WIDTH, b:b + 1],
                              pltpu.roll(k_hist, WINDOW - 1, axis=1))
        nv_ref[b] = jnp.where(newest, kvt_ref[KV_WIDTH:2 * KV_WIDTH, b:b + 1],
                              pltpu.roll(v_hist, WINDOW - 1, axis=1))
        q = q_ref[b]
        q_wide = jnp.concatenate([jnp.where(sel, q, 0.0) for sel in kv_of_head], axis=1)
        s_hist = _dot(q_wide, k_hist) * scale - slope * dist_hist
        s_new = jnp.sum(q_wide * k_new, axis=-1, keepdims=True) * scale
        m = jnp.maximum(jnp.maximum(jnp.max(s_hist, axis=-1, keepdims=True), s_new), sink)
        p_hist = jnp.exp(s_hist - m)
        p_new = jnp.exp(s_new - m)
        denom = jnp.sum(p_hist, axis=-1, keepdims=True) + p_new + jnp.exp(sink - m)
        o_wide = (_dot_nt(p_hist, v_hist) + p_new * v_new) / denom
        o = o_wide[:, 0:HEAD_DIM]
        for kvh in range(1, N_KV_HEADS):
            o = jnp.where(kv_of_head[kvh], o_wide[:, kvh * HEAD_DIM:(kvh + 1) * HEAD_DIM], o)
        attn_ref[b] = o

    heads = range(N_GDN_HEADS)
    for b in range(SAMPLE_BLOCK):
        kcol = [jnp.broadcast_to(kt_ref[h, :, b:b + 1], (GDN_DK, GDN_DV)) for h in heads]
        qcol = [jnp.broadcast_to(qt_ref[h, :, b:b + 1], (GDN_DK, GDN_DV)) for h in heads]
        decayed = [s_ref[b, h] * decay_ref[b:b + 1, N_GDN_HEADS + h:N_GDN_HEADS + h + 1]
                   for h in heads]
        ks = [jnp.sum(decayed[h] * kcol[h], axis=0, keepdims=True) for h in heads]
        delta = [beta_ref[b:b + 1, h:h + 1] * (v_ref[b:b + 1, h * GDN_DV:(h + 1) * GDN_DV] - ks[h])
                 for h in heads]
        state = [decayed[h] + kcol[h] * delta[h] for h in heads]
        outs = []
        for h in heads:
            ns_ref[b, h] = state[h]
            o = jnp.sum(state[h] * qcol[h], axis=0, keepdims=True)
            gz = p_ref[b:b + 1, P_GZ + h * GDN_DV:P_GZ + (h + 1) * GDN_DV]
            outs.append(_rms(o, ng_ref[...]) * _silu(gz))
        gdn_ref[b:b + 1, :] = jnp.concatenate(outs, axis=1)


def _sample_mix(sinks, q, p, k_hist, v_hist, kv_t, qt, kt, v, beta, decay, state, ng):
    nseq = p.shape[0]
    bb = SAMPLE_BLOCK
    rows = lambda w: pl.BlockSpec((bb, w), lambda i: (i, 0))
    heads = pl.BlockSpec((bb, N_Q_HEADS, HEAD_DIM), lambda i: (i, 0, 0))
    cache = pl.BlockSpec((bb, KV_WIDTH, WINDOW), lambda i: (i, 0, 0))
    cols = pl.BlockSpec((None, N_GDN_HEADS, GDN_DK, bb), lambda i: (i, 0, 0, 0))
    kv_cols = pl.BlockSpec((None, 2 * KV_WIDTH, bb), lambda i: (i, 0, 0))
    st = pl.BlockSpec((bb, N_GDN_HEADS, GDN_DK, GDN_DV), lambda i: (i, 0, 0, 0))
    return pl.pallas_call(
        _sample_mix_kernel,
        grid=(nseq // bb,),
        in_specs=[pl.BlockSpec(memory_space=pltpu.SMEM), heads, rows(P_WIDTH), cache, cache,
                  kv_cols, cols, cols, rows(GDN_V_WIDTH), rows(LANES), rows(LANES), st,
                  pl.BlockSpec((1, GDN_DV), lambda i: (0, 0))],
        out_specs=[heads, rows(GDN_V_WIDTH), cache, cache, st],
        out_shape=[jax.ShapeDtypeStruct((nseq, N_Q_HEADS, HEAD_DIM), F32),
                   jax.ShapeDtypeStruct((nseq, GDN_V_WIDTH), F32),
                   jax.ShapeDtypeStruct(k_hist.shape, F32),
                   jax.ShapeDtypeStruct(v_hist.shape, F32),
                   jax.ShapeDtypeStruct(state.shape, F32)],
        compiler_params=_params("parallel"),
        name="sample_mix",
    )(sinks, q, p, k_hist, v_hist, kv_t, qt, kt, v, beta, decay, state, ng)


def _lane_pad(vec, offset):
    return jnp.zeros((1, LANES), F32).at[0, offset:offset + vec.shape[0]].set(vec)


def kernel(x_prompt, x_sample, cache_attn_k, cache_attn_v, state_conv, state_gdn, ffn1_norm_g,
           ffn1_w_in, ffn1_w_out, mix_norm_g, w_in_mix, attn_sinks, conv_w, gdn_A_log, gdn_dt_bias,
           gdn_norm_g, w_out_mix, ffn2_norm_g, ffn2_w_in, ffn2_w_out, final_norm_g):
    depth = ffn1_w_in.shape[0]
    assert depth == 1, "single-layer trunk"
    batch, seq, _ = x_prompt.shape
    nseq = x_sample.shape[0]
    assert x_sample.shape[1] == 1 and cache_attn_k.shape[2] == WINDOW
    assert seq % GDN_TILE == 0 and seq % WINDOW == 0 and nseq % SAMPLE_BLOCK == 0
    l = 0
    row = lambda v: v.reshape(1, -1)
    g1, gm, g2, gf = row(ffn1_norm_g[l]), row(mix_norm_g[l]), row(ffn2_norm_g[l]), row(final_norm_g)
    ng = row(gdn_norm_g[l])
    wgu1, wo1 = ffn1_w_in[l].astype(BF16), ffn1_w_out[l].astype(BF16)
    wgu2, wo2 = ffn2_w_in[l].astype(BF16), ffn2_w_out[l].astype(BF16)
    win = jnp.pad(w_in_mix[l].astype(BF16), ((0, 0), (0, P_WIDTH - w_in_mix.shape[2])))
    wmix = w_out_mix[l].astype(BF16)
    alog = _lane_pad(gdn_A_log[l], N_GDN_HEADS)
    dtb = _lane_pad(gdn_dt_bias[l], N_GDN_HEADS)
    sinks = attn_sinks[l]
    convw = conv_w[l]

    xp = x_prompt.reshape(batch * seq, D_MODEL)
    x1p, pp = _ffn_mix_in(xp, g1, wgu1, wo1, gm, win, TOKEN_TILE)
    pp3 = pp.reshape(batch, seq, P_WIDTH)
    attn_p = _swa_prompt(sinks, pp3)
    gdn_p, state_p = _gdn_prompt(pp3, convw, alog, dtb, ng)
    y_p = _mix_out_ffn(x1p, attn_p.reshape(batch * seq, ATTN_Q_WIDTH),
                       gdn_p.reshape(batch * seq, GDN_V_WIDTH), wmix, g2, wgu2, wo2, gf, TOKEN_TILE)
    tail = pp3[:, seq - WINDOW:, P_KV:P_KV + 2 * KV_WIDTH]
    new_k_p = tail[:, :, :KV_WIDTH].reshape(1, batch, WINDOW, N_KV_HEADS, HEAD_DIM)
    new_v_p = tail[:, :, KV_WIDTH:].reshape(1, batch, WINDOW, N_KV_HEADS, HEAD_DIM)
    new_conv_p = pp3[:, seq - (CONV_WIDTH - 1):, P_GQKV:P_GQKV + GDN_CONV_DIM][None]

    xs = x_sample.reshape(nseq, D_MODEL)
    x1s, ps = _ffn_mix_in(xs, g1, wgu1, wo1, gm, win, nseq)
    hist = state_conv[l].reshape(nseq, (CONV_WIDTH - 1) * GDN_CONV_DIM)
    new_conv_s, qt, kt, v_s, beta_s, decay_s, kv_t = _sample_prep(ps, hist, convw, alog, dtb)
    time_minor = lambda c: jnp.transpose(c.reshape(nseq, WINDOW, KV_WIDTH), (0, 2, 1))
    time_major = lambda c: jnp.transpose(c, (0, 2, 1)).reshape(1, nseq, WINDOW, N_KV_HEADS, HEAD_DIM)
    q_s = ps[:, P_AQ:P_AQ + ATTN_Q_WIDTH].reshape(nseq, N_Q_HEADS, HEAD_DIM)
    attn_s, gdn_s, new_k_s, new_v_s, state_s = _sample_mix(
        sinks, q_s, ps, time_minor(cache_attn_k[l]), time_minor(cache_attn_v[l]), kv_t, qt, kt,
        v_s, beta_s, decay_s, state_gdn[l], ng)
    y_s = _mix_out_ffn(x1s, attn_s.reshape(nseq, ATTN_Q_WIDTH), gdn_s, wmix, g2, wgu2, wo2, gf,
                       nseq)

    return (y_p.reshape(batch, seq, D_MODEL), y_s.reshape(nseq, 1, D_MODEL),
            new_k_p, new_v_p, new_conv_p, state_p[None],
            time_major(new_k_s), time_major(new_v_s),
            new_conv_s.reshape(1, nseq, CONV_WIDTH - 1, GDN_CONV_DIM), state_s[None])
```

```python
import jax
import jax.numpy as jnp
from jax import lax
from jax.experimental import pallas as pl
from jax.experimental.pallas import tpu as pltpu

F32 = jnp.float32
BF16 = jnp.bfloat16

D_MODEL = 1024
D_FF = 2816
N_Q_HEADS = 8
N_KV_HEADS = 2
GROUP = N_Q_HEADS // N_KV_HEADS
HEAD_DIM = 64
WINDOW = 128
ATTN_Q_WIDTH = N_Q_HEADS * HEAD_DIM
KV_WIDTH = N_KV_HEADS * HEAD_DIM
N_GDN_HEADS = 4
GDN_DK = 128
GDN_DV = 128
GDN_QK_WIDTH = N_GDN_HEADS * GDN_DK
GDN_V_WIDTH = N_GDN_HEADS * GDN_DV
GDN_CONV_DIM = 2 * GDN_QK_WIDTH + GDN_V_WIDTH
CONV_WIDTH = 4
CHUNK = 64
EPS = 1e-6
LOG2E = 1.4426950408889634

LANES = 128
SUBLANES = 8
VMEM_LIMIT = 56 * 1024 * 1024

P_GQKV = 0
P_AQ = P_GQKV + GDN_CONV_DIM
P_GZ = P_AQ + ATTN_Q_WIDTH
P_KV = P_GZ + GDN_V_WIDTH
P_GBA = P_KV + 2 * KV_WIDTH
P_WIDTH = P_GBA + LANES

FF_CHUNK = 256
TOKEN_TILE = 512
GDN_TILE = 128
GDN_SEQS = 4
SWA_TILE = 2048
CONV_ROWS = 128
SAMPLE_BLOCK = 16
SAMPLE_GROUP = 4


def _sigmoid(x):
    return 1.0 / (1.0 + jnp.exp(-x))


def _silu(x):
    half = 0.5 * x
    return half + half * jnp.tanh(half)


def _softplus(x):
    return jnp.maximum(x, 0.0) + jnp.log1p(jnp.exp(-jnp.abs(x)))


def _rms(x, g):
    return x * lax.rsqrt(jnp.mean(x * x, axis=-1, keepdims=True) + EPS) * g


def _dot(a, b):
    return jnp.dot(a.astype(BF16), b.astype(BF16), preferred_element_type=F32)


def _dot_nt(a, b):
    return lax.dot_general(a.astype(BF16), b.astype(BF16), (((1,), (1,)), ((), ())),
                           preferred_element_type=F32)


def _resident(shape):
    return pl.BlockSpec(shape, lambda *_: (0,) * len(shape), pipeline_mode=pl.Buffered(1))


def _params(*semantics):
    return pltpu.CompilerParams(dimension_semantics=semantics, vmem_limit_bytes=VMEM_LIMIT)


FF_STEPS = D_FF // FF_CHUNK


def _ffn_half_step(x, g, wgu_ref, wo_ref):
    n = _rms(x, g).astype(BF16)
    hidden = []
    for c in range(FF_STEPS):
        lo = c * FF_CHUNK
        gate = jnp.dot(n, wgu_ref[:, lo:lo + FF_CHUNK], preferred_element_type=F32)
        up = jnp.dot(n, wgu_ref[:, D_FF + lo:D_FF + lo + FF_CHUNK], preferred_element_type=F32)
        hidden.append((_silu(gate) * up).astype(BF16))
    out = jnp.dot(jnp.concatenate(hidden, axis=1), wo_ref[...], preferred_element_type=F32)
    return x + 0.5 * out


_W_AQ, _W_KV = 0, ATTN_Q_WIDTH
_W_GQKV = _W_KV + 2 * KV_WIDTH
_W_GZ = _W_GQKV + GDN_CONV_DIM
_W_GBA = _W_GZ + GDN_V_WIDTH
_PROJ_GROUPS = ((_W_GQKV, GDN_CONV_DIM, P_GQKV), (_W_GBA, LANES, P_GBA), (_W_AQ, ATTN_Q_WIDTH, P_AQ),
                (_W_GZ, GDN_V_WIDTH, P_GZ), (_W_KV, 2 * KV_WIDTH, P_KV))


def _gdn_gates(gba, alog, dtb):
    beta = _sigmoid(gba)
    g = -jnp.exp(alog) * _softplus(gba + dtb)
    return beta, g


def _l2norm(x, scale=None):
    inv = lax.rsqrt(jnp.sum(x * x, axis=-1, keepdims=True) + EPS)
    return x * (inv if scale is None else inv * scale)


def _ffn_mix_in_kernel(x_ref, g1_ref, wgu_ref, wo_ref, gm_ref, win_ref, x1_ref, p_ref):
    x1 = _ffn_half_step(x_ref[...], g1_ref[...], wgu_ref, wo_ref)
    x1_ref[...] = x1
    n = _rms(x1, gm_ref[...]).astype(BF16)
    for src, width, dst in _PROJ_GROUPS:
        p_ref[:, dst:dst + width] = jnp.dot(n, win_ref[:, src:src + width],
                                            preferred_element_type=F32)


def _ffn_weight_specs():
    return [_resident((1, D_MODEL)), _resident((D_MODEL, 2 * D_FF)), _resident((D_FF, D_MODEL)),
            _resident((1, D_MODEL)), _resident((D_MODEL, P_WIDTH))]


def _ffn_mix_in(x, g1, wgu, wo, gm, win, tm):
    n_tok = x.shape[0]
    row = lambda w: pl.BlockSpec((tm, w), lambda i: (i, 0))
    return pl.pallas_call(
        _ffn_mix_in_kernel,
        grid=(n_tok // tm,),
        in_specs=[row(D_MODEL)] + _ffn_weight_specs(),
        out_specs=[row(D_MODEL), row(P_WIDTH)],
        out_shape=[jax.ShapeDtypeStruct((n_tok, D_MODEL), F32),
                   jax.ShapeDtypeStruct((n_tok, P_WIDTH), F32)],
        compiler_params=_params("parallel"),
        name="ffn_mix_in",
    )(x, g1, wgu, wo, gm, win)


def _mix_out_ffn_kernel(x1_ref, attn_ref, gdn_ref, wmix_ref, g2_ref, wgu_ref, wo_ref, gf_ref, y_ref):
    mixed = jnp.concatenate([attn_ref[...].astype(BF16), gdn_ref[...].astype(BF16)], axis=1)
    x2 = x1_ref[...] + jnp.dot(mixed, wmix_ref[...], preferred_element_type=F32)
    x3 = _ffn_half_step(x2, g2_ref[...], wgu_ref, wo_ref)
    y_ref[...] = _rms(x3, gf_ref[...])


def _mix_out_ffn(x1, attn, gdn, wmix, g2, wgu, wo, gf, tm):
    n_tok = x1.shape[0]
    row = lambda w: pl.BlockSpec((tm, w), lambda i: (i, 0))
    return pl.pallas_call(
        _mix_out_ffn_kernel,
        grid=(n_tok // tm,),
        in_specs=[row(D_MODEL), row(ATTN_Q_WIDTH), row(GDN_V_WIDTH),
                  _resident((ATTN_Q_WIDTH + GDN_V_WIDTH, D_MODEL)),
                  _resident((1, D_MODEL)), _resident((D_MODEL, 2 * D_FF)),
                  _resident((D_FF, D_MODEL)), _resident((1, D_MODEL))],
        out_specs=row(D_MODEL),
        out_shape=jax.ShapeDtypeStruct((n_tok, D_MODEL), F32),
        compiler_params=_params("parallel"),
        name="mix_out_ffn",
    )(x1, attn, gdn, wmix, g2, wgu, wo, gf)


def _alibi_slope(head):
    return 2.0 ** (-8.0 * (head + 1) / N_Q_HEADS)


def _swa_prompt_kernel(sink_ref, q_ref, kvc_ref, kvp_ref, o_ref, bias_ref, s_ref):
    blk = pl.program_id(1)
    nkeys = 2 * WINDOW

    @pl.when((pl.program_id(0) == 0) & (blk == 0))
    def _():
        key = lax.broadcasted_iota(jnp.int32, (nkeys, WINDOW), 0)
        qry = lax.broadcasted_iota(jnp.int32, (nkeys, WINDOW), 1)
        dist = qry - key + WINDOW
        valid = (dist >= 0) & (dist <= WINDOW)
        for h in range(N_Q_HEADS):
            g = h % GROUP
            bias_ref[h // GROUP, :, g * WINDOW:(g + 1) * WINDOW] = jnp.where(
                valid, -(LOG2E * _alibi_slope(h)) * dist.astype(F32), -jnp.inf)

    first_pen = jnp.where(blk > 0, 0.0, -jnp.inf)
    scale = LOG2E * HEAD_DIM ** -0.5
    nsub = q_ref.shape[0] // WINDOW
    jobs = [(j, kvh) for j in range(nsub) for kvh in range(N_KV_HEADS)]
    group_heads = lambda kvh: range(kvh * GROUP, (kvh + 1) * GROUP)
    def keys_values(j, col):
        if j == 0:
            return jnp.concatenate([kvp_ref[:, col], kvc_ref[0:WINDOW, col]], axis=0)
        return kvc_ref[(j - 1) * WINDOW:(j + 1) * WINDOW, col]

    values_t = []
    for job, (j, kvh) in enumerate(jobs):
        rows = slice(j * WINDOW, (j + 1) * WINDOW)
        k_all = keys_values(j, slice(kvh * HEAD_DIM, (kvh + 1) * HEAD_DIM)).astype(BF16)
        v_t = keys_values(j, slice(KV_WIDTH + kvh * HEAD_DIM, KV_WIDTH + (kvh + 1) * HEAD_DIM)).T
        values_t.append(jnp.concatenate([v_t, jnp.ones((SUBLANES, 2 * WINDOW), F32)],
                                        axis=0).astype(BF16))
        q_stack = jnp.concatenate(
            [(q_ref[rows, h * HEAD_DIM:(h + 1) * HEAD_DIM] * scale).astype(BF16)
             for h in group_heads(kvh)], axis=0)
        s_ref[job] = lax.dot_general(k_all, q_stack, (((1,), (1,)), ((), ())),
                                     preferred_element_type=F32)
    for job, (j, kvh) in enumerate(jobs):
        rows = slice(j * WINDOW, (j + 1) * WINDOW)
        v_t = values_t[job]
        outs = []
        for g, h in enumerate(group_heads(kvh)):
            lanes = slice(g * WINDOW, (g + 1) * WINDOW)
            s = s_ref[job, :, lanes] + bias_ref[kvh, :, lanes]
            if j == 0:
                s = jnp.concatenate([s[:WINDOW] + first_pen, s[WINDOW:]], axis=0)
            sink = LOG2E * sink_ref[h]
            m = jnp.maximum(jnp.max(s, axis=0, keepdims=True), sink)
            p = jnp.exp2(s - m).astype(BF16)
            pv = jnp.dot(v_t, p, preferred_element_type=F32)
            denom = pv[HEAD_DIM:HEAD_DIM + 1] + jnp.exp2(sink - m)
            outs.append(pv[:HEAD_DIM] / denom)
        for pair in range(GROUP // 2):
            lo = (kvh * GROUP + 2 * pair) * HEAD_DIM
            o_ref[rows, lo:lo + 2 * HEAD_DIM] = jnp.concatenate(outs[2 * pair:2 * pair + 2], axis=0).T


def _swa_prompt(sinks, p3):
    batch, seq, _ = p3.shape
    tq = SWA_TILE
    nsub = tq // WINDOW
    q_blk = P_AQ // ATTN_Q_WIDTH
    kv_blk = P_KV // (2 * KV_WIDTH)
    return pl.pallas_call(
        _swa_prompt_kernel,
        grid=(batch, seq // tq),
        in_specs=[pl.BlockSpec(memory_space=pltpu.SMEM),
                  pl.BlockSpec((None, tq, ATTN_Q_WIDTH), lambda b, i: (b, i, q_blk)),
                  pl.BlockSpec((None, tq, 2 * KV_WIDTH), lambda b, i: (b, i, kv_blk)),
                  pl.BlockSpec((None, WINDOW, 2 * KV_WIDTH),
                               lambda b, i: (b, jnp.maximum(i * nsub - 1, 0), kv_blk))],
        out_specs=pl.BlockSpec((None, tq, ATTN_Q_WIDTH), lambda b, i: (b, i, 0)),
        out_shape=jax.ShapeDtypeStruct((batch, seq, ATTN_Q_WIDTH), F32),
        scratch_shapes=[pltpu.VMEM((N_KV_HEADS, 2 * WINDOW, GROUP * WINDOW), F32),
                        pltpu.VMEM((nsub * N_KV_HEADS, 2 * WINDOW, GROUP * WINDOW), F32)],
        compiler_params=_params("arbitrary", "arbitrary"),
        name="swa_prompt",
    )(sinks, p3, p3, p3)


INV_LEVELS = tuple(2 ** i for i in range(1, CHUNK.bit_length()))


def _gdn_prompt_kernel(raw_ref, prev_ref, gz_ref, gba_ref, convw_ref, alog_ref, dtb_ref, ng_ref,
                       o_ref, s_ref, mask_ref, act_ref):
    t = pl.program_id(1)
    tt = GDN_TILE
    nchunk = tt // CHUNK
    heads = range(N_GDN_HEADS)

    r = lax.broadcasted_iota(jnp.int32, (tt, tt), 0)
    c = lax.broadcasted_iota(jnp.int32, (tt, tt), 1)
    xor_idx = r ^ c

    @pl.when((pl.program_id(0) == 0) & (t == 0))
    def _():
        mask_ref[0] = jnp.where((xor_idx < CHUNK) & (r >= c), 1.0, 0.0).astype(BF16)
        for i, s in enumerate(INV_LEVELS):
            mask_ref[i + 1] = jnp.where((xor_idx >= s // 2) & (xor_idx < s) & (r > c),
                                        1.0, 0.0).astype(BF16)

    @pl.when(t == 0)
    def _():
        s_ref[...] = jnp.zeros(s_ref.shape, F32)

    strict = (xor_idx < CHUNK) & (r > c)
    diag = r == c

    nseq = raw_ref.shape[0]
    chains = [(sq, h) for sq in range(nseq) for h in heads]
    ids = range(len(chains))

    for sq in range(nseq):
        for cg in range(GDN_CONV_DIM // LANES):
            cols = slice(cg * LANES, (cg + 1) * LANES)
            w = convw_ref[:, cols]
            for r0 in range(0, tt, CONV_ROWS):
                if r0 == 0:
                    before = jnp.where(t > 0, prev_ref[sq, :, cols], 0.0)
                else:
                    before = raw_ref[sq, r0 - SUBLANES:r0, cols]
                cur = raw_ref[sq, r0:r0 + CONV_ROWS, cols]
                ext = jnp.concatenate([before, cur], axis=0)
                acc = w[0:1] * ext
                for j in range(1, CONV_WIDTH):
                    acc = pltpu.roll(acc, 1, axis=0) + w[j:j + 1] * ext
                act = _silu(acc[SUBLANES:])
                if cg < GDN_QK_WIDTH // LANES:
                    act = _l2norm(act, GDN_DK ** -0.5)
                elif cg < 2 * GDN_QK_WIDTH // LANES:
                    act = _l2norm(act)
                act_ref[sq, r0:r0 + CONV_ROWS, cols] = act

    gates, gc, gc_t, gtot = [], [], [], []
    for sq in range(nseq):
        beta_all, g_all = _gdn_gates(gba_ref[sq], alog_ref[...], dtb_ref[...])
        g_hi = g_all.astype(BF16)
        g_lo = (g_all - g_hi.astype(F32)).astype(BF16)
        cs = jnp.dot(mask_ref[0], jnp.concatenate([g_hi, g_lo], axis=1),
                     preferred_element_type=F32)
        gc_sq = (cs[:, :LANES] + cs[:, LANES:]) * LOG2E
        gates.append(beta_all)
        gc.append(gc_sq)
        gc_t.append(gc_sq.T)
        gtot.append(jnp.concatenate(
            [jnp.broadcast_to(gc_sq[(ci + 1) * CHUNK - 1:(ci + 1) * CHUNK, :], (CHUNK, LANES))
             for ci in range(nchunk)], axis=0))

    q_l, k_l, v_l, beta_l, gcol_l, gend_l, a_l, qk_l, x_l = [], [], [], [], [], [], [], [], []
    for sq, h in chains:
        q = act_ref[sq, :, h * GDN_DK:(h + 1) * GDN_DK]
        k = act_ref[sq, :, GDN_QK_WIDTH + h * GDN_DK:GDN_QK_WIDTH + (h + 1) * GDN_DK]
        beta = gates[sq][:, h:h + 1]
        gcol = gc[sq][:, N_GDN_HEADS + h:N_GDN_HEADS + h + 1]
        grow = gc_t[sq][N_GDN_HEADS + h:N_GDN_HEADS + h + 1, :]
        decay = jnp.exp2(jnp.where(strict, gcol - grow, -jnp.inf))
        kb = k.astype(BF16)
        qk_kk = _dot_nt(jnp.concatenate([q.astype(BF16), kb], axis=0), kb)
        a = ((beta * qk_kk[tt:]) * decay).astype(BF16)
        qk_l.append((qk_kk[:tt] * jnp.where(diag, 1.0, decay)).astype(BF16))
        x_l.append(jnp.where(diag, 1.0, 0.0).astype(BF16) - a * mask_ref[1])
        q_l.append(q); k_l.append(k); beta_l.append(beta); gcol_l.append(gcol); a_l.append(a)
        gend_l.append(gtot[sq][:, N_GDN_HEADS + h:N_GDN_HEADS + h + 1])
        v_l.append(act_ref[sq, :, 2 * GDN_QK_WIDTH + h * GDN_DV:2 * GDN_QK_WIDTH + (h + 1) * GDN_DV])

    for i in range(1, len(INV_LEVELS)):
        for n in ids:
            x = x_l[n]
            y = jnp.dot(x, a_l[n], preferred_element_type=F32).astype(BF16)
            z = jnp.dot(y, x, preferred_element_type=F32).astype(BF16)
            x_l[n] = x - z * mask_ref[i + 1]

    u_l, w_l, qd_l, kd_l = [], [], [], []
    for n in ids:
        eg = jnp.exp2(gcol_l[n])
        rhs = jnp.concatenate([v_l[n] * beta_l[n], k_l[n] * (beta_l[n] * eg)], axis=1)
        uw = jnp.dot(x_l[n], rhs.astype(BF16), preferred_element_type=F32)
        u_l.append(uw[:, :GDN_DV])
        w_l.append(uw[:, GDN_DV:].astype(BF16))
        qd_l.append((q_l[n] * eg).astype(BF16))
        kd_l.append((k_l[n] * jnp.exp2(gend_l[n] - gcol_l[n])).astype(BF16))

    states = [s_ref[sq, h] for sq, h in chains]
    v_new = [[] for _ in ids]
    o_inter = [[] for _ in ids]
    for ci in range(nchunk):
        rows = slice(ci * CHUNK, (ci + 1) * CHUNK)
        for n in ids:
            sb = states[n].astype(BF16)
            ws_qs = jnp.dot(jnp.concatenate([w_l[n][rows], qd_l[n][rows]], axis=0), sb,
                            preferred_element_type=F32)
            vn = u_l[n][rows] - ws_qs[:CHUNK]
            o_inter[n].append(ws_qs[CHUNK:])
            g_last = jnp.exp2(gend_l[n][ci * CHUNK:ci * CHUNK + 1, :])
            upd = lax.dot_general(kd_l[n][rows], vn.astype(BF16), (((0,), (0,)), ((), ())),
                                  preferred_element_type=F32)
            states[n] = states[n] * g_last + upd
            v_new[n].append(vn)

    for n, (sq, h) in enumerate(chains):
        s_ref[sq, h] = states[n]
        vn_all = jnp.concatenate(v_new[n], axis=0).astype(BF16)
        o = jnp.concatenate(o_inter[n], axis=0) + jnp.dot(qk_l[n], vn_all,
                                                           preferred_element_type=F32)
        gz = gz_ref[sq, :, h * GDN_DV:(h + 1) * GDN_DV]
        o_ref[sq, :, h * GDN_DV:(h + 1) * GDN_DV] = _rms(o, ng_ref[...]) * _silu(gz)


def _gdn_prompt(p3, convw, alog, dtb, ng):
    batch, seq, _ = p3.shape
    tt = GDN_TILE
    ns = GDN_SEQS if batch % GDN_SEQS == 0 else 1
    prev_per_tile = tt // SUBLANES
    small = lambda shape: pl.BlockSpec(shape, lambda b, t: (0,) * len(shape))
    return pl.pallas_call(
        _gdn_prompt_kernel,
        grid=(batch // ns, seq // tt),
        in_specs=[pl.BlockSpec((ns, tt, GDN_CONV_DIM), lambda b, t: (b, t, P_GQKV // GDN_CONV_DIM)),
                  pl.BlockSpec((ns, SUBLANES, GDN_CONV_DIM),
                               lambda b, t: (b, jnp.maximum(t * prev_per_tile - 1, 0),
                                             P_GQKV // GDN_CONV_DIM)),
                  pl.BlockSpec((ns, tt, GDN_V_WIDTH), lambda b, t: (b, t, P_GZ // GDN_V_WIDTH)),
                  pl.BlockSpec((ns, tt, LANES), lambda b, t: (b, t, P_GBA // LANES)),
                  small((CONV_WIDTH, GDN_CONV_DIM)), small((1, LANES)), small((1, LANES)),
                  small((1, GDN_DV))],
        out_specs=[pl.BlockSpec((ns, tt, GDN_V_WIDTH), lambda b, t: (b, t, 0)),
                   pl.BlockSpec((ns, N_GDN_HEADS, GDN_DK, GDN_DV), lambda b, t: (b, 0, 0, 0))],
        out_shape=[jax.ShapeDtypeStruct((batch, seq, GDN_V_WIDTH), F32),
                   jax.ShapeDtypeStruct((batch, N_GDN_HEADS, GDN_DK, GDN_DV), F32)],
        scratch_shapes=[pltpu.VMEM((len(INV_LEVELS) + 1, tt, tt), BF16),
                        pltpu.VMEM((ns, tt, GDN_CONV_DIM), F32)],
        compiler_params=_params("arbitrary", "arbitrary"),
        name="gdn_prompt",
    )(p3, p3, p3, p3, convw, alog, dtb, ng)


def _sample_prep_kernel(p_ref, hist_ref, convw_ref, alog_ref, dtb_ref,
                        conv_ref, qt_ref, kt_ref, v_ref, beta_ref, decay_ref, kvt_ref):
    nblk = qt_ref.shape[0]
    kv_t = p_ref[:, P_KV:P_KV + 2 * KV_WIDTH].T
    for i in range(nblk):
        kvt_ref[i] = kv_t[:, i * SAMPLE_BLOCK:(i + 1) * SAMPLE_BLOCK]
    raw = p_ref[:, P_GQKV:P_GQKV + GDN_CONV_DIM]
    w = convw_ref[...]
    conv = w[0:1] * hist_ref[:, 0:GDN_CONV_DIM]
    for j in range(1, CONV_WIDTH - 1):
        conv = conv + w[j:j + 1] * hist_ref[:, j * GDN_CONV_DIM:(j + 1) * GDN_CONV_DIM]
    conv = conv + w[CONV_WIDTH - 1:CONV_WIDTH] * raw
    act = _silu(conv)
    conv_ref[:, 0:(CONV_WIDTH - 2) * GDN_CONV_DIM] = hist_ref[:, GDN_CONV_DIM:]
    conv_ref[:, (CONV_WIDTH - 2) * GDN_CONV_DIM:] = raw
    v_ref[...] = act[:, 2 * GDN_QK_WIDTH:]
    beta, g = _gdn_gates(p_ref[:, P_GBA:P_GBA + LANES], alog_ref[...], dtb_ref[...])
    beta_ref[...] = beta
    decay_ref[...] = jnp.exp(g)
    for h in range(N_GDN_HEADS):
        qt = _l2norm(act[:, h * GDN_DK:(h + 1) * GDN_DK], GDN_DK ** -0.5).T
        kt = _l2norm(act[:, GDN_QK_WIDTH + h * GDN_DK:GDN_QK_WIDTH + (h + 1) * GDN_DK]).T
        for i in range(nblk):
            qt_ref[i, h] = qt[:, i * SAMPLE_BLOCK:(i + 1) * SAMPLE_BLOCK]
            kt_ref[i, h] = kt[:, i * SAMPLE_BLOCK:(i + 1) * SAMPLE_BLOCK]


def _sample_prep(p, hist, convw, alog, dtb):
    nseq = p.shape[0]
    nblk = nseq // SAMPLE_BLOCK
    hist_w = (CONV_WIDTH - 1) * GDN_CONV_DIM
    cols = jax.ShapeDtypeStruct((nblk, N_GDN_HEADS, GDN_DK, SAMPLE_BLOCK), F32)
    return pl.pallas_call(
        _sample_prep_kernel,
        out_shape=[jax.ShapeDtypeStruct((nseq, hist_w), F32), cols, cols,
                   jax.ShapeDtypeStruct((nseq, GDN_V_WIDTH), F32),
                   jax.ShapeDtypeStruct((nseq, LANES), F32),
                   jax.ShapeDtypeStruct((nseq, LANES), F32),
                   jax.ShapeDtypeStruct((nblk, 2 * KV_WIDTH, SAMPLE_BLOCK), F32)],
        compiler_params=pltpu.CompilerParams(vmem_limit_bytes=VMEM_LIMIT),
        name="sample_prep",
    )(p, hist, convw, alog, dtb)


def _per_head(values):
    h = lax.broadcasted_iota(jnp.int32, (N_Q_HEADS, 1), 0)
    col = jnp.full((N_Q_HEADS, 1), values[N_Q_HEADS - 1], F32)
    for i in range(N_Q_HEADS - 2, -1, -1):
        col = jnp.where(h == i, values[i], col)
    return col


def _sample_mix_kernel(sink_ref, q_ref, p_ref, kc_ref, vc_ref, kvt_ref, qt_ref, kt_ref, v_ref,
                       beta_ref, decay_ref, s_ref, ng_ref, attn_ref, gdn_ref, nk_ref, nv_ref, ns_ref):
    time = lax.broadcasted_iota(jnp.int32, (KV_WIDTH, WINDOW), 1)
    newest = time == WINDOW - 1
    key_pos = lax.broadcasted_iota(jnp.int32, (1, WINDOW), 1)
    dist_hist = (WINDOW - key_pos).astype(F32)
    head = lax.broadcasted_iota(jnp.int32, (N_Q_HEADS, HEAD_DIM), 0)
    kv_of_head = [head // GROUP == kvh for kvh in range(N_KV_HEADS)]
    slope = _per_head([_alibi_slope(h) for h in range(N_Q_HEADS)])
    sink = _per_head([sink_ref[h] for h in range(N_Q_HEADS)])
    scale = HEAD_DIM ** -0.5
    for b in range(SAMPLE_BLOCK):
        k_hist = kc_ref[b]
        v_hist = vc_ref[b]
        k_new = p_ref[b:b + 1, P_KV:P_KV + KV_WIDTH]
        v_new = p_ref[b:b + 1, P_KV + KV_WIDTH:P_KV + 2 * KV_WIDTH]
        nk_ref[b] = jnp.where(newest, kvt_ref[0:KV_WIDTH, b:b + 1],
                              pltpu.roll(k_hist, WINDOW - 1, axis=1))
        nv_ref[b] = jnp.where(newest, kvt_ref[KV_WIDTH:2 * KV_WIDTH, b:b + 1],
                              pltpu.roll(v_hist, WINDOW - 1, axis=1))
        q = q_ref[b]
        q_wide = jnp.concatenate([jnp.where(sel, q, 0.0) for sel in kv_of_head], axis=1)
        s_hist = _dot(q_wide, k_hist) * scale - slope * dist_hist
        s_new = jnp.sum(q_wide * k_new, axis=-1, keepdims=True) * scale
        m = jnp.maximum(jnp.maximum(jnp.max(s_hist, axis=-1, keepdims=True), s_new), sink)
        p_hist = jnp.exp(s_hist - m)
        p_new = jnp.exp(s_new - m)
        denom = jnp.sum(p_hist, axis=-1, keepdims=True) + p_new + jnp.exp(sink - m)
        o_wide = (_dot_nt(p_hist, v_hist) + p_new * v_new) / denom
        o = o_wide[:, 0:HEAD_DIM]
        for kvh in range(1, N_KV_HEADS):
            o = jnp.where(kv_of_head[kvh], o_wide[:, kvh * HEAD_DIM:(kvh + 1) * HEAD_DIM], o)
        attn_ref[b] = o

    heads = range(N_GDN_HEADS)
    for b0 in range(0, SAMPLE_BLOCK, SAMPLE_GROUP):
        units = [(b, h) for b in range(b0, b0 + SAMPLE_GROUP) for h in heads]
        kcol = [jnp.broadcast_to(kt_ref[h, :, b:b + 1], (GDN_DK, GDN_DV)) for b, h in units]
        qcol = [jnp.broadcast_to(qt_ref[h, :, b:b + 1], (GDN_DK, GDN_DV)) for b, h in units]
        decayed = [s_ref[b, h] * decay_ref[b:b + 1, N_GDN_HEADS + h:N_GDN_HEADS + h + 1]
                   for b, h in units]
        ks = [jnp.sum(d * kc, axis=0, keepdims=True) for d, kc in zip(decayed, kcol)]
        delta = [beta_ref[b:b + 1, h:h + 1] * (v_ref[b:b + 1, h * GDN_DV:(h + 1) * GDN_DV] - s)
                 for (b, h), s in zip(units, ks)]
        state = [d + kc * dl for d, kc, dl in zip(decayed, kcol, delta)]
        outs = {}
        for (b, h), st, qc in zip(units, state, qcol):
            ns_ref[b, h] = st
            o = jnp.sum(st * qc, axis=0, keepdims=True)
            gz = p_ref[b:b + 1, P_GZ + h * GDN_DV:P_GZ + (h + 1) * GDN_DV]
            outs.setdefault(b, []).append(_rms(o, ng_ref[...]) * _silu(gz))
        for b, row in outs.items():
            gdn_ref[b:b + 1, :] = jnp.concatenate(row, axis=1)


def _sample_mix(sinks, q, p, k_hist, v_hist, kv_t, qt, kt, v, beta, decay, state, ng):
    nseq = p.shape[0]
    bb = SAMPLE_BLOCK
    rows = lambda w: pl.BlockSpec((bb, w), lambda i: (i, 0))
    heads = pl.BlockSpec((bb, N_Q_HEADS, HEAD_DIM), lambda i: (i, 0, 0))
    cache = pl.BlockSpec((bb, KV_WIDTH, WINDOW), lambda i: (i, 0, 0))
    cols = pl.BlockSpec((None, N_GDN_HEADS, GDN_DK, bb), lambda i: (i, 0, 0, 0))
    kv_cols = pl.BlockSpec((None, 2 * KV_WIDTH, bb), lambda i: (i, 0, 0))
    st = pl.BlockSpec((bb, N_GDN_HEADS, GDN_DK, GDN_DV), lambda i: (i, 0, 0, 0))
    return pl.pallas_call(
        _sample_mix_kernel,
        grid=(nseq // bb,),
        in_specs=[pl.BlockSpec(memory_space=pltpu.SMEM), heads, rows(P_WIDTH), cache, cache,
                  kv_cols, cols, cols, rows(GDN_V_WIDTH), rows(LANES), rows(LANES), st,
                  pl.BlockSpec((1, GDN_DV), lambda i: (0, 0))],
        out_specs=[heads, rows(GDN_V_WIDTH), cache, cache, st],
        out_shape=[jax.ShapeDtypeStruct((nseq, N_Q_HEADS, HEAD_DIM), F32),
                   jax.ShapeDtypeStruct((nseq, GDN_V_WIDTH), F32),
                   jax.ShapeDtypeStruct(k_hist.shape, F32),
                   jax.ShapeDtypeStruct(v_hist.shape, F32),
                   jax.ShapeDtypeStruct(state.shape, F32)],
        compiler_params=_params("parallel"),
        name="sample_mix",
    )(sinks, q, p, k_hist, v_hist, kv_t, qt, kt, v, beta, decay, state, ng)


def _lane_pad(vec, offset):
    return jnp.zeros((1, LANES), F32).at[0, offset:offset + vec.shape[0]].set(vec)


def kernel(x_prompt, x_sample, cache_attn_k, cache_attn_v, state_conv, state_gdn, ffn1_norm_g,
           ffn1_w_in, ffn1_w_out, mix_norm_g, w_in_mix, attn_sinks, conv_w, gdn_A_log, gdn_dt_bias,
           gdn_norm_g, w_out_mix, ffn2_norm_g, ffn2_w_in, ffn2_w_out, final_norm_g):
    depth = ffn1_w_in.shape[0]
    assert depth == 1, "single-layer trunk"
    batch, seq, _ = x_prompt.shape
    nseq = x_sample.shape[0]
    assert x_sample.shape[1] == 1 and cache_attn_k.shape[2] == WINDOW
    assert seq % GDN_TILE == 0 and seq % WINDOW == 0 and nseq % SAMPLE_BLOCK == 0
    l = 0
    row = lambda v: v.reshape(1, -1)
    g1, gm, g2, gf = row(ffn1_norm_g[l]), row(mix_norm_g[l]), row(ffn2_norm_g[l]), row(final_norm_g)
    ng = row(gdn_norm_g[l])
    wgu1, wo1 = ffn1_w_in[l].astype(BF16), ffn1_w_out[l].astype(BF16)
    wgu2, wo2 = ffn2_w_in[l].astype(BF16), ffn2_w_out[l].astype(BF16)
    win = jnp.pad(w_in_mix[l].astype(BF16), ((0, 0), (0, P_WIDTH - w_in_mix.shape[2])))
    wmix = w_out_mix[l].astype(BF16)
    alog = _lane_pad(gdn_A_log[l], N_GDN_HEADS)
    dtb = _lane_pad(gdn_dt_bias[l], N_GDN_HEADS)
    sinks = attn_sinks[l]
    convw = conv_w[l]

    xp = x_prompt.reshape(batch * seq, D_MODEL)
    x1p, pp = _ffn_mix_in(xp, g1, wgu1, wo1, gm, win, TOKEN_TILE)
    pp3 = pp.reshape(batch, seq, P_WIDTH)
    attn_p = _swa_prompt(sinks, pp3)
    gdn_p, state_p = _gdn_prompt(pp3, convw, alog, dtb, ng)
    y_p = _mix_out_ffn(x1p, attn_p.reshape(batch * seq, ATTN_Q_WIDTH),
                       gdn_p.reshape(batch * seq, GDN_V_WIDTH), wmix, g2, wgu2, wo2, gf, TOKEN_TILE)
    tail = pp3[:, seq - WINDOW:, P_KV:P_KV + 2 * KV_WIDTH]
    new_k_p = tail[:, :, :KV_WIDTH].reshape(1, batch, WINDOW, N_KV_HEADS, HEAD_DIM)
    new_v_p = tail[:, :, KV_WIDTH:].reshape(1, batch, WINDOW, N_KV_HEADS, HEAD_DIM)
    new_conv_p = pp3[:, seq - (CONV_WIDTH - 1):, P_GQKV:P_GQKV + GDN_CONV_DIM][None]

    xs = x_sample.reshape(nseq, D_MODEL)
    x1s, ps = _ffn_mix_in(xs, g1, wgu1, wo1, gm, win, nseq)
    hist = state_conv[l].reshape(nseq, (CONV_WIDTH - 1) * GDN_CONV_DIM)
    new_conv_s, qt, kt, v_s, beta_s, decay_s, kv_t = _sample_prep(ps, hist, convw, alog, dtb)
    time_minor = lambda c: jnp.transpose(c.reshape(nseq, WINDOW, KV_WIDTH), (0, 2, 1))
    time_major = lambda c: jnp.transpose(c, (0, 2, 1)).reshape(1, nseq, WINDOW, N_KV_HEADS, HEAD_DIM)
    q_s = ps[:, P_AQ:P_AQ + ATTN_Q_WIDTH].reshape(nseq, N_Q_HEADS, HEAD_DIM)
    attn_s, gdn_s, new_k_s, new_v_s, state_s = _sample_mix(
        sinks, q_s, ps, time_minor(cache_attn_k[l]), time_minor(cache_attn_v[l]), kv_t, qt, kt,
        v_s, beta_s, decay_s, state_gdn[l], ng)
    y_s = _mix_out_ffn(x1s, attn_s.reshape(nseq, ATTN_Q_WIDTH), gdn_s, wmix, g2, wgu2, wo2, gf,
                       nseq)

    return (y_p.reshape(batch, seq, D_MODEL), y_s.reshape(nseq, 1, D_MODEL),
            new_k_p, new_v_p, new_conv_p, state_p[None],
            time_major(new_k_s), time_major(new_v_s),
            new_conv_s.reshape(1, nseq, CONV_WIDTH - 1, GDN_CONV_DIM), state_s[None])
```

```python
import jax
import jax.numpy as jnp
from jax import lax
from jax.experimental import pallas as pl
from jax.experimental.pallas import tpu as pltpu

F32 = jnp.float32
BF16 = jnp.bfloat16

D_MODEL = 1024
D_FF = 2816
N_Q_HEADS = 8
N_KV_HEADS = 2
GROUP = N_Q_HEADS // N_KV_HEADS
HEAD_DIM = 64
WINDOW = 128
ATTN_Q_WIDTH = N_Q_HEADS * HEAD_DIM
KV_WIDTH = N_KV_HEADS * HEAD_DIM
N_GDN_HEADS = 4
GDN_DK = 128
GDN_DV = 128
GDN_QK_WIDTH = N_GDN_HEADS * GDN_DK
GDN_V_WIDTH = N_GDN_HEADS * GDN_DV
GDN_CONV_DIM = 2 * GDN_QK_WIDTH + GDN_V_WIDTH
CONV_WIDTH = 4
CHUNK = 64
EPS = 1e-6
LOG2E = 1.4426950408889634

LANES = 128
SUBLANES = 8
VMEM_LIMIT = 56 * 1024 * 1024

P_GQKV = 0
P_AQ = P_GQKV + GDN_CONV_DIM
P_GZ = P_AQ + ATTN_Q_WIDTH
P_KV = P_GZ + GDN_V_WIDTH
P_GBA = P_KV + 2 * KV_WIDTH
P_WIDTH = P_GBA + LANES

FF_CHUNK = 256
TOKEN_TILE = 512
GDN_TILE = 128
GDN_SEQS = 4
SWA_TILE = 2048
CONV_ROWS = 128
SAMPLE_BLOCK = 16
SAMPLE_GROUP = 8


def _sigmoid(x):
    return 1.0 / (1.0 + jnp.exp(-x))


def _silu(x):
    half = 0.5 * x
    return half + half * jnp.tanh(half)


def _softplus(x):
    return jnp.maximum(x, 0.0) + jnp.log1p(jnp.exp(-jnp.abs(x)))


def _rms(x, g):
    return x * lax.rsqrt(jnp.mean(x * x, axis=-1, keepdims=True) + EPS) * g


def _dot(a, b):
    return jnp.dot(a.astype(BF16), b.astype(BF16), preferred_element_type=F32)


def _dot_nt(a, b):
    return lax.dot_general(a.astype(BF16), b.astype(BF16), (((1,), (1,)), ((), ())),
                           preferred_element_type=F32)


def _resident(shape):
    return pl.BlockSpec(shape, lambda *_: (0,) * len(shape), pipeline_mode=pl.Buffered(1))


def _params(*semantics):
    return pltpu.CompilerParams(dimension_semantics=semantics, vmem_limit_bytes=VMEM_LIMIT)


FF_STEPS = D_FF // FF_CHUNK


def _ffn_half_step(x, g, wgu_ref, wo_ref):
    n = _rms(x, g).astype(BF16)
    hidden = []
    for c in range(FF_STEPS):
        lo = c * FF_CHUNK
        gate = jnp.dot(n, wgu_ref[:, lo:lo + FF_CHUNK], preferred_element_type=F32)
        up = jnp.dot(n, wgu_ref[:, D_FF + lo:D_FF + lo + FF_CHUNK], preferred_element_type=F32)
        hidden.append((_silu(gate) * up).astype(BF16))
    out = jnp.dot(jnp.concatenate(hidden, axis=1), wo_ref[...], preferred_element_type=F32)
    return x + 0.5 * out


_W_AQ, _W_KV = 0, ATTN_Q_WIDTH
_W_GQKV = _W_KV + 2 * KV_WIDTH
_W_GZ = _W_GQKV + GDN_CONV_DIM
_W_GBA = _W_GZ + GDN_V_WIDTH
_PROJ_GROUPS = ((_W_GQKV, GDN_CONV_DIM, P_GQKV), (_W_GBA, LANES, P_GBA), (_W_AQ, ATTN_Q_WIDTH, P_AQ),
                (_W_GZ, GDN_V_WIDTH, P_GZ), (_W_KV, 2 * KV_WIDTH, P_KV))


def _gdn_gates(gba, alog, dtb):
    beta = _sigmoid(gba)
    g = -jnp.exp(alog) * _softplus(gba + dtb)
    return beta, g


def _l2norm(x, scale=None):
    inv = lax.rsqrt(jnp.sum(x * x, axis=-1, keepdims=True) + EPS)
    return x * (inv if scale is None else inv * scale)


def _ffn_mix_in_kernel(x_ref, g1_ref, wgu_ref, wo_ref, gm_ref, win_ref, x1_ref, p_ref):
    x1 = _ffn_half_step(x_ref[...], g1_ref[...], wgu_ref, wo_ref)
    x1_ref[...] = x1
    n = _rms(x1, gm_ref[...]).astype(BF16)
    for src, width, dst in _PROJ_GROUPS:
        p_ref[:, dst:dst + width] = jnp.dot(n, win_ref[:, src:src + width],
                                            preferred_element_type=F32)


def _ffn_weight_specs():
    return [_resident((1, D_MODEL)), _resident((D_MODEL, 2 * D_FF)), _resident((D_FF, D_MODEL)),
            _resident((1, D_MODEL)), _resident((D_MODEL, P_WIDTH))]


def _ffn_mix_in(x, g1, wgu, wo, gm, win, tm):
    n_tok = x.shape[0]
    row = lambda w: pl.BlockSpec((tm, w), lambda i: (i, 0))
    return pl.pallas_call(
        _ffn_mix_in_kernel,
        grid=(n_tok // tm,),
        in_specs=[row(D_MODEL)] + _ffn_weight_specs(),
        out_specs=[row(D_MODEL), row(P_WIDTH)],
        out_shape=[jax.ShapeDtypeStruct((n_tok, D_MODEL), F32),
                   jax.ShapeDtypeStruct((n_tok, P_WIDTH), F32)],
        compiler_params=_params("parallel"),
        name="ffn_mix_in",
    )(x, g1, wgu, wo, gm, win)


def _mix_out_ffn_kernel(x1_ref, attn_ref, gdn_ref, wmix_ref, g2_ref, wgu_ref, wo_ref, gf_ref, y_ref):
    mixed = jnp.concatenate([attn_ref[...].astype(BF16), gdn_ref[...].astype(BF16)], axis=1)
    x2 = x1_ref[...] + jnp.dot(mixed, wmix_ref[...], preferred_element_type=F32)
    x3 = _ffn_half_step(x2, g2_ref[...], wgu_ref, wo_ref)
    y_ref[...] = _rms(x3, gf_ref[...])


def _mix_out_ffn(x1, attn, gdn, wmix, g2, wgu, wo, gf, tm):
    n_tok = x1.shape[0]
    row = lambda w: pl.BlockSpec((tm, w), lambda i: (i, 0))
    return pl.pallas_call(
        _mix_out_ffn_kernel,
        grid=(n_tok // tm,),
        in_specs=[row(D_MODEL), row(ATTN_Q_WIDTH), row(GDN_V_WIDTH),
                  _resident((ATTN_Q_WIDTH + GDN_V_WIDTH, D_MODEL)),
                  _resident((1, D_MODEL)), _resident((D_MODEL, 2 * D_FF)),
                  _resident((D_FF, D_MODEL)), _resident((1, D_MODEL))],
        out_specs=row(D_MODEL),
        out_shape=jax.ShapeDtypeStruct((n_tok, D_MODEL), F32),
        compiler_params=_params("parallel"),
        name="mix_out_ffn",
    )(x1, attn, gdn, wmix, g2, wgu, wo, gf)


def _alibi_slope(head):
    return 2.0 ** (-8.0 * (head + 1) / N_Q_HEADS)


def _swa_prompt_kernel(sink_ref, q_ref, kvc_ref, kvp_ref, o_ref, bias_ref, s_ref):
    blk = pl.program_id(1)
    nkeys = 2 * WINDOW

    @pl.when((pl.program_id(0) == 0) & (blk == 0))
    def _():
        key = lax.broadcasted_iota(jnp.int32, (nkeys, WINDOW), 0)
        qry = lax.broadcasted_iota(jnp.int32, (nkeys, WINDOW), 1)
        dist = qry - key + WINDOW
        valid = (dist >= 0) & (dist <= WINDOW)
        for h in range(N_Q_HEADS):
            g = h % GROUP
            bias_ref[h // GROUP, :, g * WINDOW:(g + 1) * WINDOW] = jnp.where(
                valid, -(LOG2E * _alibi_slope(h)) * dist.astype(F32), -jnp.inf)

    first_pen = jnp.where(blk > 0, 0.0, -jnp.inf)
    scale = LOG2E * HEAD_DIM ** -0.5
    nsub = q_ref.shape[0] // WINDOW
    jobs = [(j, kvh) for j in range(nsub) for kvh in range(N_KV_HEADS)]
    group_heads = lambda kvh: range(kvh * GROUP, (kvh + 1) * GROUP)
    def keys_values(j, col):
        if j == 0:
            return jnp.concatenate([kvp_ref[:, col], kvc_ref[0:WINDOW, col]], axis=0)
        return kvc_ref[(j - 1) * WINDOW:(j + 1) * WINDOW, col]

    values_t = []
    for job, (j, kvh) in enumerate(jobs):
        rows = slice(j * WINDOW, (j + 1) * WINDOW)
        k_all = keys_values(j, slice(kvh * HEAD_DIM, (kvh + 1) * HEAD_DIM)).astype(BF16)
        v_t = keys_values(j, slice(KV_WIDTH + kvh * HEAD_DIM, KV_WIDTH + (kvh + 1) * HEAD_DIM)).T
        values_t.append(jnp.concatenate([v_t, jnp.ones((SUBLANES, 2 * WINDOW), F32)],
                                        axis=0).astype(BF16))
        q_stack = jnp.concatenate(
            [(q_ref[rows, h * HEAD_DIM:(h + 1) * HEAD_DIM] * scale).astype(BF16)
             for h in group_heads(kvh)], axis=0)
        s_ref[job] = lax.dot_general(k_all, q_stack, (((1,), (1,)), ((), ())),
                                     preferred_element_type=F32)
    for job, (j, kvh) in enumerate(jobs):
        rows = slice(j * WINDOW, (j + 1) * WINDOW)
        v_t = values_t[job]
        outs = []
        for g, h in enumerate(group_heads(kvh)):
            lanes = slice(g * WINDOW, (g + 1) * WINDOW)
            s = s_ref[job, :, lanes] + bias_ref[kvh, :, lanes]
            if j == 0:
                s = jnp.concatenate([s[:WINDOW] + first_pen, s[WINDOW:]], axis=0)
            sink = LOG2E * sink_ref[h]
            m = jnp.maximum(jnp.max(s, axis=0, keepdims=True), sink)
            p = jnp.exp2(s - m).astype(BF16)
            pv = jnp.dot(v_t, p, preferred_element_type=F32)
            denom = pv[HEAD_DIM:HEAD_DIM + 1] + jnp.exp2(sink - m)
            outs.append(pv[:HEAD_DIM] / denom)
        for pair in range(GROUP // 2):
            lo = (kvh * GROUP + 2 * pair) * HEAD_DIM
            o_ref[rows, lo:lo + 2 * HEAD_DIM] = jnp.concatenate(outs[2 * pair:2 * pair + 2], axis=0).T


def _swa_prompt(sinks, p3):
    batch, seq, _ = p3.shape
    tq = SWA_TILE
    nsub = tq // WINDOW
    q_blk = P_AQ // ATTN_Q_WIDTH
    kv_blk = P_KV // (2 * KV_WIDTH)
    return pl.pallas_call(
        _swa_prompt_kernel,
        grid=(batch, seq // tq),
        in_specs=[pl.BlockSpec(memory_space=pltpu.SMEM),
                  pl.BlockSpec((None, tq, ATTN_Q_WIDTH), lambda b, i: (b, i, q_blk)),
                  pl.BlockSpec((None, tq, 2 * KV_WIDTH), lambda b, i: (b, i, kv_blk)),
                  pl.BlockSpec((None, WINDOW, 2 * KV_WIDTH),
                               lambda b, i: (b, jnp.maximum(i * nsub - 1, 0), kv_blk))],
        out_specs=pl.BlockSpec((None, tq, ATTN_Q_WIDTH), lambda b, i: (b, i, 0)),
        out_shape=jax.ShapeDtypeStruct((batch, seq, ATTN_Q_WIDTH), F32),
        scratch_shapes=[pltpu.VMEM((N_KV_HEADS, 2 * WINDOW, GROUP * WINDOW), F32),
                        pltpu.VMEM((nsub * N_KV_HEADS, 2 * WINDOW, GROUP * WINDOW), F32)],
        compiler_params=_params("arbitrary", "arbitrary"),
        name="swa_prompt",
    )(sinks, p3, p3, p3)


INV_LEVELS = tuple(2 ** i for i in range(1, CHUNK.bit_length()))


def _gdn_prompt_kernel(raw_ref, prev_ref, gz_ref, gba_ref, convw_ref, alog_ref, dtb_ref, ng_ref,
                       o_ref, s_ref, mask_ref, act_ref):
    t = pl.program_id(1)
    tt = GDN_TILE
    nchunk = tt // CHUNK
    heads = range(N_GDN_HEADS)

    r = lax.broadcasted_iota(jnp.int32, (tt, tt), 0)
    c = lax.broadcasted_iota(jnp.int32, (tt, tt), 1)
    xor_idx = r ^ c

    @pl.when((pl.program_id(0) == 0) & (t == 0))
    def _():
        mask_ref[0] = jnp.where((xor_idx < CHUNK) & (r >= c), 1.0, 0.0).astype(BF16)
        for i, s in enumerate(INV_LEVELS):
            mask_ref[i + 1] = jnp.where((xor_idx >= s // 2) & (xor_idx < s) & (r > c),
                                        1.0, 0.0).astype(BF16)

    @pl.when(t == 0)
    def _():
        s_ref[...] = jnp.zeros(s_ref.shape, F32)

    strict = (xor_idx < CHUNK) & (r > c)
    diag = r == c

    nseq = raw_ref.shape[0]
    chains = [(sq, h) for sq in range(nseq) for h in heads]
    ids = range(len(chains))

    for sq in range(nseq):
        for cg in range(GDN_CONV_DIM // LANES):
            cols = slice(cg * LANES, (cg + 1) * LANES)
            w = convw_ref[:, cols]
            for r0 in range(0, tt, CONV_ROWS):
                if r0 == 0:
                    before = jnp.where(t > 0, prev_ref[sq, :, cols], 0.0)
                else:
                    before = raw_ref[sq, r0 - SUBLANES:r0, cols]
                cur = raw_ref[sq, r0:r0 + CONV_ROWS, cols]
                ext = jnp.concatenate([before, cur], axis=0)
                acc = w[0:1] * ext
                for j in range(1, CONV_WIDTH):
                    acc = pltpu.roll(acc, 1, axis=0) + w[j:j + 1] * ext
                act = _silu(acc[SUBLANES:])
                if cg < GDN_QK_WIDTH // LANES:
                    act = _l2norm(act, GDN_DK ** -0.5)
                elif cg < 2 * GDN_QK_WIDTH // LANES:
                    act = _l2norm(act)
                act_ref[sq, r0:r0 + CONV_ROWS, cols] = act

    gates, gc, gc_t, gtot = [], [], [], []
    for sq in range(nseq):
        beta_all, g_all = _gdn_gates(gba_ref[sq], alog_ref[...], dtb_ref[...])
        g_hi = g_all.astype(BF16)
        g_lo = (g_all - g_hi.astype(F32)).astype(BF16)
        cs = jnp.dot(mask_ref[0], jnp.concatenate([g_hi, g_lo], axis=1),
                     preferred_element_type=F32)
        gc_sq = (cs[:, :LANES] + cs[:, LANES:]) * LOG2E
        gates.append(beta_all)
        gc.append(gc_sq)
        gc_t.append(gc_sq.T)
        gtot.append(jnp.concatenate(
            [jnp.broadcast_to(gc_sq[(ci + 1) * CHUNK - 1:(ci + 1) * CHUNK, :], (CHUNK, LANES))
             for ci in range(nchunk)], axis=0))

    q_l, k_l, v_l, beta_l, gcol_l, gend_l, a_l, qk_l, x_l = [], [], [], [], [], [], [], [], []
    for sq, h in chains:
        q = act_ref[sq, :, h * GDN_DK:(h + 1) * GDN_DK]
        k = act_ref[sq, :, GDN_QK_WIDTH + h * GDN_DK:GDN_QK_WIDTH + (h + 1) * GDN_DK]
        beta = gates[sq][:, h:h + 1]
        gcol = gc[sq][:, N_GDN_HEADS + h:N_GDN_HEADS + h + 1]
        grow = gc_t[sq][N_GDN_HEADS + h:N_GDN_HEADS + h + 1, :]
        decay = jnp.exp2(jnp.where(strict, gcol - grow, -jnp.inf))
        kb = k.astype(BF16)
        qk_kk = _dot_nt(jnp.concatenate([q.astype(BF16), kb], axis=0), kb)
        a = ((beta * qk_kk[tt:]) * decay).astype(BF16)
        qk_l.append((qk_kk[:tt] * jnp.where(diag, 1.0, decay)).astype(BF16))
        x_l.append(jnp.where(diag, 1.0, 0.0).astype(BF16) - a * mask_ref[1])
        q_l.append(q); k_l.append(k); beta_l.append(beta); gcol_l.append(gcol); a_l.append(a)
        gend_l.append(gtot[sq][:, N_GDN_HEADS + h:N_GDN_HEADS + h + 1])
        v_l.append(act_ref[sq, :, 2 * GDN_QK_WIDTH + h * GDN_DV:2 * GDN_QK_WIDTH + (h + 1) * GDN_DV])

    for i in range(1, len(INV_LEVELS)):
        for n in ids:
            x = x_l[n]
            y = jnp.dot(x, a_l[n], preferred_element_type=F32).astype(BF16)
            z = jnp.dot(y, x, preferred_element_type=F32).astype(BF16)
            x_l[n] = x - z * mask_ref[i + 1]

    u_l, w_l, qd_l, kd_l = [], [], [], []
    for n in ids:
        eg = jnp.exp2(gcol_l[n])
        rhs = jnp.concatenate([v_l[n] * beta_l[n], k_l[n] * (beta_l[n] * eg)], axis=1)
        uw = jnp.dot(x_l[n], rhs.astype(BF16), preferred_element_type=F32)
        u_l.append(uw[:, :GDN_DV])
        w_l.append(uw[:, GDN_DV:].astype(BF16))
        qd_l.append((q_l[n] * eg).astype(BF16))
        kd_l.append((k_l[n] * jnp.exp2(gend_l[n] - gcol_l[n])).astype(BF16))

    states = [s_ref[sq, h] for sq, h in chains]
    v_new = [[] for _ in ids]
    o_inter = [[] for _ in ids]
    for ci in range(nchunk):
        rows = slice(ci * CHUNK, (ci + 1) * CHUNK)
        for n in ids:
            sb = states[n].astype(BF16)
            ws_qs = jnp.dot(jnp.concatenate([w_l[n][rows], qd_l[n][rows]], axis=0), sb,
                            preferred_element_type=F32)
            vn = u_l[n][rows] - ws_qs[:CHUNK]
            o_inter[n].append(ws_qs[CHUNK:])
            g_last = jnp.exp2(gend_l[n][ci * CHUNK:ci * CHUNK + 1, :])
            upd = lax.dot_general(kd_l[n][rows], vn.astype(BF16), (((0,), (0,)), ((), ())),
                                  preferred_element_type=F32)
            states[n] = states[n] * g_last + upd
            v_new[n].append(vn)

    for n, (sq, h) in enumerate(chains):
        s_ref[sq, h] = states[n]
        vn_all = jnp.concatenate(v_new[n], axis=0).astype(BF16)
        o = jnp.concatenate(o_inter[n], axis=0) + jnp.dot(qk_l[n], vn_all,
                                                           preferred_element_type=F32)
        gz = gz_ref[sq, :, h * GDN_DV:(h + 1) * GDN_DV]
        o_ref[sq, :, h * GDN_DV:(h + 1) * GDN_DV] = _rms(o, ng_ref[...]) * _silu(gz)


def _gdn_prompt(p3, convw, alog, dtb, ng):
    batch, seq, _ = p3.shape
    tt = GDN_TILE
    ns = GDN_SEQS if batch % GDN_SEQS == 0 else 1
    prev_per_tile = tt // SUBLANES
    small = lambda shape: pl.BlockSpec(shape, lambda b, t: (0,) * len(shape))
    return pl.pallas_call(
        _gdn_prompt_kernel,
        grid=(batch // ns, seq // tt),
        in_specs=[pl.BlockSpec((ns, tt, GDN_CONV_DIM), lambda b, t: (b, t, P_GQKV // GDN_CONV_DIM)),
                  pl.BlockSpec((ns, SUBLANES, GDN_CONV_DIM),
                               lambda b, t: (b, jnp.maximum(t * prev_per_tile - 1, 0),
                                             P_GQKV // GDN_CONV_DIM)),
                  pl.BlockSpec((ns, tt, GDN_V_WIDTH), lambda b, t: (b, t, P_GZ // GDN_V_WIDTH)),
                  pl.BlockSpec((ns, tt, LANES), lambda b, t: (b, t, P_GBA // LANES)),
                  small((CONV_WIDTH, GDN_CONV_DIM)), small((1, LANES)), small((1, LANES)),
                  small((1, GDN_DV))],
        out_specs=[pl.BlockSpec((ns, tt, GDN_V_WIDTH), lambda b, t: (b, t, 0)),
                   pl.BlockSpec((ns, N_GDN_HEADS, GDN_DK, GDN_DV), lambda b, t: (b, 0, 0, 0))],
        out_shape=[jax.ShapeDtypeStruct((batch, seq, GDN_V_WIDTH), F32),
                   jax.ShapeDtypeStruct((batch, N_GDN_HEADS, GDN_DK, GDN_DV), F32)],
        scratch_shapes=[pltpu.VMEM((len(INV_LEVELS) + 1, tt, tt), BF16),
                        pltpu.VMEM((ns, tt, GDN_CONV_DIM), F32)],
        compiler_params=_params("arbitrary", "arbitrary"),
        name="gdn_prompt",
    )(p3, p3, p3, p3, convw, alog, dtb, ng)


def _sample_prep_kernel(p_ref, hist_ref, convw_ref, alog_ref, dtb_ref,
                        conv_ref, qt_ref, kt_ref, v_ref, beta_ref, decay_ref, kvt_ref):
    nblk = qt_ref.shape[0]
    kv_t = p_ref[:, P_KV:P_KV + 2 * KV_WIDTH].T
    for i in range(nblk):
        kvt_ref[i] = kv_t[:, i * SAMPLE_BLOCK:(i + 1) * SAMPLE_BLOCK]
    raw = p_ref[:, P_GQKV:P_GQKV + GDN_CONV_DIM]
    w = convw_ref[...]
    conv = w[0:1] * hist_ref[:, 0:GDN_CONV_DIM]
    for j in range(1, CONV_WIDTH - 1):
        conv = conv + w[j:j + 1] * hist_ref[:, j * GDN_CONV_DIM:(j + 1) * GDN_CONV_DIM]
    conv = conv + w[CONV_WIDTH - 1:CONV_WIDTH] * raw
    act = _silu(conv)
    conv_ref[:, 0:(CONV_WIDTH - 2) * GDN_CONV_DIM] = hist_ref[:, GDN_CONV_DIM:]
    conv_ref[:, (CONV_WIDTH - 2) * GDN_CONV_DIM:] = raw
    v_ref[...] = act[:, 2 * GDN_QK_WIDTH:]
    beta, g = _gdn_gates(p_ref[:, P_GBA:P_GBA + LANES], alog_ref[...], dtb_ref[...])
    beta_ref[...] = beta
    decay_ref[...] = jnp.exp(g)
    for h in range(N_GDN_HEADS):
        qt = _l2norm(act[:, h * GDN_DK:(h + 1) * GDN_DK], GDN_DK ** -0.5).T
        kt = _l2norm(act[:, GDN_QK_WIDTH + h * GDN_DK:GDN_QK_WIDTH + (h + 1) * GDN_DK]).T
        for i in range(nblk):
            qt_ref[i, h] = qt[:, i * SAMPLE_BLOCK:(i + 1) * SAMPLE_BLOCK]
            kt_ref[i, h] = kt[:, i * SAMPLE_BLOCK:(i + 1) * SAMPLE_BLOCK]


def _sample_prep(p, hist, convw, alog, dtb):
    nseq = p.shape[0]
    nblk = nseq // SAMPLE_BLOCK
    hist_w = (CONV_WIDTH - 1) * GDN_CONV_DIM
    cols = jax.ShapeDtypeStruct((nblk, N_GDN_HEADS, GDN_DK, SAMPLE_BLOCK), F32)
    return pl.pallas_call(
        _sample_prep_kernel,
        out_shape=[jax.ShapeDtypeStruct((nseq, hist_w), F32), cols, cols,
                   jax.ShapeDtypeStruct((nseq, GDN_V_WIDTH), F32),
                   jax.ShapeDtypeStruct((nseq, LANES), F32),
                   jax.ShapeDtypeStruct((nseq, LANES), F32),
                   jax.ShapeDtypeStruct((nblk, 2 * KV_WIDTH, SAMPLE_BLOCK), F32)],
        compiler_params=pltpu.CompilerParams(vmem_limit_bytes=VMEM_LIMIT),
        name="sample_prep",
    )(p, hist, convw, alog, dtb)


def _per_head(values):
    h = lax.broadcasted_iota(jnp.int32, (N_Q_HEADS, 1), 0)
    col = jnp.full((N_Q_HEADS, 1), values[N_Q_HEADS - 1], F32)
    for i in range(N_Q_HEADS - 2, -1, -1):
        col = jnp.where(h == i, values[i], col)
    return col


def _sample_mix_kernel(sink_ref, q_ref, p_ref, kc_ref, vc_ref, kvt_ref, qt_ref, kt_ref, v_ref,
                       beta_ref, decay_ref, s_ref, ng_ref, attn_ref, gdn_ref, nk_ref, nv_ref, ns_ref):
    time = lax.broadcasted_iota(jnp.int32, (KV_WIDTH, WINDOW), 1)
    newest = time == WINDOW - 1
    key_pos = lax.broadcasted_iota(jnp.int32, (1, WINDOW), 1)
    dist_hist = (WINDOW - key_pos).astype(F32)
    head = lax.broadcasted_iota(jnp.int32, (N_Q_HEADS, HEAD_DIM), 0)
    kv_of_head = [head // GROUP == kvh for kvh in range(N_KV_HEADS)]
    slope = _per_head([_alibi_slope(h) for h in range(N_Q_HEADS)])
    sink = _per_head([sink_ref[h] for h in range(N_Q_HEADS)])
    scale = HEAD_DIM ** -0.5
    for b in range(SAMPLE_BLOCK):
        k_hist = kc_ref[b]
        v_hist = vc_ref[b]
        k_new = p_ref[b:b + 1, P_KV:P_KV + KV_WIDTH]
        v_new = p_ref[b:b + 1, P_KV + KV_WIDTH:P_KV + 2 * KV_WIDTH]
        nk_ref[b] = jnp.where(newest, kvt_ref[0:KV_WIDTH, b:b + 1],
                              pltpu.roll(k_hist, WINDOW - 1, axis=1))
        nv_ref[b] = jnp.where(newest, kvt_ref[KV_WIDTH:2 * KV_WIDTH, b:b + 1],
                              pltpu.roll(v_hist, WINDOW - 1, axis=1))
        q = q_ref[b]
        q_wide = jnp.concatenate([jnp.where(sel, q, 0.0) for sel in kv_of_head], axis=1)
        s_hist = _dot(q_wide, k_hist) * scale - slope * dist_hist
        s_new = jnp.sum(q_wide * k_new, axis=-1, keepdims=True) * scale
        m = jnp.maximum(jnp.maximum(jnp.max(s_hist, axis=-1, keepdims=True), s_new), sink)
        p_hist = jnp.exp(s_hist - m)
        p_new = jnp.exp(s_new - m)
        denom = jnp.sum(p_hist, axis=-1, keepdims=True) + p_new + jnp.exp(sink - m)
        o_wide = (_dot_nt(p_hist, v_hist) + p_new * v_new) / denom
        o = o_wide[:, 0:HEAD_DIM]
        for kvh in range(1, N_KV_HEADS):
            o = jnp.where(kv_of_head[kvh], o_wide[:, kvh * HEAD_DIM:(kvh + 1) * HEAD_DIM], o)
        attn_ref[b] = o

    heads = range(N_GDN_HEADS)
    for b0 in range(0, SAMPLE_BLOCK, SAMPLE_GROUP):
        units = [(b, h) for b in range(b0, b0 + SAMPLE_GROUP) for h in heads]
        kcol = [jnp.broadcast_to(kt_ref[h, :, b:b + 1], (GDN_DK, GDN_DV)) for b, h in units]
        qcol = [jnp.broadcast_to(qt_ref[h, :, b:b + 1], (GDN_DK, GDN_DV)) for b, h in units]
        decayed = [s_ref[b, h] * decay_ref[b:b + 1, N_GDN_HEADS + h:N_GDN_HEADS + h + 1]
                   for b, h in units]
        ks = [jnp.sum(d * kc, axis=0, keepdims=True) for d, kc in zip(decayed, kcol)]
        delta = [beta_ref[b:b + 1, h:h + 1] * (v_ref[b:b + 1, h * GDN_DV:(h + 1) * GDN_DV] - s)
                 for (b, h), s in zip(units, ks)]
        state = [d + kc * dl for d, kc, dl in zip(decayed, kcol, delta)]
        outs = {}
        for (b, h), st, qc in zip(units, state, qcol):
            ns_ref[b, h] = st
            o = jnp.sum(st * qc, axis=0, keepdims=True)
            gz = p_ref[b:b + 1, P_GZ + h * GDN_DV:P_GZ + (h + 1) * GDN_DV]
            outs.setdefault(b, []).append(_rms(o, ng_ref[...]) * _silu(gz))
        for b, row in outs.items():
            gdn_ref[b:b + 1, :] = jnp.concatenate(row, axis=1)


def _sample_mix(sinks, q, p, k_hist, v_hist, kv_t, qt, kt, v, beta, decay, state, ng):
    nseq = p.shape[0]
    bb = SAMPLE_BLOCK
    rows = lambda w: pl.BlockSpec((bb, w), lambda i: (i, 0))
    heads = pl.BlockSpec((bb, N_Q_HEADS, HEAD_DIM), lambda i: (i, 0, 0))
    cache = pl.BlockSpec((bb, KV_WIDTH, WINDOW), lambda i: (i, 0, 0))
    cols = pl.BlockSpec((None, N_GDN_HEADS, GDN_DK, bb), lambda i: (i, 0, 0, 0))
    kv_cols = pl.BlockSpec((None, 2 * KV_WIDTH, bb), lambda i: (i, 0, 0))
    st = pl.BlockSpec((bb, N_GDN_HEADS, GDN_DK, GDN_DV), lambda i: (i, 0, 0, 0))
    return pl.pallas_call(
        _sample_mix_kernel,
        grid=(nseq // bb,),
        in_specs=[pl.BlockSpec(memory_space=pltpu.SMEM), heads, rows(P_WIDTH), cache, cache,
                  kv_cols, cols, cols, rows(GDN_V_WIDTH), rows(LANES), rows(LANES), st,
                  pl.BlockSpec((1, GDN_DV), lambda i: (0, 0))],
        out_specs=[heads, rows(GDN_V_WIDTH), cache, cache, st],
        out_shape=[jax.ShapeDtypeStruct((nseq, N_Q_HEADS, HEAD_DIM), F32),
                   jax.ShapeDtypeStruct((nseq, GDN_V_WIDTH), F32),
                   jax.ShapeDtypeStruct(k_hist.shape, F32),
                   jax.ShapeDtypeStruct(v_hist.shape, F32),
                   jax.ShapeDtypeStruct(state.shape, F32)],
        compiler_params=_params("parallel"),
        name="sample_mix",
    )(sinks, q, p, k_hist, v_hist, kv_t, qt, kt, v, beta, decay, state, ng)


def _lane_pad(vec, offset):
    return jnp.zeros((1, LANES), F32).at[0, offset:offset + vec.shape[0]].set(vec)


def kernel(x_prompt, x_sample, cache_attn_k, cache_attn_v, state_conv, state_gdn, ffn1_norm_g,
           ffn1_w_in, ffn1_w_out, mix_norm_g, w_in_mix, attn_sinks, conv_w, gdn_A_log, gdn_dt_bias,
           gdn_norm_g, w_out_mix, ffn2_norm_g, ffn2_w_in, ffn2_w_out, final_norm_g):
    depth = ffn1_w_in.shape[0]
    assert depth == 1, "single-layer trunk"
    batch, seq, _ = x_prompt.shape
    nseq = x_sample.shape[0]
    assert x_sample.shape[1] == 1 and cache_attn_k.shape[2] == WINDOW
    assert seq % GDN_TILE == 0 and seq % WINDOW == 0 and nseq % SAMPLE_BLOCK == 0
    l = 0
    row = lambda v: v.reshape(1, -1)
    g1, gm, g2, gf = row(ffn1_norm_g[l]), row(mix_norm_g[l]), row(ffn2_norm_g[l]), row(final_norm_g)
    ng = row(gdn_norm_g[l])
    wgu1, wo1 = ffn1_w_in[l].astype(BF16), ffn1_w_out[l].astype(BF16)
    wgu2, wo2 = ffn2_w_in[l].astype(BF16), ffn2_w_out[l].astype(BF16)
    win = jnp.pad(w_in_mix[l].astype(BF16), ((0, 0), (0, P_WIDTH - w_in_mix.shape[2])))
    wmix = w_out_mix[l].astype(BF16)
    alog = _lane_pad(gdn_A_log[l], N_GDN_HEADS)
    dtb = _lane_pad(gdn_dt_bias[l], N_GDN_HEADS)
    sinks = attn_sinks[l]
    convw = conv_w[l]

    xp = x_prompt.reshape(batch * seq, D_MODEL)
    x1p, pp = _ffn_mix_in(xp, g1, wgu1, wo1, gm, win, TOKEN_TILE)
    pp3 = pp.reshape(batch, seq, P_WIDTH)
    attn_p = _swa_prompt(sinks, pp3)
    gdn_p, state_p = _gdn_prompt(pp3, convw, alog, dtb, ng)
    y_p = _mix_out_ffn(x1p, attn_p.reshape(batch * seq, ATTN_Q_WIDTH),
                       gdn_p.reshape(batch * seq, GDN_V_WIDTH), wmix, g2, wgu2, wo2, gf, TOKEN_TILE)
    tail = pp3[:, seq - WINDOW:, P_KV:P_KV + 2 * KV_WIDTH]
    new_k_p = tail[:, :, :KV_WIDTH].reshape(1, batch, WINDOW, N_KV_HEADS, HEAD_DIM)
    new_v_p = tail[:, :, KV_WIDTH:].reshape(1, batch, WINDOW, N_KV_HEADS, HEAD_DIM)
    new_conv_p = pp3[:, seq - (CONV_WIDTH - 1):, P_GQKV:P_GQKV + GDN_CONV_DIM][None]

    xs = x_sample.reshape(nseq, D_MODEL)
    x1s, ps = _ffn_mix_in(xs, g1, wgu1, wo1, gm, win, nseq)
    hist = state_conv[l].reshape(nseq, (CONV_WIDTH - 1) * GDN_CONV_DIM)
    new_conv_s, qt, kt, v_s, beta_s, decay_s, kv_t = _sample_prep(ps, hist, convw, alog, dtb)
    time_minor = lambda c: jnp.transpose(c.reshape(nseq, WINDOW, KV_WIDTH), (0, 2, 1))
    time_major = lambda c: jnp.transpose(c, (0, 2, 1)).reshape(1, nseq, WINDOW, N_KV_HEADS, HEAD_DIM)
    q_s = ps[:, P_AQ:P_AQ + ATTN_Q_WIDTH].reshape(nseq, N_Q_HEADS, HEAD_DIM)
    attn_s, gdn_s, new_k_s, new_v_s, state_s = _sample_mix(
        sinks, q_s, ps, time_minor(cache_attn_k[l]), time_minor(cache_attn_v[l]), kv_t, qt, kt,
        v_s, beta_s, decay_s, state_gdn[l], ng)
    y_s = _mix_out_ffn(x1s, attn_s.reshape(nseq, ATTN_Q_WIDTH), gdn_s, wmix, g2, wgu2, wo2, gf,
                       nseq)

    return (y_p.reshape(batch, seq, D_MODEL), y_s.reshape(nseq, 1, D_MODEL),
            new_k_p, new_v_p, new_conv_p, state_p[None],
            time_major(new_k_s), time_major(new_v_s),
            new_conv_s.reshape(1, nseq, CONV_WIDTH - 1, GDN_CONV_DIM), state_s[None])
```
